```python
import math
import jax, jax.numpy as jnp
from jax import lax
import numpy as np

D_MODEL = 1024
BATCH = 2
SEQ = 8192
DEPTH = 2
DEC_BATCH = 32
DEC_SEQ = 8
PAST_LEN = 16384
PAGE_SIZE = 128

N_A = DEPTH // 2
N_B = DEPTH - N_A
CHUNK = 128
D_SG = 3 * D_MODEL
N_SG_GROUPS = 8
D_FF = ((8 * D_MODEL // 3 + 127) // 128) * 128
CONV_W = 3
HEAD_DIM = 64
N_HEADS = D_MODEL // HEAD_DIM
N_KV = 4
GROUP = N_HEADS // N_KV
BLK = 64
N_SEL = 16
WINDOW = 512
CMP_HID = 2 * HEAD_DIM
NUM_BUCKETS = 32
REL_MAX_DIST = 1024
Q_BLOCK = 128
ALPHA = (2 * DEPTH) ** 0.25
BETA = (8 * DEPTH) ** -0.25
LN_EPS = 1e-5
NEG_INF = -1e30
FORCED_SCORE = 1e9

kernel_name = 'yoco_sgu_nsa_convffn_decoder_step'


def layer_norm(x, g, b):
    xf = x.astype(jnp.float32)
    mu = xf.mean(-1, keepdims=True)
    var = jnp.square(xf - mu).mean(-1, keepdims=True)
    return ((xf - mu) * lax.rsqrt(var + LN_EPS) * g + b).astype(x.dtype)


def gelu(x):
    return jax.nn.gelu(x, approximate=False)


def spatial_gate(v, w_s, b_s):
    bsz, L, _ = v.shape
    cl = min(CHUNK, L)
    n_ch = -(-L // cl)
    vp = jnp.pad(v, ((0, 0), (0, n_ch * cl - L), (0, 0)))
    vc = vp.reshape(bsz, n_ch, cl, N_SG_GROUPS, D_SG // N_SG_GROUPS)
    w = jnp.tril(w_s[:, :cl, :cl])
    out = jnp.einsum('gts,bnsgc->bntgc', w, vc) + b_s[:, :cl].T[None, None, :, :, None]
    return out.reshape(bsz, n_ch * cl, D_SG)[:, :L]


def sgu_mixer(x, w_in, b_in, g_v, b_v, w_s, b_s, w_out, b_out):
    u, v = jnp.split(gelu(x @ w_in + b_in), 2, axis=-1)
    v = layer_norm(v, g_v, b_v)
    return (u * spatial_gate(v, w_s, b_s)) @ w_out + b_out, v


def conv_ffn(x, conv_state, w_up, b_up, w_dw, b_dw, w_down, b_down):
    a, g = jnp.split(x @ w_up + b_up, 2, axis=-1)
    L = a.shape[1]
    ap = jnp.concatenate([conv_state.astype(a.dtype), a], axis=1)
    c = b_dw
    for k in range(CONV_W):
        c = c + ap[:, k:k + L] * w_dw[k]
    return (gelu(c) * g) @ w_down + b_down, ap[:, L:]


def t5_bucket(dist):
    n = jnp.maximum(dist, 0)
    max_exact = NUM_BUCKETS // 2
    nf = jnp.maximum(n, 1).astype(jnp.float32)
    large = max_exact + (jnp.log(nf / max_exact) / math.log(REL_MAX_DIST / max_exact)
                         * (NUM_BUCKETS - max_exact)).astype(jnp.int32)
    return jnp.where(n < max_exact, n, jnp.minimum(large, NUM_BUCKETS - 1))


def bias_shared(table, dist):
    q, k = dist.shape
    b = table[t5_bucket(dist)].reshape(q, k, N_KV, GROUP)
    return jnp.transpose(b, (0, 2, 3, 1))[None]


def bias_per_group(table, dist):
    tbl = table.reshape(NUM_BUCKETS, N_KV, GROUP)
    b = tbl[t5_bucket(dist), jnp.arange(N_KV)[None, None, :, None]]
    return jnp.swapaxes(b, -1, -2)


def masked_softmax(logits, mask):
    logits = jnp.where(mask, logits, NEG_INF)
    p = jnp.exp(logits - logits.max(-1, keepdims=True)) * mask
    return p / jnp.maximum(p.sum(-1, keepdims=True), 1e-30)


def compress_blocks(rows, pe, w1, b1, w2, b2):
    w1r = w1.reshape(2, BLK, HEAD_DIM, CMP_HID)
    pre = (jnp.einsum('bnlkgd,kldh->bnkgh', rows, w1r)
           + (jnp.einsum('kld,kldh->kh', pe, w1r) + b1)[:, None, :])
    return jnp.einsum('bnkgh,khd->bnkgd', gelu(pre), w2) + b2[:, None, :]


def nsa_core(q, gates, qpos, kvc, cpos, n_blk, select, kvw, kwpos, rel_bias):
    f32 = jnp.float32
    scale = HEAD_DIM ** -0.5
    lc = jnp.einsum('bqgrd,bcgd->bqgrc', q, kvc[:, :, 0], preferred_element_type=f32) * scale
    lc = lc + bias_shared(rel_bias, qpos[:, None] - cpos[None, :])
    pc = masked_softmax(lc, (cpos[None, :] <= qpos[:, None])[None, :, None, None, :])
    o_c = jnp.einsum('bqgrc,bcgd->bqgrd', pc, kvc[:, :, 1])
    imp = jnp.pad(pc.sum(axis=3), ((0, 0), (0, 0), (0, 0), (0, n_blk - pc.shape[-1])))
    blk = jnp.arange(n_blk)
    cur = (qpos // BLK)[:, None]
    forced = ((blk == 0) | (blk == cur) | (blk == cur - 1))[None, :, None, :]
    valid = (blk <= cur)[None, :, None, :]
    score = jnp.where(forced, FORCED_SCORE, jnp.where(valid, imp, -1.0))
    _, idx = lax.top_k(score, min(N_SEL, n_blk))
    ks, vs, kpos, kval = select(idx)
    ls = jnp.einsum('bqgrd,bqgkd->bqgrk', q, ks, preferred_element_type=f32) * scale
    ls = ls + bias_per_group(rel_bias, qpos[None, :, None, None] - kpos)
    ps = masked_softmax(ls, (kval & (kpos <= qpos[None, :, None, None]))[:, :, :, None, :])
    o_s = jnp.einsum('bqgrk,bqgkd->bqgrd', ps, vs)
    dw = qpos[:, None] - kwpos[None, :]
    lw = jnp.einsum('bqgrd,bkgd->bqgrk', q, kvw[:, :, 0], preferred_element_type=f32) * scale
    lw = lw + bias_shared(rel_bias, dw)
    pw = masked_softmax(lw, ((dw >= 0) & (dw < WINDOW) & (kwpos >= 0)[None, :])[None, :, None, None, :])
    o_w = jnp.einsum('bqgrk,bkgd->bqgrd', pw, kvw[:, :, 1])
    return gates[..., 0:1] * o_c + gates[..., 1:2] * o_s + gates[..., 2:3] * o_w


def run_trunk(x, conv_state, prepare, attend, p):
    bsz, L = x.shape[:2]
    new_conv, sg_v = [], []
    ctx, kv_rows = None, None
    for l in range(DEPTH):
        if l < N_A:
            y, v = sgu_mixer(x, p['sg_w_in'][l], p['sg_b_in'][l], p['sg_ln_g'][l], p['sg_ln_b'][l],
                             p['sg_w_s'][l], p['sg_b_s'][l], p['sg_w_out'][l], p['sg_b_out'][l])
            sg_v.append(v)
        else:
            i = l - N_A
            qg = x @ p['nsa_w_qg'][i] + p['nsa_b_qg'][i]
            q = qg[..., :N_HEADS * HEAD_DIM].reshape(bsz, L, N_KV, GROUP, HEAD_DIM)
            gates = jax.nn.sigmoid(qg[..., N_HEADS * HEAD_DIM:].astype(jnp.float32)).reshape(bsz, L, N_KV, GROUP, 3)
            o = attend(q, gates, ctx)
            y = o.reshape(bsz, L, N_HEADS * HEAD_DIM).astype(x.dtype) @ p['nsa_w_o'][i] + p['nsa_b_o'][i]
        x = layer_norm(ALPHA * x + y, p['ln_g'][l, 0], p['ln_b'][l, 0])
        y, cs = conv_ffn(x, conv_state[l], p['ffn_w_up'][l], p['ffn_b_up'][l], p['ffn_w_dw'][l],
                         p['ffn_b_dw'][l], p['ffn_w_down'][l], p['ffn_b_down'][l])
        new_conv.append(cs)
        x = layer_norm(ALPHA * x + y, p['ln_g'][l, 1], p['ln_b'][l, 1])
        if l == N_A - 1:
            kv_rows = (x @ p['kv_w']).reshape(bsz, L, 6, N_KV, HEAD_DIM)
            ctx = prepare(kv_rows)
    return x, jnp.stack(new_conv), jnp.stack(sg_v), kv_rows


def setup_inputs(seed: int = 0) -> dict:
    key = jax.random.key(seed)
    ks = iter(jax.random.split(key, 48))

    def nrm(shape, scale):
        return jax.random.normal(next(ks), shape, jnp.float32) * scale

    n_pages = PAST_LEN // PAGE_SIZE
    n_phys = (DEC_BATCH * n_pages * 5) // 4
    win_buf = min(WINDOW, PAST_LEN)
    kvd = (2, N_KV, HEAD_DIM)
    qg_dim = N_HEADS * HEAD_DIM + 3 * N_HEADS
    page_table = jax.random.permutation(next(ks), n_phys)[:DEC_BATCH * n_pages].reshape(DEC_BATCH, n_pages).astype(jnp.int32)
    return {
        'x_prompt': nrm((BATCH, SEQ, D_MODEL), 1.0),
        'x_sample': nrm((DEC_BATCH, DEC_SEQ, D_MODEL), 1.0),
        'cache_kv_cmp': nrm((n_phys, PAGE_SIZE) + kvd, 1.0),
        'cache_kv_slc': nrm((n_phys, PAGE_SIZE) + kvd, 1.0),
        'cache_win': nrm((DEC_BATCH, win_buf) + kvd, 1.0),
        'state_conv': nrm((DEPTH, DEC_BATCH, CONV_W - 1, D_FF), 1.0),
        'page_table': page_table,
        'ln_g': 1.0 + nrm((DEPTH, 2, D_MODEL), 0.02),
        'ln_b': nrm((DEPTH, 2, D_MODEL), 0.02),
        'sg_w_in': nrm((N_A, D_MODEL, 2 * D_SG), D_MODEL ** -0.5),
        'sg_b_in': nrm((N_A, 2 * D_SG), 0.02),
        'sg_ln_g': 1.0 + nrm((N_A, D_SG), 0.02),
        'sg_ln_b': nrm((N_A, D_SG), 0.02),
        'sg_w_s': nrm((N_A, N_SG_GROUPS, CHUNK, CHUNK), CHUNK ** -0.5),
        'sg_b_s': 1.0 + nrm((N_A, N_SG_GROUPS, CHUNK), 0.02),
        'sg_w_out': nrm((N_A, D_SG, D_MODEL), BETA * D_SG ** -0.5),
        'sg_b_out': nrm((N_A, D_MODEL), 0.02),
        'ffn_w_up': nrm((DEPTH, D_MODEL, 2 * D_FF), D_MODEL ** -0.5),
        'ffn_b_up': nrm((DEPTH, 2 * D_FF), 0.02),
        'ffn_w_dw': nrm((DEPTH, CONV_W, D_FF), CONV_W ** -0.5),
        'ffn_b_dw': nrm((DEPTH, D_FF), 0.02),
        'ffn_w_down': nrm((DEPTH, D_FF, D_MODEL), BETA * D_FF ** -0.5),
        'ffn_b_down': nrm((DEPTH, D_MODEL), 0.02),
        'kv_w': nrm((D_MODEL, 6 * N_KV * HEAD_DIM), D_MODEL ** -0.5),
        'cmp_pe': nrm((2, BLK, HEAD_DIM), 0.1),
        'cmp_w1': nrm((2, BLK * HEAD_DIM, CMP_HID), (BLK * HEAD_DIM) ** -0.5),
        'cmp_b1': nrm((2, CMP_HID), 0.02),
        'cmp_w2': nrm((2, CMP_HID, HEAD_DIM), CMP_HID ** -0.5),
        'cmp_b2': nrm((2, HEAD_DIM), 0.02),
        'nsa_w_qg': nrm((N_B, D_MODEL, qg_dim), D_MODEL ** -0.5),
        'nsa_b_qg': nrm((N_B, qg_dim), 0.02),
        'nsa_w_o': nrm((N_B, N_HEADS * HEAD_DIM, D_MODEL), BETA * (N_HEADS * HEAD_DIM) ** -0.5),
        'nsa_b_o': nrm((N_B, D_MODEL), 0.02),
        'rel_bias': nrm((NUM_BUCKETS, N_HEADS), 0.5),
    }


def reference(x_prompt, x_sample, cache_kv_cmp, cache_kv_slc, cache_win, state_conv, page_table,
              ln_g, ln_b, sg_w_in, sg_b_in, sg_ln_g, sg_ln_b, sg_w_s, sg_b_s, sg_w_out, sg_b_out,
              ffn_w_up, ffn_b_up, ffn_w_dw, ffn_b_dw, ffn_w_down, ffn_b_down,
              kv_w, cmp_pe, cmp_w1, cmp_b1, cmp_w2, cmp_b2,
              nsa_w_qg, nsa_b_qg, nsa_w_o, nsa_b_o, rel_bias):
    p = dict(ln_g=ln_g, ln_b=ln_b, sg_w_in=sg_w_in, sg_b_in=sg_b_in, sg_ln_g=sg_ln_g, sg_ln_b=sg_ln_b,
             sg_w_s=sg_w_s, sg_b_s=sg_b_s, sg_w_out=sg_w_out, sg_b_out=sg_b_out,
             ffn_w_up=ffn_w_up, ffn_b_up=ffn_b_up, ffn_w_dw=ffn_w_dw, ffn_b_dw=ffn_b_dw,
             ffn_w_down=ffn_w_down, ffn_b_down=ffn_b_down, kv_w=kv_w,
             nsa_w_qg=nsa_w_qg, nsa_b_qg=nsa_b_qg, nsa_w_o=nsa_w_o, nsa_b_o=nsa_b_o)
    gi = jnp.arange(N_KV)[None, None, :, None]

    bsz, S, _ = x_prompt.shape
    nb_p = S // BLK

    def prepare_prompt(kv):
        kvc = compress_blocks(kv[:, :, 0:2].reshape(bsz, nb_p, BLK, 2, N_KV, HEAD_DIM),
                              cmp_pe, cmp_w1, cmp_b1, cmp_w2, cmp_b2)
        kvb = kv[:, :, 2:4].reshape(bsz, nb_p, BLK, 2, N_KV, HEAD_DIM)
        kvw_pad = jnp.pad(kv[:, :, 4:6], ((0, 0), (WINDOW, 0), (0, 0), (0, 0), (0, 0)))
        return kvc, kvb, kvw_pad

    def attend_prompt(q, gates, ctx):
        kvc, kvb, kvw_pad = ctx
        nqb = S // Q_BLOCK
        cpos = jnp.arange(nb_p) * BLK + BLK - 1
        bi = jnp.arange(bsz)[:, None, None, None]

        def select(idx):
            lead, n = idx.shape[:3], idx.shape[-1]
            kv = kvb[bi, idx, :, :, gi].reshape(lead + (n * BLK, 2, HEAD_DIM))
            kpos = (idx[..., None] * BLK + jnp.arange(BLK)).reshape(lead + (n * BLK,))
            return kv[..., 0, :], kv[..., 1, :], kpos, jnp.ones(kpos.shape, dtype=bool)

        def block(args):
            qb, gb, i = args
            start = i * Q_BLOCK
            qpos = start + jnp.arange(Q_BLOCK)
            kvw = lax.dynamic_slice_in_dim(kvw_pad, start, Q_BLOCK + WINDOW, axis=1)
            kwpos = start - WINDOW + jnp.arange(Q_BLOCK + WINDOW)
            return nsa_core(qb, gb, qpos, kvc, cpos, nb_p, select, kvw, kwpos, rel_bias)

        qs = jnp.swapaxes(q.reshape(bsz, nqb, Q_BLOCK, N_KV, GROUP, HEAD_DIM), 0, 1)
        gs = jnp.swapaxes(gates.reshape(bsz, nqb, Q_BLOCK, N_KV, GROUP, 3), 0, 1)
        o = lax.map(block, (qs, gs, jnp.arange(nqb)))
        return jnp.swapaxes(o, 0, 1).reshape(bsz, S, N_KV, GROUP, HEAD_DIM)

    y_p, conv_p, _, kv_p = run_trunk(x_prompt, jnp.zeros((DEPTH, bsz, CONV_W - 1, D_FF), x_prompt.dtype),
                                     prepare_prompt, attend_prompt, p)

    dbsz, L, _ = x_sample.shape
    n_pages = page_table.shape[1]
    past_len = n_pages * PAGE_SIZE
    npb = past_len // BLK
    bpp = PAGE_SIZE // BLK
    n_newc = L // BLK
    n_blk_s = npb + -(-L // BLK)
    win_buf = cache_win.shape[1]
    pool_blocks = cache_kv_slc.reshape(-1, BLK, 2, N_KV, HEAD_DIM)

    def prepare_sample(kv):
        past = cache_kv_cmp[page_table].reshape(dbsz, npb, BLK, 2, N_KV, HEAD_DIM)
        new = kv[:, :n_newc * BLK, 0:2].reshape(dbsz, n_newc, BLK, 2, N_KV, HEAD_DIM)
        kvc = jnp.concatenate([compress_blocks(past, cmp_pe, cmp_w1, cmp_b1, cmp_w2, cmp_b2),
                               compress_blocks(new, cmp_pe, cmp_w1, cmp_b1, cmp_w2, cmp_b2).astype(past.dtype)], axis=1)
        kvw = jnp.concatenate([cache_win, kv[:, :, 4:6].astype(cache_win.dtype)], axis=1)
        return kvc, kv[:, :, 2:4], kvw

    def attend_sample(q, gates, ctx):
        kvc, kv_new, kvw = ctx
        qpos = past_len + jnp.arange(L)
        cpos = jnp.arange(kvc.shape[1]) * BLK + BLK - 1
        kwpos = jnp.concatenate([past_len - win_buf + jnp.arange(win_buf), qpos])
        bi = jnp.arange(dbsz)[:, None, None, None]
        new_blk = npb + jnp.arange(L) // BLK

        def select(idx):
            lead, n = idx.shape[:3], idx.shape[-1]
            j = jnp.minimum(idx, npb - 1)
            phys = page_table[bi, j // bpp] * bpp + j % bpp
            kv = pool_blocks[phys, :, :, gi].reshape(lead + (n * BLK, 2, HEAD_DIM))
            kpos = (idx[..., None] * BLK + jnp.arange(BLK)).reshape(lead + (n * BLK,))
            kval = jnp.broadcast_to((idx < npb)[..., None], idx.shape + (BLK,)).reshape(lead + (n * BLK,))
            chosen = jnp.any(idx[..., :, None] == new_blk, axis=-2)
            kv_n = jnp.broadcast_to(jnp.transpose(kv_new, (0, 3, 1, 2, 4))[:, None], lead + (L, 2, HEAD_DIM))
            kv = jnp.concatenate([kv, kv_n.astype(kv.dtype)], axis=3)
            kpos = jnp.concatenate([kpos, jnp.broadcast_to(qpos, lead + (L,))], axis=-1)
            kval = jnp.concatenate([kval, chosen], axis=-1)
            return kv[..., 0, :], kv[..., 1, :], kpos, kval

        return nsa_core(q, gates, qpos, kvc, cpos, n_blk_s, select, kvw, kwpos, rel_bias)

    y_s, conv_s, sgv_s, kv_s = run_trunk(x_sample, state_conv, prepare_sample, attend_sample, p)

    win_p = kv_p[:, S - min(WINDOW, S):, 4:6]
    return (y_p, y_s, kv_p[:, :, 0:2], kv_p[:, :, 2:4], win_p, conv_p,
            kv_s[:, :, 0:2], kv_s[:, :, 2:4], kv_s[:, :, 4:6], conv_s, sgv_s)
```

```python
import functools
import math

import numpy as np
import jax
import jax.numpy as jnp
from jax import lax
from jax.experimental import pallas as pl
from jax.experimental.pallas import tpu as pltpu

N_SG_GROUPS = 8
CHUNK = 128
CONV_W = 3
HEAD_DIM = 64
N_KV = 4
BLK = 64
N_SEL = 16
WINDOW = 512
NUM_BUCKETS = 32
REL_MAX_DIST = 1024
Q_BLOCK = 128
PAGE_SIZE = 128
DEPTH = 2
ALPHA = (2 * DEPTH) ** 0.25
LN_EPS = 1e-5
NEG_INF = -1e30
FORCED_SCORE = 1e9

MXU_DTYPE = jnp.bfloat16
F32 = jnp.float32
LANES = 128
ROW_TILE = 256
VMEM_LIMIT = 56 * 1024 * 1024


def _bucket_thresholds():
    d = np.arange(0, 4 * REL_MAX_DIST)
    max_exact = NUM_BUCKETS // 2
    nf = np.maximum(d, 1).astype(np.float32)
    large = max_exact + (np.log(nf / np.float32(max_exact)) / np.float32(math.log(REL_MAX_DIST / max_exact))
                         * np.float32(NUM_BUCKETS - max_exact)).astype(np.int32)
    b = np.where(d < max_exact, d, np.minimum(large, NUM_BUCKETS - 1))
    assert np.all(np.diff(b) >= 0) and b[-1] == NUM_BUCKETS - 1
    return [int(np.nonzero(b >= j)[0][0]) for j in range(NUM_BUCKETS)]


BUCKET_THR = _bucket_thresholds()
FAR_DIST = BUCKET_THR[-1]
N_BIAS_TILES = -(-(FAR_DIST + LANES) // LANES)


def _gelu(x):
    return 0.5 * x * (1.0 + lax.erf(x * (1.0 / math.sqrt(2.0))))


def _layer_norm(x, g, b):
    mu = jnp.mean(x, axis=-1, keepdims=True)
    xc = x - mu
    var = jnp.mean(xc * xc, axis=-1, keepdims=True)
    return xc * lax.rsqrt(var + LN_EPS) * g + b


def _mm(a, b):
    return jnp.dot(a, b, preferred_element_type=F32)


def _mm_nt(a, b):
    return lax.dot_general(a, b, (((1,), (1,)), ((), ())), preferred_element_type=F32)


def _const_spec(shape):
    nd = len(shape)
    return pl.BlockSpec(shape, lambda *_: (0,) * nd, pipeline_mode=pl.Buffered(1))


def _params(sem):
    return pltpu.CompilerParams(dimension_semantics=sem, vmem_limit_bytes=VMEM_LIMIT)


def _bucket_of(dist):
    b = jnp.zeros(dist.shape, jnp.int32)
    for j in range(1, NUM_BUCKETS):
        b = b + (dist >= BUCKET_THR[j]).astype(jnp.int32)
    return b


def _sgu_kernel(x_ref, win_ref, bin_ref, gv_ref, bv_ref, wg_ref, bs_ref, wout_ref, bout_ref, lng_ref, lnb_ref,
                o_ref, v_ref, *, d_sg):
    x = x_ref[...]
    xb = x.astype(MXU_DTYPE)
    v = _gelu(_mm(xb, win_ref[:, d_sg:]) + bin_ref[:, d_sg:])
    vn = _layer_norm(v, gv_ref[...], bv_ref[...])
    v_ref[...] = vn
    cg = d_sg // N_SG_GROUPS
    acc = jnp.zeros(o_ref.shape, F32)
    for g in range(N_SG_GROUPS):
        lo, hi = g * cg, (g + 1) * cg
        gate = _mm(wg_ref[g], vn[:, lo:hi].astype(MXU_DTYPE)) + bs_ref[g]
        u = _gelu(_mm(xb, win_ref[:, lo:hi]) + bin_ref[:, lo:hi])
        acc = acc + _mm((u * gate).astype(MXU_DTYPE), wout_ref[lo:hi, :])
    y = acc + bout_ref[...]
    o_ref[...] = _layer_norm(ALPHA * x + y, lng_ref[...], lnb_ref[...])


def _sgu_layer(x, seq_len, w_in, b_in, g_v, b_v, w_s, b_s, w_out, b_out, ln_g, ln_b):
    t, d = x.shape
    d_sg = w_out.shape[0]
    cg = d_sg // N_SG_GROUPS
    cl = min(CHUNK, seq_len)
    assert seq_len % cl == 0 and ROW_TILE % cl == 0 and t % ROW_TILE == 0
    tm = ROW_TILE
    w_tri = jnp.tril(w_s[:, :cl, :cl])
    eye = jnp.eye(tm // cl, dtype=w_s.dtype)
    wg = jnp.einsum('ab,gts->gatbs', eye, w_tri).reshape(N_SG_GROUPS, tm, tm).astype(MXU_DTYPE)
    bs = jnp.broadcast_to(jnp.tile(b_s[:, :cl], (1, tm // cl))[:, :, None], (N_SG_GROUPS, tm, cg))
    row = lambda a: a.reshape(1, -1)
    out, v = pl.pallas_call(
        functools.partial(_sgu_kernel, d_sg=d_sg),
        grid=(t // tm,),
        in_specs=[pl.BlockSpec((tm, d), lambda i: (i, 0)),
                  _const_spec((d, 2 * d_sg)), _const_spec((1, 2 * d_sg)), _const_spec((1, d_sg)), _const_spec((1, d_sg)),
                  _const_spec((N_SG_GROUPS, tm, tm)), _const_spec((N_SG_GROUPS, tm, cg)),
                  _const_spec((d_sg, d)), _const_spec((1, d)), _const_spec((1, d)), _const_spec((1, d))],
        out_specs=[pl.BlockSpec((tm, d), lambda i: (i, 0)), pl.BlockSpec((tm, d_sg), lambda i: (i, 0))],
        out_shape=[jax.ShapeDtypeStruct((t, d), F32), jax.ShapeDtypeStruct((t, d_sg), F32)],
        compiler_params=_params(("parallel",)),
        name="sgu_layer",
    )(x, w_in.astype(MXU_DTYPE), row(b_in), row(g_v), row(b_v), wg, bs, w_out.astype(MXU_DTYPE), row(b_out),
      row(ln_g), row(ln_b))
    return out, v


def _ffn_tail(x, a, gt, s1, s2, wdw_ref, bdw_ref, wdown_ref, bdown_ref, lng_ref, lnb_ref, o_ref):
    c = bdw_ref[...] + s2 * wdw_ref[0:1, :] + s1 * wdw_ref[1:2, :] + a * wdw_ref[2:3, :]
    h = (_gelu(c) * gt).astype(MXU_DTYPE)
    y = _mm(h, wdown_ref[...]) + bdown_ref[...]
    o_ref[...] = _layer_norm(ALPHA * x + y, lng_ref[...], lnb_ref[...])


def _ffn_long_kernel(x_ref, xh_ref, st_ref, wup_ref, bup_ref, wdw_ref, bdw_ref, wdown_ref, bdown_ref, lng_ref, lnb_ref,
                     o_ref, conv_ref, *, seq_len, d_ff):
    tm = x_ref.shape[0]
    x = x_ref[...]
    xb = x.astype(MXU_DTYPE)
    a = _mm(xb, wup_ref[:, :d_ff]) + bup_ref[:, :d_ff]
    gt = _mm(xb, wup_ref[:, d_ff:]) + bup_ref[:, d_ff:]
    ah = _mm(xh_ref[...].astype(MXU_DTYPE), wup_ref[:, :d_ff]) + bup_ref[:, :d_ff]
    at_start = (pl.program_id(0) * tm) % seq_len == 0
    p1 = jnp.where(at_start, st_ref[0, 1:2, :], ah[7:8, :])
    p0 = jnp.where(at_start, st_ref[0, 0:1, :], ah[6:7, :])
    r = lax.broadcasted_iota(jnp.int32, a.shape, 0)
    s1 = jnp.where(r == 0, p1, pltpu.roll(a, 1, 0))
    s2 = jnp.where(r == 0, p0, jnp.where(r == 1, p1, pltpu.roll(a, 2, 0)))
    conv_ref[0] = a[tm - (CONV_W - 1):, :]
    _ffn_tail(x, a, gt, s1, s2, wdw_ref, bdw_ref, wdown_ref, bdown_ref, lng_ref, lnb_ref, o_ref)


def _ffn_short_kernel(x_ref, p0_ref, p1_ref, wup_ref, bup_ref, wdw_ref, bdw_ref, wdown_ref, bdown_ref, lng_ref, lnb_ref,
                      o_ref, a_ref, *, seq_len, d_ff):
    x = x_ref[...]
    xb = x.astype(MXU_DTYPE)
    a = _mm(xb, wup_ref[:, :d_ff]) + bup_ref[:, :d_ff]
    gt = _mm(xb, wup_ref[:, d_ff:]) + bup_ref[:, d_ff:]
    pos = lax.broadcasted_iota(jnp.int32, a.shape, 0) % seq_len
    s1 = jnp.where(pos == 0, p1_ref[...], pltpu.roll(a, 1, 0))
    s2 = jnp.where(pos == 0, p0_ref[...], jnp.where(pos == 1, p1_ref[...], pltpu.roll(a, 2, 0)))
    a_ref[...] = a
    _ffn_tail(x, a, gt, s1, s2, wdw_ref, bdw_ref, wdown_ref, bdown_ref, lng_ref, lnb_ref, o_ref)


def _ffn_layer(x, seq_len, conv_state, w_up, b_up, w_dw, b_dw, w_down, b_down, ln_g, ln_b):
    t, d = x.shape
    d_ff = w_down.shape[0]
    n_seq = t // seq_len
    tm = ROW_TILE
    assert seq_len >= CONV_W - 1 and t % tm == 0
    row = lambda a: a.reshape(1, -1)
    weights = (w_up.astype(MXU_DTYPE), row(b_up), w_dw, row(b_dw), w_down.astype(MXU_DTYPE), row(b_down), row(ln_g), row(ln_b))
    w_specs = [_const_spec((d, 2 * d_ff)), _const_spec((1, 2 * d_ff)), _const_spec((CONV_W, d_ff)), _const_spec((1, d_ff)),
               _const_spec((d_ff, d)), _const_spec((1, d)), _const_spec((1, d)), _const_spec((1, d))]
    if seq_len % tm == 0:
        tiles_per_seq = seq_len // tm
        out, conv = pl.pallas_call(
            functools.partial(_ffn_long_kernel, seq_len=seq_len, d_ff=d_ff),
            grid=(t // tm,),
            in_specs=[pl.BlockSpec((tm, d), lambda i: (i, 0)),
                      pl.BlockSpec((8, d), lambda i: (jnp.maximum(i * (tm // 8) - 1, 0), 0)),
                      pl.BlockSpec((1, CONV_W - 1, d_ff), lambda i: (i // tiles_per_seq, 0, 0))] + w_specs,
            out_specs=[pl.BlockSpec((tm, d), lambda i: (i, 0)),
                       pl.BlockSpec((1, CONV_W - 1, d_ff), lambda i: (i // tiles_per_seq, 0, 0))],
            out_shape=[jax.ShapeDtypeStruct((t, d), F32), jax.ShapeDtypeStruct((n_seq, CONV_W - 1, d_ff), F32)],
            compiler_params=_params(("arbitrary",)),
            name="ffn_long",
        )(x, x, conv_state, *weights)
        return out, conv
    assert tm % seq_len == 0
    p0 = jnp.repeat(conv_state[:, 0, :], seq_len, axis=0)
    p1 = jnp.repeat(conv_state[:, 1, :], seq_len, axis=0)
    out, a = pl.pallas_call(
        functools.partial(_ffn_short_kernel, seq_len=seq_len, d_ff=d_ff),
        grid=(t // tm,),
        in_specs=[pl.BlockSpec((tm, d), lambda i: (i, 0)), pl.BlockSpec((tm, d_ff), lambda i: (i, 0)),
                  pl.BlockSpec((tm, d_ff), lambda i: (i, 0))] + w_specs,
        out_specs=[pl.BlockSpec((tm, d), lambda i: (i, 0)), pl.BlockSpec((tm, d_ff), lambda i: (i, 0))],
        out_shape=[jax.ShapeDtypeStruct((t, d), F32), jax.ShapeDtypeStruct((t, d_ff), F32)],
        compiler_params=_params(("parallel",)),
        name="ffn_short",
    )(x, p0, p1, *weights)
    return out, a.reshape(n_seq, seq_len, d_ff)[:, seq_len - (CONV_W - 1):, :]


def _kv_kernel(x_ref, wt_ref, kv_ref, kvb_ref, *, n_f32_only):
    r = _mm_nt(wt_ref[...], x_ref[...].astype(MXU_DTYPE))
    kv_ref[0] = r
    kvb_ref[0] = r[n_f32_only:, :].astype(MXU_DTYPE)


def _kv_proj(x, n_seq, kv_w):
    t, d = x.shape
    seq_len = t // n_seq
    n_out = kv_w.shape[1]
    n_cmp = 2 * N_KV * HEAD_DIM
    tm = ROW_TILE
    assert seq_len % tm == 0
    tiles = seq_len // tm
    return pl.pallas_call(
        functools.partial(_kv_kernel, n_f32_only=n_cmp),
        grid=(t // tm,),
        in_specs=[pl.BlockSpec((tm, d), lambda i: (i, 0)), _const_spec((n_out, d))],
        out_specs=[pl.BlockSpec((1, n_out, tm), lambda i: (i // tiles, 0, i % tiles)),
                   pl.BlockSpec((1, n_out - n_cmp, tm), lambda i: (i // tiles, 0, i % tiles))],
        out_shape=[jax.ShapeDtypeStruct((n_seq, n_out, seq_len), F32),
                   jax.ShapeDtypeStruct((n_seq, n_out - n_cmp, seq_len), MXU_DTYPE)],
        compiler_params=_params(("parallel",)),
        name="kv_proj",
    )(x, kv_w.T.astype(MXU_DTYPE))


def _sigmoid(x):
    return 1.0 / (1.0 + jnp.exp(-x))


def _qg_heads_kernel(x_ref, wq_ref, bq_ref, wg_ref, bg_ref, q_ref, gate_ref):
    xb = x_ref[...].astype(MXU_DTYPE)
    scale = HEAD_DIM ** -0.5
    for h in range(q_ref.shape[1]):
        q_ref[0, h] = ((_mm(xb, wq_ref[h]) + bq_ref[h]) * scale).astype(MXU_DTYPE)
    for g in range(N_KV):
        gate_ref[0, g] = _sigmoid(_mm(xb, wg_ref[g]) + bg_ref[g])


def _qg_proj_heads(x, n_seq, w_qg, b_qg):
    t, d = x.shape
    seq_len = t // n_seq
    n_heads = (w_qg.shape[1] // (HEAD_DIM + 3))
    group = n_heads // N_KV
    nq = n_heads * HEAD_DIM
    tm = ROW_TILE
    assert seq_len % tm == 0 and 3 * group <= LANES
    tiles = seq_len // tm
    wq = w_qg[:, :nq].reshape(d, n_heads, HEAD_DIM).transpose(1, 0, 2).astype(MXU_DTYPE)
    bq = b_qg[:nq].reshape(n_heads, 1, HEAD_DIM)
    pad = LANES - 3 * group
    wg = jnp.pad(w_qg[:, nq:].reshape(d, N_KV, 3 * group).transpose(1, 0, 2), ((0, 0), (0, 0), (0, pad))).astype(MXU_DTYPE)
    bg = jnp.pad(b_qg[nq:].reshape(N_KV, 1, 3 * group), ((0, 0), (0, 0), (0, pad)))
    return pl.pallas_call(
        _qg_heads_kernel,
        grid=(t // tm,),
        in_specs=[pl.BlockSpec((tm, d), lambda i: (i, 0)), _const_spec(wq.shape), _const_spec(bq.shape),
                  _const_spec(wg.shape), _const_spec(bg.shape)],
        out_specs=[pl.BlockSpec((1, n_heads, tm, HEAD_DIM), lambda i: (i // tiles, 0, i % tiles, 0)),
                   pl.BlockSpec((1, N_KV, tm, LANES), lambda i: (i // tiles, 0, i % tiles, 0))],
        out_shape=[jax.ShapeDtypeStruct((n_seq, n_heads, seq_len, HEAD_DIM), MXU_DTYPE),
                   jax.ShapeDtypeStruct((n_seq, N_KV, seq_len, LANES), F32)],
        compiler_params=_params(("parallel",)),
        name="qg_proj_heads",
    )(x, wq, bq, wg, bg)


def _qg_flat_kernel(x_ref, wq_ref, bq_ref, wg_ref, bg_ref, q_ref, gate_ref):
    xb = x_ref[...].astype(MXU_DTYPE)
    q_ref[...] = (_mm(xb, wq_ref[...]) + bq_ref[...]) * (HEAD_DIM ** -0.5)
    gate_ref[...] = _sigmoid(_mm(xb, wg_ref[...]) + bg_ref[...])


def _qg_proj_flat(x, w_qg, b_qg):
    t, d = x.shape
    n_gate = w_qg.shape[1] // (HEAD_DIM + 3) * 3
    nq = w_qg.shape[1] - n_gate
    tm = ROW_TILE
    assert t % tm == 0 and n_gate <= LANES
    wg = jnp.pad(w_qg[:, nq:], ((0, 0), (0, LANES - n_gate))).astype(MXU_DTYPE)
    bg = jnp.pad(b_qg[nq:], (0, LANES - n_gate)).reshape(1, LANES)
    return pl.pallas_call(
        _qg_flat_kernel,
        grid=(t // tm,),
        in_specs=[pl.BlockSpec((tm, d), lambda i: (i, 0)), _const_spec((d, nq)), _const_spec((1, nq)),
                  _const_spec((d, LANES)), _const_spec((1, LANES))],
        out_specs=[pl.BlockSpec((tm, nq), lambda i: (i, 0)), pl.BlockSpec((tm, LANES), lambda i: (i, 0))],
        out_shape=[jax.ShapeDtypeStruct((t, nq), F32), jax.ShapeDtypeStruct((t, LANES), F32)],
        compiler_params=_params(("parallel",)),
        name="qg_proj_flat",
    )(x, w_qg[:, :nq].astype(MXU_DTYPE), b_qg[:nq].reshape(1, nq), wg, bg)


def _oproj_heads_kernel(o_ref, x_ref, wo_ref, bo_ref, lng_ref, lnb_ref, y_ref):
    acc = jnp.zeros(y_ref.shape, F32)
    for h in range(o_ref.shape[1]):
        acc = acc + _mm(o_ref[0, h], wo_ref[h])
    y_ref[...] = _layer_norm(ALPHA * x_ref[...] + acc + bo_ref[...], lng_ref[...], lnb_ref[...])


def _oproj_heads(o, x, w_o, b_o, ln_g, ln_b):
    n_seq, n_heads, seq_len, _ = o.shape
    t, d = x.shape
    tm = ROW_TILE
    tiles = seq_len // tm
    row = lambda a: a.reshape(1, -1)
    return pl.pallas_call(
        _oproj_heads_kernel,
        grid=(t // tm,),
        in_specs=[pl.BlockSpec((1, n_heads, tm, HEAD_DIM), lambda i: (i // tiles, 0, i % tiles, 0)),
                  pl.BlockSpec((tm, d), lambda i: (i, 0)), _const_spec((n_heads, HEAD_DIM, d)),
                  _const_spec((1, d)), _const_spec((1, d)), _const_spec((1, d))],
        out_specs=pl.BlockSpec((tm, d), lambda i: (i, 0)),
        out_shape=jax.ShapeDtypeStruct((t, d), F32),
        compiler_params=_params(("parallel",)),
        name="oproj_heads",
    )(o, x, w_o.reshape(n_heads, HEAD_DIM, d).astype(MXU_DTYPE), row(b_o), row(ln_g), row(ln_b))


def _oproj_flat_kernel(o_ref, x_ref, wo_ref, bo_ref, lng_ref, lnb_ref, y_ref):
    y = _mm(o_ref[...].astype(MXU_DTYPE), wo_ref[...]) + bo_ref[...]
    y_ref[...] = _layer_norm(ALPHA * x_ref[...] + y, lng_ref[...], lnb_ref[...])


def _oproj_flat(o, x, w_o, b_o, ln_g, ln_b):
    t, d = x.shape
    tm = ROW_TILE
    row = lambda a: a.reshape(1, -1)
    return pl.pallas_call(
        _oproj_flat_kernel,
        grid=(t // tm,),
        in_specs=[pl.BlockSpec((tm, o.shape[1]), lambda i: (i, 0)), pl.BlockSpec((tm, d), lambda i: (i, 0)),
                  _const_spec(w_o.shape), _const_spec((1, d)), _const_spec((1, d)), _const_spec((1, d))],
        out_specs=pl.BlockSpec((tm, d), lambda i: (i, 0)),
        out_shape=jax.ShapeDtypeStruct((t, d), F32),
        compiler_params=_params(("parallel",)),
        name="oproj_flat",
    )(o, x, w_o.astype(MXU_DTYPE), row(b_o), row(ln_g), row(ln_b))


def _bias_tile_kernel(rel_ref, o_ref):
    h = pl.program_id(0)
    r = lax.broadcasted_iota(jnp.int32, (Q_BLOCK, LANES), 0)
    c = lax.broadcasted_iota(jnp.int32, (Q_BLOCK, LANES), 1)
    for dl in range(o_ref.shape[1]):
        dist = dl * LANES + r - c
        val = jnp.full((Q_BLOCK, LANES), rel_ref[0, h], F32)
        for j in range(1, NUM_BUCKETS):
            val = jnp.where(dist >= BUCKET_THR[j], rel_ref[j, h], val)
        o_ref[0, dl] = val


def _bias_tiles(rel_bias):
    n_heads = rel_bias.shape[1]
    return pl.pallas_call(
        _bias_tile_kernel,
        grid=(n_heads,),
        in_specs=[pl.BlockSpec(memory_space=pltpu.SMEM)],
        out_specs=pl.BlockSpec((1, N_BIAS_TILES, Q_BLOCK, LANES), lambda h: (h, 0, 0, 0)),
        out_shape=jax.ShapeDtypeStruct((n_heads, N_BIAS_TILES, Q_BLOCK, LANES), F32),
        compiler_params=_params(("parallel",)),
        name="bias_tiles",
    )(rel_bias)


CMP_PAGES = 16
PE_ROWS = 8


def _compress_kernel(pt_ref, *refs, n_pages):
    del pt_ref
    page_refs = refs[:n_pages]
    pe_ref, w1_ref, b1_ref, w2_ref, b2_ref, o_ref, buf_ref = refs[n_pages:]
    for j in range(n_pages):
        buf_ref[j] = page_refs[j][0]
    rows = N_KV * n_pages
    for k in range(2):
        acc = jnp.zeros((rows + PE_ROWS, 2 * LANES), F32)
        for dd in range(HEAD_DIM // 2):
            parts = [jnp.concatenate([buf_ref[:, k, g, 2 * dd, :], buf_ref[:, k, g, 2 * dd + 1, :]], axis=1)
                     for g in range(N_KV)]
            parts.append(pe_ref[k, dd])
            acc = acc + _mm(jnp.concatenate(parts, axis=0).astype(MXU_DTYPE), w1_ref[k, dd])
        pre = acc[:rows] + acc[rows:rows + 1] + b1_ref[k]
        o_ref[0, 0, k] = _mm(_gelu(pre).astype(MXU_DTYPE), w2_ref[k]) + b2_ref[k]


def _compress_pages(pages, page_table, cmp_pe, cmp_w1, cmp_b1, cmp_w2, cmp_b2):
    n_seq, pps = page_table.shape
    bpp = PAGE_SIZE // BLK
    hid = cmp_w1.shape[-1]
    assert pps % CMP_PAGES == 0 and bpp * hid == 2 * LANES and bpp * HEAD_DIM == LANES
    n_chunks = pps // CMP_PAGES
    w1r = cmp_w1.reshape(2, BLK, HEAD_DIM // 2, 2, hid)
    eye = jnp.eye(bpp, dtype=cmp_w1.dtype)
    w1bd = jnp.einsum('kldeh,nm->kdenlmh', w1r, eye).reshape(2, HEAD_DIM // 2, 2 * PAGE_SIZE, bpp * hid).astype(MXU_DTYPE)
    pe_t = jnp.tile(jnp.swapaxes(cmp_pe, 1, 2), (1, 1, bpp))
    pe_rows = jnp.pad(pe_t.reshape(2, HEAD_DIM // 2, 1, 2 * PAGE_SIZE), ((0, 0), (0, 0), (0, PE_ROWS - 1), (0, 0)))
    b1 = jnp.tile(cmp_b1, (1, bpp)).reshape(2, 1, bpp * hid)
    w2bd = jnp.einsum('khd,nm->knhmd', cmp_w2, eye).reshape(2, bpp * hid, bpp * HEAD_DIM).astype(MXU_DTYPE)
    b2 = jnp.tile(cmp_b2, (1, bpp)).reshape(2, 1, bpp * HEAD_DIM)
    rows = N_KV * CMP_PAGES
    page_spec = lambda j: pl.BlockSpec((1, 2, N_KV, HEAD_DIM, PAGE_SIZE),
                                       lambda b, c, pt: (pt[b, c * CMP_PAGES + j], 0, 0, 0, 0))
    const = lambda a: pl.BlockSpec(a.shape, lambda b, c, pt: (0,) * a.ndim, pipeline_mode=pl.Buffered(1))
    out = pl.pallas_call(
        functools.partial(_compress_kernel, n_pages=CMP_PAGES),
        grid_spec=pltpu.PrefetchScalarGridSpec(
            num_scalar_prefetch=1,
            grid=(n_seq, n_chunks),
            in_specs=[page_spec(j) for j in range(CMP_PAGES)] + [const(a) for a in (pe_rows, w1bd, b1, w2bd, b2)],
            out_specs=pl.BlockSpec((1, 1, 2, rows, bpp * HEAD_DIM), lambda b, c, pt: (b, c, 0, 0, 0)),
            scratch_shapes=[pltpu.VMEM((CMP_PAGES, 2, N_KV, HEAD_DIM, PAGE_SIZE), F32)]),
        out_shape=jax.ShapeDtypeStruct((n_seq, n_chunks, 2, rows, bpp * HEAD_DIM), F32),
        compiler_params=_params(("parallel", "arbitrary")),
        name="compress_pages",
    )(page_table, *([pages] * CMP_PAGES), pe_rows, w1bd, b1, w2bd, b2)
    out = out.reshape(n_seq, n_chunks, 2, N_KV, CMP_PAGES, bpp, HEAD_DIM)
    return out.transpose(0, 2, 3, 1, 4, 5, 6).reshape(n_seq, 2, N_KV, pps * bpp, HEAD_DIM)


def _online_update(carry, s, vt):
    m, l, acc = carry
    m_new = jnp.maximum(m, jnp.max(s, axis=-1, keepdims=True))
    a = jnp.exp(m - m_new)
    p = jnp.exp(s - m_new)
    return m_new, a * l + jnp.sum(p, axis=-1, keepdims=True), a * acc + _mm_nt(p.astype(MXU_DTYPE), vt)


def _attn_prompt_kernel(rel_ref, q_ref, gate_ref, kc_ref, vc_ref, sk_ref, sv_ref, wk_ref, wv_ref, tb_ref, o_ref,
                        *, group, n_cmp):
    g = pl.program_id(1)
    i = pl.program_id(2)
    rows = group * Q_BLOCK
    q4 = q_ref[0].reshape(rows, HEAD_DIM)
    r_io = lax.broadcasted_iota(jnp.int32, (Q_BLOCK, LANES), 0)
    c_io = lax.broadcasted_iota(jnp.int32, (Q_BLOCK, LANES), 1)
    qpos = i * Q_BLOCK + r_io

    qpos_c = i * Q_BLOCK + lax.broadcasted_iota(jnp.int32, (Q_BLOCK, n_cmp), 0)
    blk = lax.broadcasted_iota(jnp.int32, (Q_BLOCK, n_cmp), 1)
    dist_c = qpos_c - (blk * BLK + BLK - 1)
    ok_c = dist_c >= 0
    bkt = _bucket_of(dist_c)
    lc = _mm(q4, kc_ref[0, 0])
    imp = jnp.zeros((Q_BLOCK, n_cmp), F32)
    o_c = []
    for r in range(group):
        h = g * group + r
        bias = jnp.full((Q_BLOCK, n_cmp), rel_ref[0, h], F32)
        for j in range(1, NUM_BUCKETS):
            bias = jnp.where(bkt == j, rel_ref[j, h], bias)
        lg = jnp.where(ok_c, lc[r * Q_BLOCK:(r + 1) * Q_BLOCK] + bias, NEG_INF)
        p = jnp.exp(lg - jnp.max(lg, axis=-1, keepdims=True)) * ok_c.astype(F32)
        pc = p / jnp.maximum(jnp.sum(p, axis=-1, keepdims=True), 1e-30)
        imp = imp + pc
        o_c.append(_mm(pc.astype(MXU_DTYPE), vc_ref[0, 0]))

    cur = qpos_c // BLK
    forced = (blk == 0) | (blk == cur) | (blk == cur - 1)
    score = jnp.where(forced, FORCED_SCORE, jnp.where(blk <= cur, imp, -1.0))
    sel = jnp.zeros((Q_BLOCK, n_cmp), F32)
    for _ in range(min(N_SEL, n_cmp)):
        mx = jnp.max(score, axis=-1, keepdims=True)
        idx = jnp.min(jnp.where(score == mx, blk, n_cmp), axis=-1, keepdims=True)
        hit = blk == idx
        sel = jnp.where(hit, 1.0, sel)
        score = jnp.where(hit, -2.0, score)
    sel_b = sel.astype(MXU_DTYPE)
    blk_rows = lax.broadcasted_iota(jnp.int32, (n_cmp, LANES), 0)
    col_blk = lax.broadcasted_iota(jnp.int32, (n_cmp, LANES), 1) // BLK

    def biased(s, j):
        dl = i - j
        out = []
        for r in range(group):
            far = rel_ref[NUM_BUCKETS - 1, g * group + r]
            bias = jnp.where(dl < N_BIAS_TILES, tb_ref[r, jnp.minimum(dl, N_BIAS_TILES - 1)], far)
            out.append(s[r * Q_BLOCK:(r + 1) * Q_BLOCK] + bias)
        return out

    def sel_body(j, carry):
        off = pl.multiple_of(j * LANES, LANES)
        s = _mm(q4, sk_ref[0, 0, 0, :, pl.ds(off, LANES)])
        expand = (blk_rows == j * (LANES // BLK) + col_blk).astype(MXU_DTYPE)
        chosen = _mm(sel_b, expand)
        ok = (chosen > 0.5) & (j * LANES + c_io <= qpos)
        madd = jnp.where(ok, 0.0, NEG_INF)
        sa = jnp.concatenate([x + madd for x in biased(s, j)], axis=0)
        return _online_update(carry, sa, sv_ref[0, 0, 0, :, pl.ds(off, LANES)])

    def win_body(j, carry):
        off = pl.multiple_of(j * LANES, LANES)
        s = _mm(q4, wk_ref[0, 0, 0, :, pl.ds(off, LANES)])
        dw = qpos - (j * LANES + c_io)
        madd = jnp.where((dw >= 0) & (dw < WINDOW), 0.0, NEG_INF)
        sa = jnp.concatenate([x + madd for x in biased(s, j)], axis=0)
        return _online_update(carry, sa, wv_ref[0, 0, 0, :, pl.ds(off, LANES)])

    init = (jnp.full((rows, 1), NEG_INF, F32), jnp.zeros((rows, 1), F32), jnp.zeros((rows, HEAD_DIM), F32))
    _, l_s, acc_s = lax.fori_loop(0, i + 1, sel_body, init)
    _, l_w, acc_w = lax.fori_loop(jnp.maximum(i - WINDOW // LANES, 0), i + 1, win_body, init)
    o_s = acc_s / l_s
    o_w = acc_w / l_w
    gt = gate_ref[0, 0]
    for r in range(group):
        lo, hi = r * Q_BLOCK, (r + 1) * Q_BLOCK
        o = gt[:, 3 * r:3 * r + 1] * o_c[r] + gt[:, 3 * r + 1:3 * r + 2] * o_s[lo:hi] + gt[:, 3 * r + 2:3 * r + 3] * o_w[lo:hi]
        o_ref[0, r] = o.astype(o_ref.dtype)


def _attn_prompt(q, gates, kvc, kvtb, bias_tiles, rel_bias):
    bsz, n_heads, seq, _ = q.shape
    group = n_heads // N_KV
    n_cmp = kvc.shape[3]
    assert seq % Q_BLOCK == 0 and Q_BLOCK == LANES and n_cmp * BLK == seq
    kct = jnp.swapaxes(kvc[:, 0], -1, -2).astype(MXU_DTYPE)
    vc = kvc[:, 1].astype(MXU_DTYPE)
    slot = lambda s: pl.BlockSpec((1, 1, 1, HEAD_DIM, seq), lambda b, g, i: (b, s, g, 0, 0))
    return pl.pallas_call(
        functools.partial(_attn_prompt_kernel, group=group, n_cmp=n_cmp),
        grid=(bsz, N_KV, seq // Q_BLOCK),
        in_specs=[pl.BlockSpec(memory_space=pltpu.SMEM),
                  pl.BlockSpec((1, group, Q_BLOCK, HEAD_DIM), lambda b, g, i: (b, g, i, 0)),
                  pl.BlockSpec((1, 1, Q_BLOCK, LANES), lambda b, g, i: (b, g, i, 0)),
                  pl.BlockSpec((1, 1, HEAD_DIM, n_cmp), lambda b, g, i: (b, g, 0, 0)),
                  pl.BlockSpec((1, 1, n_cmp, HEAD_DIM), lambda b, g, i: (b, g, 0, 0)),
                  slot(0), slot(1), slot(2), slot(3),
                  pl.BlockSpec((group, N_BIAS_TILES, Q_BLOCK, LANES), lambda b, g, i: (g, 0, 0, 0))],
        out_specs=pl.BlockSpec((1, group, Q_BLOCK, HEAD_DIM), lambda b, g, i: (b, g, i, 0)),
        out_shape=jax.ShapeDtypeStruct((bsz, n_heads, seq, HEAD_DIM), MXU_DTYPE),
        compiler_params=_params(("parallel", "parallel", "arbitrary")),
        name="attn_prompt",
    )(rel_bias, q, gates, kct, vc, kvtb, kvtb, kvtb, kvtb, bias_tiles)


SLC_PAGES = 16
NEW_PAD = 16


def _tn(a, b):
    return lax.dot_general(a, b, (((0,), (0,)), ((), ())), preferred_element_type=F32)


def _bias_rows(trow_ref, dist):
    bkt = _bucket_of(dist)
    bias = jnp.broadcast_to(trow_ref[0:1, :], dist.shape)
    for j in range(1, NUM_BUCKETS):
        bias = jnp.where(bkt == j, trow_ref[j:j + 1, :], bias)
    return bias


def _attn_sample_kernel(pt_ref, *refs, n_pages, past_len, n_new, n_cmp, n_blk):
    del pt_ref
    page_refs = refs[:n_pages]
    (qbd_ref, gate_ref, trow_ref, kc_ref, vct_ref, knew_ref, vnewt_ref, wkt_ref, wvt_ref, wknew_ref, wvnewt_ref,
     o_ref, sel_ref, oc_ref, m_ref, l_ref, acc_ref) = refs[n_pages:]
    c = pl.program_id(1)
    qbd = qbd_ref[0]
    gd = qbd.shape[0]
    bpp = PAGE_SIZE // BLK
    npb = past_len // BLK

    def qpos_like(shape):
        return past_len + lax.broadcasted_iota(jnp.int32, shape, 1) % n_new

    @pl.when(c == 0)
    def _():
        blk = lax.broadcasted_iota(jnp.int32, (n_cmp, LANES), 0)
        dist = qpos_like((n_cmp, LANES)) - (blk * BLK + BLK - 1)
        ok = dist >= 0
        lg = jnp.where(ok, _mm(kc_ref[0], qbd) + _bias_rows(trow_ref, dist), NEG_INF)
        p = jnp.exp(lg - jnp.max(lg, axis=0, keepdims=True)) * ok.astype(F32)
        pc = p / jnp.maximum(jnp.sum(p, axis=0, keepdims=True), 1e-30)
        oc_ref[...] = _mm(vct_ref[0], pc.astype(MXU_DTYPE))
        quarter = LANES // 4
        imp = pc + pltpu.roll(pc, quarter, 1) + pltpu.roll(pc, 2 * quarter, 1) + pltpu.roll(pc, 3 * quarter, 1)
        nbp = sel_ref.shape[0]
        imp = jnp.concatenate([imp, jnp.zeros((nbp - n_cmp, LANES), F32)], axis=0)
        blk = lax.broadcasted_iota(jnp.int32, (nbp, LANES), 0)
        cur = qpos_like((nbp, LANES)) // BLK
        forced = (blk == 0) | (blk == cur) | (blk == cur - 1)
        score = jnp.where(forced, FORCED_SCORE, jnp.where(blk <= cur, imp, -1.0))
        score = jnp.where(blk >= n_blk, -2.0, score)
        sel = jnp.zeros((nbp, LANES), F32)
        for _ in range(min(N_SEL, n_blk)):
            mx = jnp.max(score, axis=0, keepdims=True)
            idx = jnp.min(jnp.where(score == mx, blk, nbp), axis=0, keepdims=True)
            hit = blk == idx
            sel = jnp.where(hit, 1.0, sel)
            score = jnp.where(hit, -3.0, score)
        sel_ref[...] = sel
        m_ref[...] = jnp.full(m_ref.shape, NEG_INF, F32)
        l_ref[...] = jnp.zeros(l_ref.shape, F32)
        acc_ref[...] = jnp.zeros(acc_ref.shape, F32)

    tok = lax.broadcasted_iota(jnp.int32, (PAGE_SIZE, LANES), 0)
    qpos_p = qpos_like((PAGE_SIZE, LANES))
    parts = []
    for j in range(n_pages):
        pg = c * n_pages + j
        st = _tn(page_refs[j][0, 0].reshape(gd, PAGE_SIZE).astype(MXU_DTYPE), qbd)
        dist = qpos_p - (pg * PAGE_SIZE + tok)
        is_far = past_len - (pg * PAGE_SIZE + PAGE_SIZE - 1) >= FAR_DIST
        bias = lax.cond(is_far,
                        lambda: jnp.broadcast_to(trow_ref[NUM_BUCKETS - 1:NUM_BUCKETS, :], (PAGE_SIZE, LANES)),
                        lambda dist=dist: _bias_rows(trow_ref, dist))
        chosen = jnp.broadcast_to(sel_ref[pl.ds(pg * bpp + bpp - 1, 1), :], (PAGE_SIZE, LANES))
        for t in reversed(range(bpp - 1)):
            chosen = jnp.where(tok < (t + 1) * BLK, sel_ref[pl.ds(pg * bpp + t, 1), :], chosen)
        parts.append(jnp.where((chosen > 0.5) & (dist >= 0), st + bias, NEG_INF))
    s_all = jnp.concatenate(parts, axis=0)
    m_old = m_ref[...]
    m_new = jnp.maximum(m_old, jnp.max(s_all, axis=0, keepdims=True))
    a = jnp.exp(m_old - m_new)
    p = jnp.exp(s_all - m_new)
    acc = acc_ref[...] * a
    for j in range(n_pages):
        vt = page_refs[j][0, 1].reshape(gd, PAGE_SIZE).astype(MXU_DTYPE)
        acc = acc + _mm(vt, p[j * PAGE_SIZE:(j + 1) * PAGE_SIZE].astype(MXU_DTYPE))
    acc_ref[...] = acc
    l_ref[...] = a * l_ref[...] + jnp.sum(p, axis=0, keepdims=True)
    m_ref[...] = m_new

    @pl.when(c == pl.num_programs(1) - 1)
    def _():
        tok_n = lax.broadcasted_iota(jnp.int32, (NEW_PAD, LANES), 0)
        dist_n = qpos_like((NEW_PAD, LANES)) - (past_len + tok_n)
        live_n = (tok_n < n_new) & (dist_n >= 0)
        bias_n = _bias_rows(trow_ref, dist_n)
        chosen_n = sel_ref[pl.ds(npb, 1), :] > 0.5
        s_n = jnp.where(live_n & chosen_n, _mm(knew_ref[0], qbd) + bias_n, NEG_INF)
        m_old = m_ref[...]
        m_fin = jnp.maximum(m_old, jnp.max(s_n, axis=0, keepdims=True))
        a = jnp.exp(m_old - m_fin)
        p_n = jnp.exp(s_n - m_fin)
        acc_s = acc_ref[...] * a + _mm(vnewt_ref[0], p_n.astype(MXU_DTYPE))
        o_s = acc_s / (a * l_ref[...] + jnp.sum(p_n, axis=0, keepdims=True))
        wb = wkt_ref.shape[-1]
        tok_w = lax.broadcasted_iota(jnp.int32, (wb, LANES), 0)
        kwpos = past_len - wb + tok_w
        dw = qpos_like((wb, LANES)) - kwpos
        ok_w = (dw >= 0) & (dw < WINDOW) & (kwpos >= 0)
        s_w = _tn(wkt_ref[0, 0].reshape(gd, wb).astype(MXU_DTYPE), qbd) + _bias_rows(trow_ref, dw)
        s_w = jnp.where(ok_w, s_w, NEG_INF)
        s_wn = jnp.where(live_n & (dist_n < WINDOW), _mm(wknew_ref[0], qbd) + bias_n, NEG_INF)
        m_w = jnp.maximum(jnp.max(s_w, axis=0, keepdims=True), jnp.max(s_wn, axis=0, keepdims=True))
        p_w = jnp.exp(s_w - m_w)
        p_wn = jnp.exp(s_wn - m_w)
        o_w = (_mm(wvt_ref[0, 0].reshape(gd, wb).astype(MXU_DTYPE), p_w.astype(MXU_DTYPE))
               + _mm(wvnewt_ref[0], p_wn.astype(MXU_DTYPE)))
        o_w = o_w / (jnp.sum(p_w, axis=0, keepdims=True) + jnp.sum(p_wn, axis=0, keepdims=True))
        gt = gate_ref[0]
        tot = gt[0:1, :] * oc_ref[...] + gt[1:2, :] * o_s + gt[2:3, :] * o_w
        lane_g = (lax.broadcasted_iota(jnp.int32, (HEAD_DIM, LANES), 1) // n_new) % N_KV
        out = jnp.zeros((HEAD_DIM, LANES), F32)
        for g in range(N_KV):
            out = out + jnp.where(lane_g == g, tot[g * HEAD_DIM:(g + 1) * HEAD_DIM, :], 0.0)
        o_ref[0] = out


def _attn_sample(q, gates, kvc, kv_new_t, slc_pages, page_table, win_t, rel_bias):
    bsz, n_new, nq = q.shape
    n_heads = nq // HEAD_DIM
    group = n_heads // N_KV
    pps = page_table.shape[1]
    past_len = pps * PAGE_SIZE
    n_cmp = kvc.shape[3]
    n_blk = past_len // BLK + -(-n_new // BLK)
    nbp = -(-n_blk // 8) * 8
    gd = N_KV * HEAD_DIM
    wb = win_t.shape[-1]
    assert n_heads * n_new == LANES and group == 4 and n_new <= min(BLK, NEW_PAD) and pps % SLC_PAGES == 0
    assert n_cmp == past_len // BLK and n_cmp % 8 == 0
    q5 = q.reshape(bsz, n_new, N_KV, group, HEAD_DIM)
    qbd = jnp.einsum('hg,bqgrd->bhdrgq', jnp.eye(N_KV, dtype=q.dtype), q5).reshape(bsz, gd, LANES).astype(MXU_DTYPE)
    g5 = gates[:, :, :3 * n_heads].reshape(bsz, n_new, N_KV, group, 3)
    gate_t = jnp.pad(g5.transpose(0, 4, 3, 2, 1).reshape(bsz, 3, LANES), ((0, 0), (0, 5), (0, 0)))
    trow = jnp.broadcast_to(rel_bias.reshape(NUM_BUCKETS, N_KV, group).transpose(0, 2, 1)[..., None],
                            (NUM_BUCKETS, group, N_KV, n_new)).reshape(NUM_BUCKETS, LANES)
    kc = kvc[:, 0].transpose(0, 2, 1, 3).reshape(bsz, n_cmp, gd).astype(MXU_DTYPE)
    vct = kvc[:, 1].transpose(0, 1, 3, 2).reshape(bsz, gd, n_cmp).astype(MXU_DTYPE)
    new = kv_new_t.reshape(6, gd, bsz, n_new).transpose(0, 2, 1, 3)
    new = jnp.pad(new, ((0, 0), (0, 0), (0, 0), (0, NEW_PAD - n_new))).astype(MXU_DTYPE)
    k_rows = lambda s: jnp.swapaxes(new[s], 1, 2)
    page_spec = lambda j: pl.BlockSpec((1, 2, N_KV, HEAD_DIM, PAGE_SIZE),
                                       lambda b, c, pt: (pt[b, c * SLC_PAGES + j], 0, 0, 0, 0))
    per_b = lambda a: pl.BlockSpec((1,) + a.shape[1:], lambda b, c, pt: (b,) + (0,) * (a.ndim - 1))
    win_spec = lambda s: pl.BlockSpec((1, 1, N_KV, HEAD_DIM, wb), lambda b, c, pt: (b, s, 0, 0, 0))
    smalls = (qbd, gate_t)
    o_t = pl.pallas_call(
        functools.partial(_attn_sample_kernel, n_pages=SLC_PAGES, past_len=past_len, n_new=n_new, n_cmp=n_cmp,
                          n_blk=n_blk),
        grid_spec=pltpu.PrefetchScalarGridSpec(
            num_scalar_prefetch=1,
            grid=(bsz, pps // SLC_PAGES),
            in_specs=[page_spec(j) for j in range(SLC_PAGES)]
            + [per_b(qbd), per_b(gate_t), pl.BlockSpec(trow.shape, lambda b, c, pt: (0, 0)),
               per_b(kc), per_b(vct), per_b(k_rows(2)), per_b(new[3]), win_spec(0), win_spec(1),
               per_b(k_rows(4)), per_b(new[5])],
            out_specs=pl.BlockSpec((1, HEAD_DIM, LANES), lambda b, c, pt: (b, 0, 0)),
            scratch_shapes=[pltpu.VMEM((nbp, LANES), F32), pltpu.VMEM((gd, LANES), F32), pltpu.VMEM((1, LANES), F32),
                            pltpu.VMEM((1, LANES), F32), pltpu.VMEM((gd, LANES), F32)]),
        out_shape=jax.ShapeDtypeStruct((bsz, HEAD_DIM, LANES), F32),
        compiler_params=_params(("parallel", "arbitrary")),
        name="attn_sample",
    )(page_table, *([slc_pages] * SLC_PAGES), qbd, gate_t, trow, kc, vct, k_rows(2), new[3], win_t, win_t,
      k_rows(4), new[5])
    del smalls
    return o_t.reshape(bsz, HEAD_DIM, group, N_KV, n_new).transpose(0, 4, 3, 2, 1).reshape(bsz, n_new, nq)


def _tokens_last(x):
    nd = x.ndim
    return jnp.transpose(x, tuple(range(nd - 4)) + (nd - 3, nd - 2, nd - 1, nd - 4))


def _tokens_first(x):
    nd = x.ndim
    return jnp.transpose(x, tuple(range(nd - 4)) + (nd - 1, nd - 4, nd - 3, nd - 2))


def kernel(x_prompt, x_sample, cache_kv_cmp, cache_kv_slc, cache_win, state_conv, page_table, ln_g, ln_b, sg_w_in, sg_b_in, sg_ln_g, sg_ln_b, sg_w_s, sg_b_s, sg_w_out, sg_b_out, ffn_w_up, ffn_b_up, ffn_w_dw, ffn_b_dw, ffn_w_down, ffn_b_down, kv_w, cmp_pe, cmp_w1, cmp_b1, cmp_w2, cmp_b2, nsa_w_qg, nsa_b_qg, nsa_w_o, nsa_b_o, rel_bias):
    depth = ln_g.shape[0]
    n_a = sg_w_in.shape[0]
    assert depth == DEPTH
    bsz, seq, d_model = x_prompt.shape
    dbsz, n_new, _ = x_sample.shape
    d_ff = ffn_w_down.shape[1]
    kvd = (2, N_KV, HEAD_DIM)
    gd = N_KV * HEAD_DIM
    cmp_w = (cmp_pe, cmp_w1, cmp_b1, cmp_w2, cmp_b2)

    def trunk(x, seq_len, conv_state, attend_layer):
        n_seq = x.shape[0] // seq_len
        new_conv, sg_v, kv_t, ctx = [], [], None, None
        for l in range(depth):
            if l < n_a:
                x, v = _sgu_layer(x, seq_len, sg_w_in[l], sg_b_in[l], sg_ln_g[l], sg_ln_b[l], sg_w_s[l], sg_b_s[l],
                                  sg_w_out[l], sg_b_out[l], ln_g[l, 0], ln_b[l, 0])
                sg_v.append(v.reshape(n_seq, seq_len, -1))
            else:
                i = l - n_a
                x = attend_layer(x, ctx, nsa_w_qg[i], nsa_b_qg[i], nsa_w_o[i], nsa_b_o[i], ln_g[l, 0], ln_b[l, 0])
            x, cs = _ffn_layer(x, seq_len, conv_state[l], ffn_w_up[l], ffn_b_up[l], ffn_w_dw[l], ffn_b_dw[l],
                               ffn_w_down[l], ffn_b_down[l], ln_g[l, 1], ln_b[l, 1])
            new_conv.append(cs)
            if l == n_a - 1:
                kv_t, ctx = attend_layer.prepare(x)
        return x, jnp.stack(new_conv), (jnp.stack(sg_v) if sg_v else None), kv_t

    bias_tiles = _bias_tiles(rel_bias)

    def prompt_prepare(x):
        kv_t, kv_tb = _kv_proj(x, bsz, kv_w)
        cmp_t = kv_t[:, :2 * gd].reshape(bsz, 2, N_KV, HEAD_DIM, seq // PAGE_SIZE, PAGE_SIZE)
        pages = cmp_t.transpose(0, 4, 1, 2, 3, 5).reshape(bsz * (seq // PAGE_SIZE), 2, N_KV, HEAD_DIM, PAGE_SIZE)
        ident = jnp.arange(bsz * (seq // PAGE_SIZE), dtype=jnp.int32).reshape(bsz, seq // PAGE_SIZE)
        kvc = _compress_pages(pages, ident, *cmp_w)
        return kv_t, (kvc, kv_tb.reshape(bsz, 4, N_KV, HEAD_DIM, seq))

    def prompt_attend(x, ctx, w_qg, b_qg, w_o, b_o, g, b):
        kvc, kv_tb = ctx
        q, gates = _qg_proj_heads(x, bsz, w_qg, b_qg)
        o = _attn_prompt(q, gates, kvc, kv_tb, bias_tiles, rel_bias)
        return _oproj_heads(o, x, w_o, b_o, g, b)

    prompt_attend.prepare = prompt_prepare
    zeros_state = jnp.zeros((depth, bsz, CONV_W - 1, d_ff), x_prompt.dtype)
    y_p, conv_p, _, kv_p = trunk(x_prompt.reshape(bsz * seq, d_model), seq, zeros_state, prompt_attend)
    kv_p = kv_p.reshape(bsz, 3, 2, N_KV, HEAD_DIM, seq)

    cmp_pages = _tokens_last(cache_kv_cmp)
    slc_pages = _tokens_last(cache_kv_slc)
    win_t = _tokens_last(cache_win)

    def sample_prepare(x):
        kv_t, _ = _kv_proj(x, 1, kv_w)
        kvc = _compress_pages(cmp_pages, page_table, *cmp_w)
        return kv_t[0], (kvc, kv_t[0])

    def sample_attend(x, ctx, w_qg, b_qg, w_o, b_o, g, b):
        kvc, kv_new_t = ctx
        q, gates = _qg_proj_flat(x, w_qg, b_qg)
        o = _attn_sample(q.reshape(dbsz, n_new, -1), gates.reshape(dbsz, n_new, -1), kvc, kv_new_t, slc_pages,
                         page_table, win_t, rel_bias)
        return _oproj_flat(o.reshape(dbsz * n_new, -1), x, w_o, b_o, g, b)

    sample_attend.prepare = sample_prepare
    assert n_new < BLK
    y_s, conv_s, sgv_s, kv_s = trunk(x_sample.reshape(dbsz * n_new, d_model), n_new, state_conv, sample_attend)
    kv_s = kv_s.reshape(3, 2, N_KV, HEAD_DIM, dbsz, n_new)

    win_len = min(WINDOW, seq)
    out_p = [_tokens_first(kv_p[:, s]) for s in range(3)]
    out_s = [jnp.transpose(kv_s[s], (3, 4, 0, 1, 2)) for s in range(3)]
    return (y_p.reshape(bsz, seq, d_model), y_s.reshape(dbsz, n_new, d_model),
            out_p[0], out_p[1], out_p[2][:, seq - win_len:], conv_p,
            out_s[0], out_s[1], out_s[2], conv_s, sgv_s)
```

```python
import functools
import math

import numpy as np
import jax
import jax.numpy as jnp
from jax import lax
from jax.experimental import pallas as pl
from jax.experimental.pallas import tpu as pltpu

N_SG_GROUPS = 8
CHUNK = 128
CONV_W = 3
HEAD_DIM = 64
N_KV = 4
BLK = 64
N_SEL = 16
WINDOW = 512
NUM_BUCKETS = 32
REL_MAX_DIST = 1024
Q_BLOCK = 128
PAGE_SIZE = 128
DEPTH = 2
ALPHA = (2 * DEPTH) ** 0.25
LN_EPS = 1e-5
NEG_INF = -1e30
FORCED_SCORE = 1e9

MXU_DTYPE = jnp.bfloat16
F32 = jnp.float32
LANES = 128
ROW_TILE = 256
VMEM_LIMIT = 56 * 1024 * 1024


def _bucket_thresholds():
    d = np.arange(0, 4 * REL_MAX_DIST)
    max_exact = NUM_BUCKETS // 2
    nf = np.maximum(d, 1).astype(np.float32)
    large = max_exact + (np.log(nf / np.float32(max_exact)) / np.float32(math.log(REL_MAX_DIST / max_exact))
                         * np.float32(NUM_BUCKETS - max_exact)).astype(np.int32)
    b = np.where(d < max_exact, d, np.minimum(large, NUM_BUCKETS - 1))
    assert np.all(np.diff(b) >= 0) and b[-1] == NUM_BUCKETS - 1
    return [int(np.nonzero(b >= j)[0][0]) for j in range(NUM_BUCKETS)]


BUCKET_THR = _bucket_thresholds()
FAR_DIST = BUCKET_THR[-1]
N_BIAS_TILES = -(-(FAR_DIST + LANES) // LANES)


def _gelu(x):
    return 0.5 * x * (1.0 + lax.erf(x * (1.0 / math.sqrt(2.0))))


def _layer_norm(x, g, b):
    mu = jnp.mean(x, axis=-1, keepdims=True)
    xc = x - mu
    var = jnp.mean(xc * xc, axis=-1, keepdims=True)
    return xc * lax.rsqrt(var + LN_EPS) * g + b


def _mm(a, b):
    return jnp.dot(a, b, preferred_element_type=F32)


def _mm_nt(a, b):
    return lax.dot_general(a, b, (((1,), (1,)), ((), ())), preferred_element_type=F32)


def _tn(a, b):
    return lax.dot_general(a, b, (((0,), (0,)), ((), ())), preferred_element_type=F32)


def _const_spec(shape):
    nd = len(shape)
    return pl.BlockSpec(shape, lambda *_: (0,) * nd, pipeline_mode=pl.Buffered(1))


def _params(sem):
    return pltpu.CompilerParams(dimension_semantics=sem, vmem_limit_bytes=VMEM_LIMIT)


def _bucket_of(dist):
    b = jnp.zeros(dist.shape, jnp.int32)
    for j in range(1, NUM_BUCKETS):
        b = b + (dist >= BUCKET_THR[j]).astype(jnp.int32)
    return b


def _sgu_kernel(x_ref, win_ref, bin_ref, gv_ref, bv_ref, wg_ref, bs_ref, wout_ref, bout_ref, lng_ref, lnb_ref,
                o_ref, v_ref, *, d_sg):
    x = x_ref[...]
    xb = x.astype(MXU_DTYPE)
    v = _gelu(_mm(xb, win_ref[:, d_sg:]) + bin_ref[:, d_sg:])
    vn = _layer_norm(v, gv_ref[...], bv_ref[...])
    v_ref[...] = vn
    cg = d_sg // N_SG_GROUPS
    acc = jnp.zeros(o_ref.shape, F32)
    for g in range(N_SG_GROUPS):
        lo, hi = g * cg, (g + 1) * cg
        gate = _mm(wg_ref[g], vn[:, lo:hi].astype(MXU_DTYPE)) + bs_ref[g]
        u = _gelu(_mm(xb, win_ref[:, lo:hi]) + bin_ref[:, lo:hi])
        acc = acc + _mm((u * gate).astype(MXU_DTYPE), wout_ref[lo:hi, :])
    y = acc + bout_ref[...]
    o_ref[...] = _layer_norm(ALPHA * x + y, lng_ref[...], lnb_ref[...])


def _sgu_layer(x, seq_len, w_in, b_in, g_v, b_v, w_s, b_s, w_out, b_out, ln_g, ln_b):
    t, d = x.shape
    d_sg = w_out.shape[0]
    cg = d_sg // N_SG_GROUPS
    cl = min(CHUNK, seq_len)
    assert seq_len % cl == 0 and ROW_TILE % cl == 0 and t % ROW_TILE == 0
    tm = ROW_TILE
    w_tri = jnp.tril(w_s[:, :cl, :cl])
    eye = jnp.eye(tm // cl, dtype=w_s.dtype)
    wg = jnp.einsum('ab,gts->gatbs', eye, w_tri).reshape(N_SG_GROUPS, tm, tm).astype(MXU_DTYPE)
    bs = jnp.broadcast_to(jnp.tile(b_s[:, :cl], (1, tm // cl))[:, :, None], (N_SG_GROUPS, tm, cg))
    row = lambda a: a.reshape(1, -1)
    out, v = pl.pallas_call(
        functools.partial(_sgu_kernel, d_sg=d_sg),
        grid=(t // tm,),
        in_specs=[pl.BlockSpec((tm, d), lambda i: (i, 0)),
                  _const_spec((d, 2 * d_sg)), _const_spec((1, 2 * d_sg)), _const_spec((1, d_sg)), _const_spec((1, d_sg)),
                  _const_spec((N_SG_GROUPS, tm, tm)), _const_spec((N_SG_GROUPS, tm, cg)),
                  _const_spec((d_sg, d)), _const_spec((1, d)), _const_spec((1, d)), _const_spec((1, d))],
        out_specs=[pl.BlockSpec((tm, d), lambda i: (i, 0)), pl.BlockSpec((tm, d_sg), lambda i: (i, 0))],
        out_shape=[jax.ShapeDtypeStruct((t, d), F32), jax.ShapeDtypeStruct((t, d_sg), F32)],
        compiler_params=_params(("parallel",)),
        name="sgu_layer",
    )(x, w_in.astype(MXU_DTYPE), row(b_in), row(g_v), row(b_v), wg, bs, w_out.astype(MXU_DTYPE), row(b_out),
      row(ln_g), row(ln_b))
    return out, v


def _ffn_tail(x, a, gt, s1, s2, wdw_ref, bdw_ref, wdown_ref, bdown_ref, lng_ref, lnb_ref, o_ref):
    c = bdw_ref[...] + s2 * wdw_ref[0:1, :] + s1 * wdw_ref[1:2, :] + a * wdw_ref[2:3, :]
    h = (_gelu(c) * gt).astype(MXU_DTYPE)
    y = _mm(h, wdown_ref[...]) + bdown_ref[...]
    o_ref[...] = _layer_norm(ALPHA * x + y, lng_ref[...], lnb_ref[...])


def _ffn_long_kernel(x_ref, xh_ref, st_ref, wup_ref, bup_ref, wdw_ref, bdw_ref, wdown_ref, bdown_ref, lng_ref, lnb_ref,
                     o_ref, conv_ref, *, seq_len, d_ff):
    tm = x_ref.shape[0]
    x = x_ref[...]
    xb = x.astype(MXU_DTYPE)
    a = _mm(xb, wup_ref[:, :d_ff]) + bup_ref[:, :d_ff]
    gt = _mm(xb, wup_ref[:, d_ff:]) + bup_ref[:, d_ff:]
    ah = _mm(xh_ref[...].astype(MXU_DTYPE), wup_ref[:, :d_ff]) + bup_ref[:, :d_ff]
    at_start = (pl.program_id(0) * tm) % seq_len == 0
    p1 = jnp.where(at_start, st_ref[0, 1:2, :], ah[7:8, :])
    p0 = jnp.where(at_start, st_ref[0, 0:1, :], ah[6:7, :])
    r = lax.broadcasted_iota(jnp.int32, a.shape, 0)
    s1 = jnp.where(r == 0, p1, pltpu.roll(a, 1, 0))
    s2 = jnp.where(r == 0, p0, jnp.where(r == 1, p1, pltpu.roll(a, 2, 0)))
    conv_ref[0] = a[tm - (CONV_W - 1):, :]
    _ffn_tail(x, a, gt, s1, s2, wdw_ref, bdw_ref, wdown_ref, bdown_ref, lng_ref, lnb_ref, o_ref)


def _ffn_short_kernel(x_ref, p0_ref, p1_ref, wup_ref, bup_ref, wdw_ref, bdw_ref, wdown_ref, bdown_ref, lng_ref, lnb_ref,
                      o_ref, a_ref, *, seq_len, d_ff):
    x = x_ref[...]
    xb = x.astype(MXU_DTYPE)
    a = _mm(xb, wup_ref[:, :d_ff]) + bup_ref[:, :d_ff]
    gt = _mm(xb, wup_ref[:, d_ff:]) + bup_ref[:, d_ff:]
    pos = lax.broadcasted_iota(jnp.int32, a.shape, 0) % seq_len
    s1 = jnp.where(pos == 0, p1_ref[...], pltpu.roll(a, 1, 0))
    s2 = jnp.where(pos == 0, p0_ref[...], jnp.where(pos == 1, p1_ref[...], pltpu.roll(a, 2, 0)))
    a_ref[...] = a
    _ffn_tail(x, a, gt, s1, s2, wdw_ref, bdw_ref, wdown_ref, bdown_ref, lng_ref, lnb_ref, o_ref)


def _ffn_layer(x, seq_len, conv_state, w_up, b_up, w_dw, b_dw, w_down, b_down, ln_g, ln_b):
    t, d = x.shape
    d_ff = w_down.shape[0]
    n_seq = t // seq_len
    tm = ROW_TILE
    assert seq_len >= CONV_W - 1 and t % tm == 0
    row = lambda a: a.reshape(1, -1)
    weights = (w_up.astype(MXU_DTYPE), row(b_up), w_dw, row(b_dw), w_down.astype(MXU_DTYPE), row(b_down), row(ln_g), row(ln_b))
    w_specs = [_const_spec((d, 2 * d_ff)), _const_spec((1, 2 * d_ff)), _const_spec((CONV_W, d_ff)), _const_spec((1, d_ff)),
               _const_spec((d_ff, d)), _const_spec((1, d)), _const_spec((1, d)), _const_spec((1, d))]
    if seq_len % tm == 0:
        tiles_per_seq = seq_len // tm
        out, conv = pl.pallas_call(
            functools.partial(_ffn_long_kernel, seq_len=seq_len, d_ff=d_ff),
            grid=(t // tm,),
            in_specs=[pl.BlockSpec((tm, d), lambda i: (i, 0)),
                      pl.BlockSpec((8, d), lambda i: (jnp.maximum(i * (tm // 8) - 1, 0), 0)),
                      pl.BlockSpec((1, CONV_W - 1, d_ff), lambda i: (i // tiles_per_seq, 0, 0))] + w_specs,
            out_specs=[pl.BlockSpec((tm, d), lambda i: (i, 0)),
                       pl.BlockSpec((1, CONV_W - 1, d_ff), lambda i: (i // tiles_per_seq, 0, 0))],
            out_shape=[jax.ShapeDtypeStruct((t, d), F32), jax.ShapeDtypeStruct((n_seq, CONV_W - 1, d_ff), F32)],
            compiler_params=_params(("arbitrary",)),
            name="ffn_long",
        )(x, x, conv_state, *weights)
        return out, conv
    assert tm % seq_len == 0
    p0 = jnp.repeat(conv_state[:, 0, :], seq_len, axis=0)
    p1 = jnp.repeat(conv_state[:, 1, :], seq_len, axis=0)
    out, a = pl.pallas_call(
        functools.partial(_ffn_short_kernel, seq_len=seq_len, d_ff=d_ff),
        grid=(t // tm,),
        in_specs=[pl.BlockSpec((tm, d), lambda i: (i, 0)), pl.BlockSpec((tm, d_ff), lambda i: (i, 0)),
                  pl.BlockSpec((tm, d_ff), lambda i: (i, 0))] + w_specs,
        out_specs=[pl.BlockSpec((tm, d), lambda i: (i, 0)), pl.BlockSpec((tm, d_ff), lambda i: (i, 0))],
        out_shape=[jax.ShapeDtypeStruct((t, d), F32), jax.ShapeDtypeStruct((t, d_ff), F32)],
        compiler_params=_params(("parallel",)),
        name="ffn_short",
    )(x, p0, p1, *weights)
    return out, a.reshape(n_seq, seq_len, d_ff)[:, seq_len - (CONV_W - 1):, :]


def _kv_kernel(x_ref, wt_ref, wk_ref, kv_ref, vt_ref, k_ref):
    xb = x_ref[...].astype(MXU_DTYPE)
    r = _mm_nt(wt_ref[...], xb)
    kv_ref[0] = r
    gd = N_KV * HEAD_DIM
    for s in range(2):
        vt_ref[0, s] = r[(2 * s + 3) * gd:(2 * s + 4) * gd, :].astype(MXU_DTYPE)
        for g in range(N_KV):
            k_ref[0, s, g] = _mm(xb, wk_ref[s, g]).astype(MXU_DTYPE)


def _kv_proj(x, n_seq, kv_w):
    t, d = x.shape
    seq_len = t // n_seq
    n_out = kv_w.shape[1]
    gd = N_KV * HEAD_DIM
    tm = ROW_TILE
    assert seq_len % tm == 0 and n_out == 6 * gd
    tiles = seq_len // tm
    wk = kv_w.reshape(d, 3, 2, N_KV, HEAD_DIM)[:, 1:, 0].transpose(1, 2, 0, 3).astype(MXU_DTYPE)
    return pl.pallas_call(
        _kv_kernel,
        grid=(t // tm,),
        in_specs=[pl.BlockSpec((tm, d), lambda i: (i, 0)), _const_spec((n_out, d)), _const_spec(wk.shape)],
        out_specs=[pl.BlockSpec((1, n_out, tm), lambda i: (i // tiles, 0, i % tiles)),
                   pl.BlockSpec((1, 2, gd, tm), lambda i: (i // tiles, 0, 0, i % tiles)),
                   pl.BlockSpec((1, 2, N_KV, tm, HEAD_DIM), lambda i: (i // tiles, 0, 0, i % tiles, 0))],
        out_shape=[jax.ShapeDtypeStruct((n_seq, n_out, seq_len), F32),
                   jax.ShapeDtypeStruct((n_seq, 2, gd, seq_len), MXU_DTYPE),
                   jax.ShapeDtypeStruct((n_seq, 2, N_KV, seq_len, HEAD_DIM), MXU_DTYPE)],
        compiler_params=_params(("parallel",)),
        name="kv_proj",
    )(x, kv_w.T.astype(MXU_DTYPE), wk)


def _sigmoid(x):
    return 1.0 / (1.0 + jnp.exp(-x))


GATE_ROWS = 16


def _qg_t_kernel(x_ref, wq_ref, bq_ref, wg_ref, bg_ref, q_ref, gate_ref):
    xb = x_ref[...].astype(MXU_DTYPE)
    q_ref[0] = ((_mm_nt(wq_ref[...], xb) + bq_ref[...]) * (HEAD_DIM ** -0.5)).astype(MXU_DTYPE)
    gate_ref[0] = _sigmoid(_mm_nt(wg_ref[...], xb) + bg_ref[...])


def _qg_proj_t(x, n_seq, w_qg, b_qg):
    t, d = x.shape
    seq_len = t // n_seq
    n_heads = (w_qg.shape[1] // (HEAD_DIM + 3))
    group = n_heads // N_KV
    nq = n_heads * HEAD_DIM
    tm = ROW_TILE
    assert seq_len % tm == 0 and 3 * group <= GATE_ROWS
    tiles = seq_len // tm
    pad = GATE_ROWS - 3 * group
    wg = jnp.pad(w_qg[:, nq:].T.reshape(N_KV, 3 * group, d), ((0, 0), (0, pad), (0, 0))).reshape(N_KV * GATE_ROWS, d)
    bg = jnp.pad(b_qg[nq:].reshape(N_KV, 3 * group), ((0, 0), (0, pad))).reshape(N_KV * GATE_ROWS, 1)
    bg = jnp.broadcast_to(bg, (N_KV * GATE_ROWS, tm))
    bq = jnp.broadcast_to(b_qg[:nq].reshape(nq, 1), (nq, tm))
    return pl.pallas_call(
        _qg_t_kernel,
        grid=(t // tm,),
        in_specs=[pl.BlockSpec((tm, d), lambda i: (i, 0)), _const_spec((nq, d)), _const_spec((nq, tm)),
                  _const_spec(wg.shape), _const_spec(bg.shape)],
        out_specs=[pl.BlockSpec((1, nq, tm), lambda i: (i // tiles, 0, i % tiles)),
                   pl.BlockSpec((1, N_KV * GATE_ROWS, tm), lambda i: (i // tiles, 0, i % tiles))],
        out_shape=[jax.ShapeDtypeStruct((n_seq, nq, seq_len), MXU_DTYPE),
                   jax.ShapeDtypeStruct((n_seq, N_KV * GATE_ROWS, seq_len), F32)],
        compiler_params=_params(("parallel",)),
        name="qg_proj_t",
    )(x, w_qg[:, :nq].T.astype(MXU_DTYPE), bq, wg.astype(MXU_DTYPE), bg)


def _qg_flat_kernel(x_ref, wq_ref, bq_ref, wg_ref, bg_ref, q_ref, gate_ref):
    xb = x_ref[...].astype(MXU_DTYPE)
    q_ref[...] = (_mm(xb, wq_ref[...]) + bq_ref[...]) * (HEAD_DIM ** -0.5)
    gate_ref[...] = _sigmoid(_mm(xb, wg_ref[...]) + bg_ref[...])


def _qg_proj_flat(x, w_qg, b_qg):
    t, d = x.shape
    n_gate = w_qg.shape[1] // (HEAD_DIM + 3) * 3
    nq = w_qg.shape[1] - n_gate
    tm = ROW_TILE
    assert t % tm == 0 and n_gate <= LANES
    wg = jnp.pad(w_qg[:, nq:], ((0, 0), (0, LANES - n_gate))).astype(MXU_DTYPE)
    bg = jnp.pad(b_qg[nq:], (0, LANES - n_gate)).reshape(1, LANES)
    return pl.pallas_call(
        _qg_flat_kernel,
        grid=(t // tm,),
        in_specs=[pl.BlockSpec((tm, d), lambda i: (i, 0)), _const_spec((d, nq)), _const_spec((1, nq)),
                  _const_spec((d, LANES)), _const_spec((1, LANES))],
        out_specs=[pl.BlockSpec((tm, nq), lambda i: (i, 0)), pl.BlockSpec((tm, LANES), lambda i: (i, 0))],
        out_shape=[jax.ShapeDtypeStruct((t, nq), F32), jax.ShapeDtypeStruct((t, LANES), F32)],
        compiler_params=_params(("parallel",)),
        name="qg_proj_flat",
    )(x, w_qg[:, :nq].astype(MXU_DTYPE), b_qg[:nq].reshape(1, nq), wg, bg)


def _oproj_t_kernel(o_ref, x_ref, wo_ref, bo_ref, lng_ref, lnb_ref, y_ref):
    y = _tn(o_ref[0], wo_ref[...]) + bo_ref[...]
    y_ref[...] = _layer_norm(ALPHA * x_ref[...] + y, lng_ref[...], lnb_ref[...])


def _oproj_t(o_t, x, w_o, b_o, ln_g, ln_b):
    n_seq, nq, seq_len = o_t.shape
    t, d = x.shape
    tm = ROW_TILE
    tiles = seq_len // tm
    row = lambda a: a.reshape(1, -1)
    return pl.pallas_call(
        _oproj_t_kernel,
        grid=(t // tm,),
        in_specs=[pl.BlockSpec((1, nq, tm), lambda i: (i // tiles, 0, i % tiles)),
                  pl.BlockSpec((tm, d), lambda i: (i, 0)), _const_spec((nq, d)),
                  _const_spec((1, d)), _const_spec((1, d)), _const_spec((1, d))],
        out_specs=pl.BlockSpec((tm, d), lambda i: (i, 0)),
        out_shape=jax.ShapeDtypeStruct((t, d), F32),
        compiler_params=_params(("parallel",)),
        name="oproj_t",
    )(o_t, x, w_o.astype(MXU_DTYPE), row(b_o), row(ln_g), row(ln_b))


def _oproj_flat_kernel(o_ref, x_ref, wo_ref, bo_ref, lng_ref, lnb_ref, y_ref):
    y = _mm(o_ref[...].astype(MXU_DTYPE), wo_ref[...]) + bo_ref[...]
    y_ref[...] = _layer_norm(ALPHA * x_ref[...] + y, lng_ref[...], lnb_ref[...])


def _oproj_flat(o, x, w_o, b_o, ln_g, ln_b):
    t, d = x.shape
    tm = ROW_TILE
    row = lambda a: a.reshape(1, -1)
    return pl.pallas_call(
        _oproj_flat_kernel,
        grid=(t // tm,),
        in_specs=[pl.BlockSpec((tm, o.shape[1]), lambda i: (i, 0)), pl.BlockSpec((tm, d), lambda i: (i, 0)),
                  _const_spec(w_o.shape), _const_spec((1, d)), _const_spec((1, d)), _const_spec((1, d))],
        out_specs=pl.BlockSpec((tm, d), lambda i: (i, 0)),
        out_shape=jax.ShapeDtypeStruct((t, d), F32),
        compiler_params=_params(("parallel",)),
        name="oproj_flat",
    )(o, x, w_o.astype(MXU_DTYPE), row(b_o), row(ln_g), row(ln_b))


def _bias_tile_kernel(rel_ref, o_ref):
    h = pl.program_id(0)
    far = rel_ref[NUM_BUCKETS - 1, h]
    key = lax.broadcasted_iota(jnp.int32, (LANES, Q_BLOCK), 0)
    qry = lax.broadcasted_iota(jnp.int32, (LANES, Q_BLOCK), 1)
    for dl in range(o_ref.shape[1]):
        dist = dl * LANES + qry - key
        val = jnp.full((LANES, Q_BLOCK), rel_ref[0, h] - far, F32)
        for j in range(1, NUM_BUCKETS):
            val = jnp.where(dist >= BUCKET_THR[j], rel_ref[j, h] - far, val)
        o_ref[0, dl] = val


def _bias_tiles(rel_bias):
    n_heads = rel_bias.shape[1]
    return pl.pallas_call(
        _bias_tile_kernel,
        grid=(n_heads,),
        in_specs=[pl.BlockSpec(memory_space=pltpu.SMEM)],
        out_specs=pl.BlockSpec((1, N_BIAS_TILES + 1, LANES, Q_BLOCK), lambda h: (h, 0, 0, 0)),
        out_shape=jax.ShapeDtypeStruct((n_heads, N_BIAS_TILES + 1, LANES, Q_BLOCK), F32),
        compiler_params=_params(("parallel",)),
        name="bias_tiles",
    )(rel_bias)


CMP_PAGES = 16
PE_ROWS = 8


def _compress_kernel(pt_ref, *refs, n_pages):
    del pt_ref
    page_refs = refs[:n_pages]
    pe_ref, w1_ref, b1_ref, w2_ref, b2_ref, o_ref, buf_ref = refs[n_pages:]
    for j in range(n_pages):
        buf_ref[j] = page_refs[j][0]
    rows = N_KV * n_pages
    for k in range(2):
        acc = jnp.zeros((rows + PE_ROWS, 2 * LANES), F32)
        for dd in range(HEAD_DIM // 2):
            parts = [jnp.concatenate([buf_ref[:, k, g, 2 * dd, :], buf_ref[:, k, g, 2 * dd + 1, :]], axis=1)
                     for g in range(N_KV)]
            parts.append(pe_ref[k, dd])
            acc = acc + _mm(jnp.concatenate(parts, axis=0).astype(MXU_DTYPE), w1_ref[k, dd])
        pre = acc[:rows] + acc[rows:rows + 1] + b1_ref[k]
        o_ref[0, 0, k] = _mm(_gelu(pre).astype(MXU_DTYPE), w2_ref[k]) + b2_ref[k]


def _compress_pages(pages, page_table, cmp_pe, cmp_w1, cmp_b1, cmp_w2, cmp_b2):
    n_seq, pps = page_table.shape
    bpp = PAGE_SIZE // BLK
    hid = cmp_w1.shape[-1]
    assert pps % CMP_PAGES == 0 and bpp * hid == 2 * LANES and bpp * HEAD_DIM == LANES
    n_chunks = pps // CMP_PAGES
    w1r = cmp_w1.reshape(2, BLK, HEAD_DIM // 2, 2, hid)
    eye = jnp.eye(bpp, dtype=cmp_w1.dtype)
    w1bd = jnp.einsum('kldeh,nm->kdenlmh', w1r, eye).reshape(2, HEAD_DIM // 2, 2 * PAGE_SIZE, bpp * hid).astype(MXU_DTYPE)
    pe_t = jnp.tile(jnp.swapaxes(cmp_pe, 1, 2), (1, 1, bpp))
    pe_rows = jnp.pad(pe_t.reshape(2, HEAD_DIM // 2, 1, 2 * PAGE_SIZE), ((0, 0), (0, 0), (0, PE_ROWS - 1), (0, 0)))
    b1 = jnp.tile(cmp_b1, (1, bpp)).reshape(2, 1, bpp * hid)
    w2bd = jnp.einsum('khd,nm->knhmd', cmp_w2, eye).reshape(2, bpp * hid, bpp * HEAD_DIM).astype(MXU_DTYPE)
    b2 = jnp.tile(cmp_b2, (1, bpp)).reshape(2, 1, bpp * HEAD_DIM)
    rows = N_KV * CMP_PAGES
    page_spec = lambda j: pl.BlockSpec((1, 2, N_KV, HEAD_DIM, PAGE_SIZE),
                                       lambda b, c, pt: (pt[b, c * CMP_PAGES + j], 0, 0, 0, 0))
    const = lambda a: pl.BlockSpec(a.shape, lambda b, c, pt: (0,) * a.ndim, pipeline_mode=pl.Buffered(1))
    out = pl.pallas_call(
        functools.partial(_compress_kernel, n_pages=CMP_PAGES),
        grid_spec=pltpu.PrefetchScalarGridSpec(
            num_scalar_prefetch=1,
            grid=(n_seq, n_chunks),
            in_specs=[page_spec(j) for j in range(CMP_PAGES)] + [const(a) for a in (pe_rows, w1bd, b1, w2bd, b2)],
            out_specs=pl.BlockSpec((1, 1, 2, rows, bpp * HEAD_DIM), lambda b, c, pt: (b, c, 0, 0, 0)),
            scratch_shapes=[pltpu.VMEM((CMP_PAGES, 2, N_KV, HEAD_DIM, PAGE_SIZE), F32)]),
        out_shape=jax.ShapeDtypeStruct((n_seq, n_chunks, 2, rows, bpp * HEAD_DIM), F32),
        compiler_params=_params(("parallel", "arbitrary")),
        name="compress_pages",
    )(page_table, *([pages] * CMP_PAGES), pe_rows, w1bd, b1, w2bd, b2)
    out = out.reshape(n_seq, n_chunks, 2, N_KV, CMP_PAGES, bpp, HEAD_DIM)
    return out.transpose(0, 2, 3, 1, 4, 5, 6).reshape(n_seq, 2, N_KV, pps * bpp, HEAD_DIM)


PAIR = 2
WIN_SUBS = WINDOW // LANES + 1


def _attn_prompt_kernel(rel_ref, qt_ref, gate_ref, kc_ref, vct_ref, sk_ref, svt_ref, wk_ref, wvt_ref, tb_ref, o_ref,
                        sel_ref, *, group, n_cmp, n_sub):
    g = pl.program_id(1)
    i = pl.program_id(2)
    qt = jnp.concatenate([qt_ref[0, r] for r in range(group)], axis=1)
    lanes = lambda x, r: x[:, r * Q_BLOCK:(r + 1) * Q_BLOCK]

    blk = lax.broadcasted_iota(jnp.int32, (n_cmp, Q_BLOCK), 0)
    qpos_c = i * Q_BLOCK + lax.broadcasted_iota(jnp.int32, (n_cmp, Q_BLOCK), 1)
    dist_c = qpos_c - (blk * BLK + BLK - 1)
    ok_c = dist_c >= 0
    ok_f = ok_c.astype(F32)
    bkt = _bucket_of(dist_c)
    lct = _mm(kc_ref[0, 0], qt)
    imp = jnp.zeros((n_cmp, Q_BLOCK), F32)
    pcs = []
    for r in range(group):
        h = g * group + r
        far = rel_ref[NUM_BUCKETS - 1, h]
        bias = jnp.full((n_cmp, Q_BLOCK), rel_ref[0, h] - far, F32)
        for j in range(1, NUM_BUCKETS - 1):
            bias = jnp.where(bkt == j, rel_ref[j, h] - far, bias)
        bias = jnp.where(bkt == NUM_BUCKETS - 1, 0.0, bias)
        lg = jnp.where(ok_c, lanes(lct, r) + bias, NEG_INF)
        p = jnp.exp(lg - jnp.max(lg, axis=0, keepdims=True)) * ok_f
        pc = p * (1.0 / jnp.maximum(jnp.sum(p, axis=0, keepdims=True), 1e-30))
        imp = imp + pc
        pcs.append(pc.astype(MXU_DTYPE))
    o_c = _mm(vct_ref[0, 0], jnp.concatenate(pcs, axis=1))

    cur = qpos_c // BLK
    forced = (blk == 0) | (blk == cur) | (blk == cur - 1)
    score = jnp.where(forced, FORCED_SCORE, jnp.where(blk <= cur, imp, -1.0))
    sel = jnp.zeros((n_cmp, Q_BLOCK), F32)
    for _ in range(min(N_SEL, n_cmp)):
        mx = jnp.max(score, axis=0, keepdims=True)
        idx = jnp.min(jnp.where(score == mx, blk, n_cmp), axis=0, keepdims=True)
        hit = blk == idx
        sel = jnp.where(hit, 1.0, sel)
        score = jnp.where(hit, -2.0, score)
    sel_ref[...] = sel

    key = lax.broadcasted_iota(jnp.int32, (LANES, Q_BLOCK), 0)
    qpos = i * Q_BLOCK + lax.broadcasted_iota(jnp.int32, (LANES, Q_BLOCK), 1)
    bps = LANES // BLK

    def sub_logits(k_ref, t, madd, dl):
        off = pl.multiple_of(jnp.clip(t, 0, n_sub - 1) * LANES, LANES)
        s = _mm(k_ref[0, 0, 0, pl.ds(off, LANES), :], qt)
        if dl is None:
            return jnp.concatenate([lanes(s, r) + madd for r in range(group)], axis=1)
        return jnp.concatenate([lanes(s, r) + tb_ref[r, dl] + madd for r in range(group)], axis=1)

    def sel_madd(t):
        chosen = jnp.concatenate([jnp.broadcast_to(sel_ref[pl.ds(t * bps + u, 1), :], (BLK, Q_BLOCK))
                                  for u in range(bps)], axis=0)
        return jnp.where((chosen > 0.5) & (t * LANES + key <= qpos), 0.0, NEG_INF)

    def sel_step(p_idx, carry, near):
        m, l, acc = carry
        parts = []
        for u in range(PAIR):
            t = p_idx * PAIR + u
            dl = jnp.clip(i - t, 0, N_BIAS_TILES) if near else None
            parts.append(sub_logits(sk_ref, t, sel_madd(jnp.minimum(t, n_sub - 1)), dl))
        sa = jnp.concatenate(parts, axis=0)
        m_new = jnp.maximum(m, jnp.max(sa, axis=0, keepdims=True))
        a = jnp.exp(m - m_new)
        p = jnp.exp(sa - m_new)
        off = pl.multiple_of(p_idx * (PAIR * LANES), PAIR * LANES)
        pv = _mm(svt_ref[0, 0, 0, :, pl.ds(off, PAIR * LANES)], p.astype(MXU_DTYPE))
        return m_new, a * l + jnp.sum(p, axis=0, keepdims=True), a * acc + pv

    nl = group * Q_BLOCK
    init = (jnp.full((1, nl), NEG_INF, F32), jnp.zeros((1, nl), F32), jnp.zeros((HEAD_DIM, nl), F32))
    n_pairs = i // PAIR + 1
    n_far = jnp.maximum((i - N_BIAS_TILES + 1) // PAIR, 0)
    carry = lax.fori_loop(0, n_far, functools.partial(sel_step, near=False), init)
    _, l_s, acc_s = lax.fori_loop(n_far, n_pairs, functools.partial(sel_step, near=True), carry)
    o_s = acc_s * (1.0 / l_s)

    parts = []
    for u in range(WIN_SUBS):
        t = i - (WIN_SUBS - 1) + u
        dw = qpos - (t * LANES + key)
        madd = jnp.where((dw >= 0) & (dw < WINDOW) & (t >= 0), 0.0, NEG_INF)
        parts.append(sub_logits(wk_ref, t, madd, WIN_SUBS - 1 - u))
    sw = jnp.concatenate(parts, axis=0)
    pw = jnp.exp(sw - jnp.max(sw, axis=0, keepdims=True))
    acc_w = jnp.zeros((HEAD_DIM, nl), F32)
    for u in range(WIN_SUBS):
        off = pl.multiple_of(jnp.maximum(i - (WIN_SUBS - 1) + u, 0) * LANES, LANES)
        acc_w = acc_w + _mm(wvt_ref[0, 0, 0, :, pl.ds(off, LANES)], pw[u * LANES:(u + 1) * LANES].astype(MXU_DTYPE))
    o_w = acc_w * (1.0 / jnp.sum(pw, axis=0, keepdims=True))

    gt = gate_ref[0]
    for r in range(group):
        o = (gt[3 * r:3 * r + 1, :] * lanes(o_c, r) + gt[3 * r + 1:3 * r + 2, :] * lanes(o_s, r)
             + gt[3 * r + 2:3 * r + 3, :] * lanes(o_w, r))
        o_ref[0, r] = o.astype(o_ref.dtype)


def _attn_prompt(q_t, gates_t, kvc, k_rows, v_t, bias_tiles, rel_bias):
    bsz, nq, seq = q_t.shape
    n_heads = nq // HEAD_DIM
    group = n_heads // N_KV
    n_cmp = kvc.shape[3]
    n_sub = seq // LANES
    assert seq % Q_BLOCK == 0 and Q_BLOCK == LANES and n_cmp * BLK == seq and n_sub % PAIR == 0
    assert WIN_SUBS - 1 <= N_BIAS_TILES
    kc = kvc[:, 0].astype(MXU_DTYPE)
    vct = jnp.swapaxes(kvc[:, 1], -1, -2).astype(MXU_DTYPE)
    q4 = q_t.reshape(bsz, n_heads, HEAD_DIM, seq)
    v4 = v_t.reshape(bsz, 2, N_KV, HEAD_DIM, seq)
    k_spec = lambda s: pl.BlockSpec((1, 1, 1, seq, HEAD_DIM), lambda b, g, i: (b, s, g, 0, 0))
    v_spec = lambda s: pl.BlockSpec((1, 1, 1, HEAD_DIM, seq), lambda b, g, i: (b, s, g, 0, 0))
    o = pl.pallas_call(
        functools.partial(_attn_prompt_kernel, group=group, n_cmp=n_cmp, n_sub=n_sub),
        grid=(bsz, N_KV, seq // Q_BLOCK),
        in_specs=[pl.BlockSpec(memory_space=pltpu.SMEM),
                  pl.BlockSpec((1, group, HEAD_DIM, Q_BLOCK), lambda b, g, i: (b, g, 0, i)),
                  pl.BlockSpec((1, GATE_ROWS, Q_BLOCK), lambda b, g, i: (b, g, i)),
                  pl.BlockSpec((1, 1, n_cmp, HEAD_DIM), lambda b, g, i: (b, g, 0, 0)),
                  pl.BlockSpec((1, 1, HEAD_DIM, n_cmp), lambda b, g, i: (b, g, 0, 0)),
                  k_spec(0), v_spec(0), k_spec(1), v_spec(1),
                  pl.BlockSpec((group, N_BIAS_TILES + 1, LANES, Q_BLOCK), lambda b, g, i: (g, 0, 0, 0))],
        out_specs=pl.BlockSpec((1, group, HEAD_DIM, Q_BLOCK), lambda b, g, i: (b, g, 0, i)),
        out_shape=jax.ShapeDtypeStruct((bsz, n_heads, HEAD_DIM, seq), MXU_DTYPE),
        scratch_shapes=[pltpu.VMEM((n_cmp, Q_BLOCK), F32)],
        compiler_params=_params(("parallel", "parallel", "arbitrary")),
        name="attn_prompt",
    )(rel_bias, q4, gates_t, kc, vct, k_rows, v4, k_rows, v4, bias_tiles)
    return o.reshape(bsz, nq, seq)


SLC_PAGES = 16
NEW_PAD = 16


def _bias_rows(trow_ref, dist):
    bkt = _bucket_of(dist)
    bias = jnp.broadcast_to(trow_ref[0:1, :], dist.shape)
    for j in range(1, NUM_BUCKETS):
        bias = jnp.where(bkt == j, trow_ref[j:j + 1, :], bias)
    return bias


def _attn_sample_kernel(pt_ref, *refs, n_pages, past_len, n_new, n_cmp, n_blk):
    del pt_ref
    page_refs = refs[:n_pages]
    (qbd_ref, gate_ref, trow_ref, kc_ref, vct_ref, knew_ref, vnewt_ref, wkt_ref, wvt_ref, wknew_ref, wvnewt_ref,
     o_ref, sel_ref, oc_ref, m_ref, l_ref, acc_ref) = refs[n_pages:]
    c = pl.program_id(1)
    qbd = qbd_ref[0]
    gd = qbd.shape[0]
    bpp = PAGE_SIZE // BLK
    npb = past_len // BLK

    def qpos_like(shape):
        return past_len + lax.broadcasted_iota(jnp.int32, shape, 1) % n_new

    @pl.when(c == 0)
    def _():
        blk = lax.broadcasted_iota(jnp.int32, (n_cmp, LANES), 0)
        dist = qpos_like((n_cmp, LANES)) - (blk * BLK + BLK - 1)
        ok = dist >= 0
        lg = jnp.where(ok, _mm(kc_ref[0], qbd) + _bias_rows(trow_ref, dist), NEG_INF)
        p = jnp.exp(lg - jnp.max(lg, axis=0, keepdims=True)) * ok.astype(F32)
        pc = p / jnp.maximum(jnp.sum(p, axis=0, keepdims=True), 1e-30)
        oc_ref[...] = _mm(vct_ref[0], pc.astype(MXU_DTYPE))
        quarter = LANES // 4
        imp = pc + pltpu.roll(pc, quarter, 1) + pltpu.roll(pc, 2 * quarter, 1) + pltpu.roll(pc, 3 * quarter, 1)
        nbp = sel_ref.shape[0]
        imp = jnp.concatenate([imp, jnp.zeros((nbp - n_cmp, LANES), F32)], axis=0)
        blk = lax.broadcasted_iota(jnp.int32, (nbp, LANES), 0)
        cur = qpos_like((nbp, LANES)) // BLK
        forced = (blk == 0) | (blk == cur) | (blk == cur - 1)
        score = jnp.where(forced, FORCED_SCORE, jnp.where(blk <= cur, imp, -1.0))
        score = jnp.where(blk >= n_blk, -2.0, score)
        sel = jnp.zeros((nbp, LANES), F32)
        for _ in range(min(N_SEL, n_blk)):
            mx = jnp.max(score, axis=0, keepdims=True)
            idx = jnp.min(jnp.where(score == mx, blk, nbp), axis=0, keepdims=True)
            hit = blk == idx
            sel = jnp.where(hit, 1.0, sel)
            score = jnp.where(hit, -3.0, score)
        sel_ref[...] = sel
        m_ref[...] = jnp.full(m_ref.shape, NEG_INF, F32)
        l_ref[...] = jnp.zeros(l_ref.shape, F32)
        acc_ref[...] = jnp.zeros(acc_ref.shape, F32)

    tok = lax.broadcasted_iota(jnp.int32, (PAGE_SIZE, LANES), 0)
    qpos_p = qpos_like((PAGE_SIZE, LANES))
    parts = []
    for j in range(n_pages):
        pg = c * n_pages + j
        st = _tn(page_refs[j][0, 0].reshape(gd, PAGE_SIZE).astype(MXU_DTYPE), qbd)
        dist = qpos_p - (pg * PAGE_SIZE + tok)
        is_far = past_len - (pg * PAGE_SIZE + PAGE_SIZE - 1) >= FAR_DIST
        bias = lax.cond(is_far,
                        lambda: jnp.broadcast_to(trow_ref[NUM_BUCKETS - 1:NUM_BUCKETS, :], (PAGE_SIZE, LANES)),
                        lambda dist=dist: _bias_rows(trow_ref, dist))
        chosen = jnp.broadcast_to(sel_ref[pl.ds(pg * bpp + bpp - 1, 1), :], (PAGE_SIZE, LANES))
        for t in reversed(range(bpp - 1)):
            chosen = jnp.where(tok < (t + 1) * BLK, sel_ref[pl.ds(pg * bpp + t, 1), :], chosen)
        parts.append(jnp.where((chosen > 0.5) & (dist >= 0), st + bias, NEG_INF))
    s_all = jnp.concatenate(parts, axis=0)
    m_old = m_ref[...]
    m_new = jnp.maximum(m_old, jnp.max(s_all, axis=0, keepdims=True))
    a = jnp.exp(m_old - m_new)
    p = jnp.exp(s_all - m_new)
    acc = acc_ref[...] * a
    for j in range(n_pages):
        vt = page_refs[j][0, 1].reshape(gd, PAGE_SIZE).astype(MXU_DTYPE)
        acc = acc + _mm(vt, p[j * PAGE_SIZE:(j + 1) * PAGE_SIZE].astype(MXU_DTYPE))
    acc_ref[...] = acc
    l_ref[...] = a * l_ref[...] + jnp.sum(p, axis=0, keepdims=True)
    m_ref[...] = m_new

    @pl.when(c == pl.num_programs(1) - 1)
    def _():
        tok_n = lax.broadcasted_iota(jnp.int32, (NEW_PAD, LANES), 0)
        dist_n = qpos_like((NEW_PAD, LANES)) - (past_len + tok_n)
        live_n = (tok_n < n_new) & (dist_n >= 0)
        bias_n = _bias_rows(trow_ref, dist_n)
        chosen_n = sel_ref[pl.ds(npb, 1), :] > 0.5
        s_n = jnp.where(live_n & chosen_n, _mm(knew_ref[0], qbd) + bias_n, NEG_INF)
        m_old = m_ref[...]
        m_fin = jnp.maximum(m_old, jnp.max(s_n, axis=0, keepdims=True))
        a = jnp.exp(m_old - m_fin)
        p_n = jnp.exp(s_n - m_fin)
        acc_s = acc_ref[...] * a + _mm(vnewt_ref[0], p_n.astype(MXU_DTYPE))
        o_s = acc_s / (a * l_ref[...] + jnp.sum(p_n, axis=0, keepdims=True))
        wb = wkt_ref.shape[-1]
        tok_w = lax.broadcasted_iota(jnp.int32, (wb, LANES), 0)
        kwpos = past_len - wb + tok_w
        dw = qpos_like((wb, LANES)) - kwpos
        ok_w = (dw >= 0) & (dw < WINDOW) & (kwpos >= 0)
        s_w = _tn(wkt_ref[0, 0].reshape(gd, wb).astype(MXU_DTYPE), qbd) + _bias_rows(trow_ref, dw)
        s_w = jnp.where(ok_w, s_w, NEG_INF)
        s_wn = jnp.where(live_n & (dist_n < WINDOW), _mm(wknew_ref[0], qbd) + bias_n, NEG_INF)
        m_w = jnp.maximum(jnp.max(s_w, axis=0, keepdims=True), jnp.max(s_wn, axis=0, keepdims=True))
        p_w = jnp.exp(s_w - m_w)
        p_wn = jnp.exp(s_wn - m_w)
        o_w = (_mm(wvt_ref[0, 0].reshape(gd, wb).astype(MXU_DTYPE), p_w.astype(MXU_DTYPE))
               + _mm(wvnewt_ref[0], p_wn.astype(MXU_DTYPE)))
        o_w = o_w / (jnp.sum(p_w, axis=0, keepdims=True) + jnp.sum(p_wn, axis=0, keepdims=True))
        gt = gate_ref[0]
        tot = gt[0:1, :] * oc_ref[...] + gt[1:2, :] * o_s + gt[2:3, :] * o_w
        lane_g = (lax.broadcasted_iota(jnp.int32, (HEAD_DIM, LANES), 1) // n_new) % N_KV
        out = jnp.zeros((HEAD_DIM, LANES), F32)
        for g in range(N_KV):
            out = out + jnp.where(lane_g == g, tot[g * HEAD_DIM:(g + 1) * HEAD_DIM, :], 0.0)
        o_ref[0] = out


def _attn_sample(q, gates, kvc, kv_new_t, slc_pages, page_table, win_t, rel_bias):
    bsz, n_new, nq = q.shape
    n_heads = nq // HEAD_DIM
    group = n_heads // N_KV
    pps = page_table.shape[1]
    past_len = pps * PAGE_SIZE
    n_cmp = kvc.shape[3]
    n_blk = past_len // BLK + -(-n_new // BLK)
    nbp = -(-n_blk // 8) * 8
    gd = N_KV * HEAD_DIM
    wb = win_t.shape[-1]
    assert n_heads * n_new == LANES and group == 4 and n_new <= min(BLK, NEW_PAD) and pps % SLC_PAGES == 0
    assert n_cmp == past_len // BLK and n_cmp % 8 == 0
    q5 = q.reshape(bsz, n_new, N_KV, group, HEAD_DIM)
    qbd = jnp.einsum('hg,bqgrd->bhdrgq', jnp.eye(N_KV, dtype=q.dtype), q5).reshape(bsz, gd, LANES).astype(MXU_DTYPE)
    g5 = gates[:, :, :3 * n_heads].reshape(bsz, n_new, N_KV, group, 3)
    gate_t = jnp.pad(g5.transpose(0, 4, 3, 2, 1).reshape(bsz, 3, LANES), ((0, 0), (0, 5), (0, 0)))
    trow = jnp.broadcast_to(rel_bias.reshape(NUM_BUCKETS, N_KV, group).transpose(0, 2, 1)[..., None],
                            (NUM_BUCKETS, group, N_KV, n_new)).reshape(NUM_BUCKETS, LANES)
    kc = kvc[:, 0].transpose(0, 2, 1, 3).reshape(bsz, n_cmp, gd).astype(MXU_DTYPE)
    vct = kvc[:, 1].transpose(0, 1, 3, 2).reshape(bsz, gd, n_cmp).astype(MXU_DTYPE)
    new = kv_new_t.reshape(6, gd, bsz, n_new).transpose(0, 2, 1, 3)
    new = jnp.pad(new, ((0, 0), (0, 0), (0, 0), (0, NEW_PAD - n_new))).astype(MXU_DTYPE)
    k_rows = lambda s: jnp.swapaxes(new[s], 1, 2)
    page_spec = lambda j: pl.BlockSpec((1, 2, N_KV, HEAD_DIM, PAGE_SIZE),
                                       lambda b, c, pt: (pt[b, c * SLC_PAGES + j], 0, 0, 0, 0))
    per_b = lambda a: pl.BlockSpec((1,) + a.shape[1:], lambda b, c, pt: (b,) + (0,) * (a.ndim - 1))
    win_spec = lambda s: pl.BlockSpec((1, 1, N_KV, HEAD_DIM, wb), lambda b, c, pt: (b, s, 0, 0, 0))
    smalls = (qbd, gate_t)
    o_t = pl.pallas_call(
        functools.partial(_attn_sample_kernel, n_pages=SLC_PAGES, past_len=past_len, n_new=n_new, n_cmp=n_cmp,
                          n_blk=n_blk),
        grid_spec=pltpu.PrefetchScalarGridSpec(
            num_scalar_prefetch=1,
            grid=(bsz, pps // SLC_PAGES),
            in_specs=[page_spec(j) for j in range(SLC_PAGES)]
            + [per_b(qbd), per_b(gate_t), pl.BlockSpec(trow.shape, lambda b, c, pt: (0, 0)),
               per_b(kc), per_b(vct), per_b(k_rows(2)), per_b(new[3]), win_spec(0), win_spec(1),
               per_b(k_rows(4)), per_b(new[5])],
            out_specs=pl.BlockSpec((1, HEAD_DIM, LANES), lambda b, c, pt: (b, 0, 0)),
            scratch_shapes=[pltpu.VMEM((nbp, LANES), F32), pltpu.VMEM((gd, LANES), F32), pltpu.VMEM((1, LANES), F32),
                            pltpu.VMEM((1, LANES), F32), pltpu.VMEM((gd, LANES), F32)]),
        out_shape=jax.ShapeDtypeStruct((bsz, HEAD_DIM, LANES), F32),
        compiler_params=_params(("parallel", "arbitrary")),
        name="attn_sample",
    )(page_table, *([slc_pages] * SLC_PAGES), qbd, gate_t, trow, kc, vct, k_rows(2), new[3], win_t, win_t,
      k_rows(4), new[5])
    del smalls
    return o_t.reshape(bsz, HEAD_DIM, group, N_KV, n_new).transpose(0, 4, 3, 2, 1).reshape(bsz, n_new, nq)


def _tokens_last(x):
    nd = x.ndim
    return jnp.transpose(x, tuple(range(nd - 4)) + (nd - 3, nd - 2, nd - 1, nd - 4))


def _tokens_first(x):
    nd = x.ndim
    return jnp.transpose(x, tuple(range(nd - 4)) + (nd - 1, nd - 4, nd - 3, nd - 2))


def kernel(x_prompt, x_sample, cache_kv_cmp, cache_kv_slc, cache_win, state_conv, page_table, ln_g, ln_b, sg_w_in, sg_b_in, sg_ln_g, sg_ln_b, sg_w_s, sg_b_s, sg_w_out, sg_b_out, ffn_w_up, ffn_b_up, ffn_w_dw, ffn_b_dw, ffn_w_down, ffn_b_down, kv_w, cmp_pe, cmp_w1, cmp_b1, cmp_w2, cmp_b2, nsa_w_qg, nsa_b_qg, nsa_w_o, nsa_b_o, rel_bias):
    depth = ln_g.shape[0]
    n_a = sg_w_in.shape[0]
    assert depth == DEPTH
    bsz, seq, d_model = x_prompt.shape
    dbsz, n_new, _ = x_sample.shape
    d_ff = ffn_w_down.shape[1]
    kvd = (2, N_KV, HEAD_DIM)
    gd = N_KV * HEAD_DIM
    cmp_w = (cmp_pe, cmp_w1, cmp_b1, cmp_w2, cmp_b2)

    def trunk(x, seq_len, conv_state, attend_layer):
        n_seq = x.shape[0] // seq_len
        new_conv, sg_v, kv_t, ctx = [], [], None, None
        for l in range(depth):
            if l < n_a:
                x, v = _sgu_layer(x, seq_len, sg_w_in[l], sg_b_in[l], sg_ln_g[l], sg_ln_b[l], sg_w_s[l], sg_b_s[l],
                                  sg_w_out[l], sg_b_out[l], ln_g[l, 0], ln_b[l, 0])
                sg_v.append(v.reshape(n_seq, seq_len, -1))
            else:
                i = l - n_a
                x = attend_layer(x, ctx, nsa_w_qg[i], nsa_b_qg[i], nsa_w_o[i], nsa_b_o[i], ln_g[l, 0], ln_b[l, 0])
            x, cs = _ffn_layer(x, seq_len, conv_state[l], ffn_w_up[l], ffn_b_up[l], ffn_w_dw[l], ffn_b_dw[l],
                               ffn_w_down[l], ffn_b_down[l], ln_g[l, 1], ln_b[l, 1])
            new_conv.append(cs)
            if l == n_a - 1:
                kv_t, ctx = attend_layer.prepare(x)
        return x, jnp.stack(new_conv), (jnp.stack(sg_v) if sg_v else None), kv_t

    bias_tiles = _bias_tiles(rel_bias)

    def prompt_prepare(x):
        kv_t, v_t, k_rows = _kv_proj(x, bsz, kv_w)
        cmp_t = kv_t[:, :2 * gd].reshape(bsz, 2, N_KV, HEAD_DIM, seq // PAGE_SIZE, PAGE_SIZE)
        pages = cmp_t.transpose(0, 4, 1, 2, 3, 5).reshape(bsz * (seq // PAGE_SIZE), 2, N_KV, HEAD_DIM, PAGE_SIZE)
        ident = jnp.arange(bsz * (seq // PAGE_SIZE), dtype=jnp.int32).reshape(bsz, seq // PAGE_SIZE)
        kvc = _compress_pages(pages, ident, *cmp_w)
        return kv_t, (kvc, k_rows, v_t)

    def prompt_attend(x, ctx, w_qg, b_qg, w_o, b_o, g, b):
        kvc, k_rows, v_t = ctx
        q_t, gates_t = _qg_proj_t(x, bsz, w_qg, b_qg)
        o_t = _attn_prompt(q_t, gates_t, kvc, k_rows, v_t, bias_tiles, rel_bias)
        return _oproj_t(o_t, x, w_o, b_o, g, b)

    prompt_attend.prepare = prompt_prepare
    zeros_state = jnp.zeros((depth, bsz, CONV_W - 1, d_ff), x_prompt.dtype)
    y_p, conv_p, _, kv_p = trunk(x_prompt.reshape(bsz * seq, d_model), seq, zeros_state, prompt_attend)
    kv_p = kv_p.reshape(bsz, 3, 2, N_KV, HEAD_DIM, seq)

    cmp_pages = _tokens_last(cache_kv_cmp)
    slc_pages = _tokens_last(cache_kv_slc)
    win_t = _tokens_last(cache_win)

    def sample_prepare(x):
        kv_t, _, _ = _kv_proj(x, 1, kv_w)
        kvc = _compress_pages(cmp_pages, page_table, *cmp_w)
        return kv_t[0], (kvc, kv_t[0])

    def sample_attend(x, ctx, w_qg, b_qg, w_o, b_o, g, b):
        kvc, kv_new_t = ctx
        q, gates = _qg_proj_flat(x, w_qg, b_qg)
        o = _attn_sample(q.reshape(dbsz, n_new, -1), gates.reshape(dbsz, n_new, -1), kvc, kv_new_t, slc_pages,
                         page_table, win_t, rel_bias)
        return _oproj_flat(o.reshape(dbsz * n_new, -1), x, w_o, b_o, g, b)

    sample_attend.prepare = sample_prepare
    assert n_new < BLK
    y_s, conv_s, sgv_s, kv_s = trunk(x_sample.reshape(dbsz * n_new, d_model), n_new, state_conv, sample_attend)
    kv_s = kv_s.reshape(3, 2, N_KV, HEAD_DIM, dbsz, n_new)

    win_len = min(WINDOW, seq)
    out_p = [_tokens_first(kv_p[:, s]) for s in range(3)]
    out_s = [jnp.transpose(kv_s[s], (3, 4, 0, 1, 2)) for s in range(3)]
    return (y_p.reshape(bsz, seq, d_model), y_s.reshape(dbsz, n_new, d_model),
            out_p[0], out_p[1], out_p[2][:, seq - win_len:], conv_p,
            out_s[0], out_s[1], out_s[2], conv_s, sgv_s)
```

```python
import functools
import math

import numpy as np
import jax
import jax.numpy as jnp
from jax import lax
from jax.experimental import pallas as pl
from jax.experimental.pallas import tpu as pltpu

N_SG_GROUPS = 8
CHUNK = 128
CONV_W = 3
HEAD_DIM = 64
N_KV = 4
BLK = 64
N_SEL = 16
WINDOW = 512
NUM_BUCKETS = 32
REL_MAX_DIST = 1024
Q_BLOCK = 128
PAGE_SIZE = 128
DEPTH = 2
ALPHA = (2 * DEPTH) ** 0.25
LN_EPS = 1e-5
NEG_INF = -1e30
FORCED_SCORE = 1e9
LOG2E = math.log2(math.e)

MXU_DTYPE = jnp.bfloat16
F32 = jnp.float32
LANES = 128
ROW_TILE = 256
VMEM_LIMIT = 56 * 1024 * 1024


def _bucket_thresholds():
    d = np.arange(0, 4 * REL_MAX_DIST)
    max_exact = NUM_BUCKETS // 2
    nf = np.maximum(d, 1).astype(np.float32)
    large = max_exact + (np.log(nf / np.float32(max_exact)) / np.float32(math.log(REL_MAX_DIST / max_exact))
                         * np.float32(NUM_BUCKETS - max_exact)).astype(np.int32)
    b = np.where(d < max_exact, d, np.minimum(large, NUM_BUCKETS - 1))
    assert np.all(np.diff(b) >= 0) and b[-1] == NUM_BUCKETS - 1
    return [int(np.nonzero(b >= j)[0][0]) for j in range(NUM_BUCKETS)]


BUCKET_THR = _bucket_thresholds()
FAR_DIST = BUCKET_THR[-1]
N_BIAS_TILES = -(-(FAR_DIST + LANES) // LANES)


def _gelu(x):
    return 0.5 * x * (1.0 + lax.erf(x * (1.0 / math.sqrt(2.0))))


def _layer_norm(x, g, b):
    mu = jnp.mean(x, axis=-1, keepdims=True)
    xc = x - mu
    var = jnp.mean(xc * xc, axis=-1, keepdims=True)
    return xc * lax.rsqrt(var + LN_EPS) * g + b


def _mm(a, b):
    return jnp.dot(a, b, preferred_element_type=F32)


def _mm_nt(a, b):
    return lax.dot_general(a, b, (((1,), (1,)), ((), ())), preferred_element_type=F32)


def _tn(a, b):
    return lax.dot_general(a, b, (((0,), (0,)), ((), ())), preferred_element_type=F32)


def _const_spec(shape):
    nd = len(shape)
    return pl.BlockSpec(shape, lambda *_: (0,) * nd, pipeline_mode=pl.Buffered(1))


def _params(sem):
    return pltpu.CompilerParams(dimension_semantics=sem, vmem_limit_bytes=VMEM_LIMIT)


def _bucket_of(dist):
    b = jnp.zeros(dist.shape, jnp.int32)
    for j in range(1, NUM_BUCKETS):
        b = b + (dist >= BUCKET_THR[j]).astype(jnp.int32)
    return b


def _sgu_kernel(x_ref, win_ref, bin_ref, gv_ref, bv_ref, wg_ref, bs_ref, wout_ref, bout_ref, lng_ref, lnb_ref,
                o_ref, v_ref, *, d_sg):
    x = x_ref[...]
    xb = x.astype(MXU_DTYPE)
    v = _gelu(_mm(xb, win_ref[:, d_sg:]) + bin_ref[:, d_sg:])
    vn = _layer_norm(v, gv_ref[...], bv_ref[...])
    v_ref[...] = vn
    cg = d_sg // N_SG_GROUPS
    acc = jnp.zeros(o_ref.shape, F32)
    for g in range(N_SG_GROUPS):
        lo, hi = g * cg, (g + 1) * cg
        gate = _mm(wg_ref[g], vn[:, lo:hi].astype(MXU_DTYPE)) + bs_ref[g]
        u = _gelu(_mm(xb, win_ref[:, lo:hi]) + bin_ref[:, lo:hi])
        acc = acc + _mm((u * gate).astype(MXU_DTYPE), wout_ref[lo:hi, :])
    y = acc + bout_ref[...]
    o_ref[...] = _layer_norm(ALPHA * x + y, lng_ref[...], lnb_ref[...])


def _sgu_layer(x, seq_len, w_in, b_in, g_v, b_v, w_s, b_s, w_out, b_out, ln_g, ln_b):
    t, d = x.shape
    d_sg = w_out.shape[0]
    cg = d_sg // N_SG_GROUPS
    cl = min(CHUNK, seq_len)
    assert seq_len % cl == 0 and ROW_TILE % cl == 0 and t % ROW_TILE == 0
    tm = ROW_TILE
    w_tri = jnp.tril(w_s[:, :cl, :cl])
    eye = jnp.eye(tm // cl, dtype=w_s.dtype)
    wg = jnp.einsum('ab,gts->gatbs', eye, w_tri).reshape(N_SG_GROUPS, tm, tm).astype(MXU_DTYPE)
    bs = jnp.broadcast_to(jnp.tile(b_s[:, :cl], (1, tm // cl))[:, :, None], (N_SG_GROUPS, tm, cg))
    row = lambda a: a.reshape(1, -1)
    out, v = pl.pallas_call(
        functools.partial(_sgu_kernel, d_sg=d_sg),
        grid=(t // tm,),
        in_specs=[pl.BlockSpec((tm, d), lambda i: (i, 0)),
                  _const_spec((d, 2 * d_sg)), _const_spec((1, 2 * d_sg)), _const_spec((1, d_sg)), _const_spec((1, d_sg)),
                  _const_spec((N_SG_GROUPS, tm, tm)), _const_spec((N_SG_GROUPS, tm, cg)),
                  _const_spec((d_sg, d)), _const_spec((1, d)), _const_spec((1, d)), _const_spec((1, d))],
        out_specs=[pl.BlockSpec((tm, d), lambda i: (i, 0)), pl.BlockSpec((tm, d_sg), lambda i: (i, 0))],
        out_shape=[jax.ShapeDtypeStruct((t, d), F32), jax.ShapeDtypeStruct((t, d_sg), F32)],
        compiler_params=_params(("parallel",)),
        name="sgu_layer",
    )(x, w_in.astype(MXU_DTYPE), row(b_in), row(g_v), row(b_v), wg, bs, w_out.astype(MXU_DTYPE), row(b_out),
      row(ln_g), row(ln_b))
    return out, v


def _ffn_tail(x, a, gt, s1, s2, wdw_ref, bdw_ref, wdown_ref, bdown_ref, lng_ref, lnb_ref, o_ref):
    c = bdw_ref[...] + s2 * wdw_ref[0:1, :] + s1 * wdw_ref[1:2, :] + a * wdw_ref[2:3, :]
    h = (_gelu(c) * gt).astype(MXU_DTYPE)
    y = _mm(h, wdown_ref[...]) + bdown_ref[...]
    o_ref[...] = _layer_norm(ALPHA * x + y, lng_ref[...], lnb_ref[...])


def _ffn_long_kernel(x_ref, xh_ref, st_ref, wup_ref, bup_ref, wdw_ref, bdw_ref, wdown_ref, bdown_ref, lng_ref, lnb_ref,
                     o_ref, conv_ref, *, seq_len, d_ff):
    tm = x_ref.shape[0]
    x = x_ref[...]
    xb = x.astype(MXU_DTYPE)
    a = _mm(xb, wup_ref[:, :d_ff]) + bup_ref[:, :d_ff]
    gt = _mm(xb, wup_ref[:, d_ff:]) + bup_ref[:, d_ff:]
    ah = _mm(xh_ref[...].astype(MXU_DTYPE), wup_ref[:, :d_ff]) + bup_ref[:, :d_ff]
    at_start = (pl.program_id(0) * tm) % seq_len == 0
    p1 = jnp.where(at_start, st_ref[0, 1:2, :], ah[7:8, :])
    p0 = jnp.where(at_start, st_ref[0, 0:1, :], ah[6:7, :])
    r = lax.broadcasted_iota(jnp.int32, a.shape, 0)
    s1 = jnp.where(r == 0, p1, pltpu.roll(a, 1, 0))
    s2 = jnp.where(r == 0, p0, jnp.where(r == 1, p1, pltpu.roll(a, 2, 0)))
    conv_ref[0] = a[tm - (CONV_W - 1):, :]
    _ffn_tail(x, a, gt, s1, s2, wdw_ref, bdw_ref, wdown_ref, bdown_ref, lng_ref, lnb_ref, o_ref)


def _ffn_short_kernel(x_ref, p0_ref, p1_ref, wup_ref, bup_ref, wdw_ref, bdw_ref, wdown_ref, bdown_ref, lng_ref, lnb_ref,
                      o_ref, a_ref, *, seq_len, d_ff):
    x = x_ref[...]
    xb = x.astype(MXU_DTYPE)
    a = _mm(xb, wup_ref[:, :d_ff]) + bup_ref[:, :d_ff]
    gt = _mm(xb, wup_ref[:, d_ff:]) + bup_ref[:, d_ff:]
    pos = lax.broadcasted_iota(jnp.int32, a.shape, 0) % seq_len
    s1 = jnp.where(pos == 0, p1_ref[...], pltpu.roll(a, 1, 0))
    s2 = jnp.where(pos == 0, p0_ref[...], jnp.where(pos == 1, p1_ref[...], pltpu.roll(a, 2, 0)))
    a_ref[...] = a
    _ffn_tail(x, a, gt, s1, s2, wdw_ref, bdw_ref, wdown_ref, bdown_ref, lng_ref, lnb_ref, o_ref)


def _ffn_layer(x, seq_len, conv_state, w_up, b_up, w_dw, b_dw, w_down, b_down, ln_g, ln_b):
    t, d = x.shape
    d_ff = w_down.shape[0]
    n_seq = t // seq_len
    tm = ROW_TILE
    assert seq_len >= CONV_W - 1 and t % tm == 0
    row = lambda a: a.reshape(1, -1)
    weights = (w_up.astype(MXU_DTYPE), row(b_up), w_dw, row(b_dw), w_down.astype(MXU_DTYPE), row(b_down), row(ln_g), row(ln_b))
    w_specs = [_const_spec((d, 2 * d_ff)), _const_spec((1, 2 * d_ff)), _const_spec((CONV_W, d_ff)), _const_spec((1, d_ff)),
               _const_spec((d_ff, d)), _const_spec((1, d)), _const_spec((1, d)), _const_spec((1, d))]
    if seq_len % tm == 0:
        tiles_per_seq = seq_len // tm
        out, conv = pl.pallas_call(
            functools.partial(_ffn_long_kernel, seq_len=seq_len, d_ff=d_ff),
            grid=(t // tm,),
            in_specs=[pl.BlockSpec((tm, d), lambda i: (i, 0)),
                      pl.BlockSpec((8, d), lambda i: (jnp.maximum(i * (tm // 8) - 1, 0), 0)),
                      pl.BlockSpec((1, CONV_W - 1, d_ff), lambda i: (i // tiles_per_seq, 0, 0))] + w_specs,
            out_specs=[pl.BlockSpec((tm, d), lambda i: (i, 0)),
                       pl.BlockSpec((1, CONV_W - 1, d_ff), lambda i: (i // tiles_per_seq, 0, 0))],
            out_shape=[jax.ShapeDtypeStruct((t, d), F32), jax.ShapeDtypeStruct((n_seq, CONV_W - 1, d_ff), F32)],
            compiler_params=_params(("arbitrary",)),
            name="ffn_long",
        )(x, x, conv_state, *weights)
        return out, conv
    assert tm % seq_len == 0
    p0 = jnp.repeat(conv_state[:, 0, :], seq_len, axis=0)
    p1 = jnp.repeat(conv_state[:, 1, :], seq_len, axis=0)
    out, a = pl.pallas_call(
        functools.partial(_ffn_short_kernel, seq_len=seq_len, d_ff=d_ff),
        grid=(t // tm,),
        in_specs=[pl.BlockSpec((tm, d), lambda i: (i, 0)), pl.BlockSpec((tm, d_ff), lambda i: (i, 0)),
                  pl.BlockSpec((tm, d_ff), lambda i: (i, 0))] + w_specs,
        out_specs=[pl.BlockSpec((tm, d), lambda i: (i, 0)), pl.BlockSpec((tm, d_ff), lambda i: (i, 0))],
        out_shape=[jax.ShapeDtypeStruct((t, d), F32), jax.ShapeDtypeStruct((t, d_ff), F32)],
        compiler_params=_params(("parallel",)),
        name="ffn_short",
    )(x, p0, p1, *weights)
    return out, a.reshape(n_seq, seq_len, d_ff)[:, seq_len - (CONV_W - 1):, :]


STEP_SUBS = 4
STEP_BLKS = STEP_SUBS * LANES // BLK


def _kv_kernel(x_ref, wt_ref, wk_ref, kv_ref, vt_ref, k_ref, *, tiles):
    tm = x_ref.shape[0]
    xb = x_ref[...].astype(MXU_DTYPE)
    r = _mm_nt(wt_ref[...], xb)
    kv_ref[0] = r
    gd = N_KV * HEAD_DIM
    pos = (pl.program_id(0) % tiles) * tm + lax.broadcasted_iota(jnp.int32, (tm, LANES), 0)
    lane = lax.broadcasted_iota(jnp.int32, (tm, LANES), 1)
    onehot = jnp.where((pos // BLK) % STEP_BLKS + HEAD_DIM == lane, 1.0, 0.0)
    for s in range(2):
        vt_ref[0, s] = r[(2 * s + 3) * gd:(2 * s + 4) * gd, :].astype(MXU_DTYPE)
        for g in range(N_KV):
            k_ref[0, s, g] = (_mm(xb, wk_ref[s, g]) + onehot).astype(MXU_DTYPE)


def _kv_proj(x, n_seq, kv_w):
    t, d = x.shape
    seq_len = t // n_seq
    n_out = kv_w.shape[1]
    gd = N_KV * HEAD_DIM
    tm = ROW_TILE
    assert seq_len % tm == 0 and n_out == 6 * gd and HEAD_DIM + STEP_BLKS <= LANES
    tiles = seq_len // tm
    wk = kv_w.reshape(d, 3, 2, N_KV, HEAD_DIM)[:, 1:, 0].transpose(1, 2, 0, 3)
    wk = jnp.pad(wk, ((0, 0), (0, 0), (0, 0), (0, LANES - HEAD_DIM))).astype(MXU_DTYPE)
    return pl.pallas_call(
        functools.partial(_kv_kernel, tiles=tiles),
        grid=(t // tm,),
        in_specs=[pl.BlockSpec((tm, d), lambda i: (i, 0)), _const_spec((n_out, d)), _const_spec(wk.shape)],
        out_specs=[pl.BlockSpec((1, n_out, tm), lambda i: (i // tiles, 0, i % tiles)),
                   pl.BlockSpec((1, 2, gd, tm), lambda i: (i // tiles, 0, 0, i % tiles)),
                   pl.BlockSpec((1, 2, N_KV, tm, LANES), lambda i: (i // tiles, 0, 0, i % tiles, 0))],
        out_shape=[jax.ShapeDtypeStruct((n_seq, n_out, seq_len), F32),
                   jax.ShapeDtypeStruct((n_seq, 2, gd, seq_len), MXU_DTYPE),
                   jax.ShapeDtypeStruct((n_seq, 2, N_KV, seq_len, LANES), MXU_DTYPE)],
        compiler_params=_params(("parallel",)),
        name="kv_proj",
    )(x, kv_w.T.astype(MXU_DTYPE), wk)


def _sigmoid(x):
    return 1.0 / (1.0 + jnp.exp(-x))


GATE_ROWS = 16


def _qg_t_kernel(x_ref, wq_ref, bq_ref, wg_ref, bg_ref, q_ref, gate_ref):
    xb = x_ref[...].astype(MXU_DTYPE)
    q_ref[0] = ((_mm_nt(wq_ref[...], xb) + bq_ref[...]) * (HEAD_DIM ** -0.5 * LOG2E)).astype(MXU_DTYPE)
    gate_ref[0] = _sigmoid(_mm_nt(wg_ref[...], xb) + bg_ref[...])


def _qg_proj_t(x, n_seq, w_qg, b_qg):
    t, d = x.shape
    seq_len = t // n_seq
    n_heads = (w_qg.shape[1] // (HEAD_DIM + 3))
    group = n_heads // N_KV
    nq = n_heads * HEAD_DIM
    tm = ROW_TILE
    assert seq_len % tm == 0 and 3 * group <= GATE_ROWS
    tiles = seq_len // tm
    pad = GATE_ROWS - 3 * group
    wg = jnp.pad(w_qg[:, nq:].T.reshape(N_KV, 3 * group, d), ((0, 0), (0, pad), (0, 0))).reshape(N_KV * GATE_ROWS, d)
    bg = jnp.pad(b_qg[nq:].reshape(N_KV, 3 * group), ((0, 0), (0, pad))).reshape(N_KV * GATE_ROWS, 1)
    bg = jnp.broadcast_to(bg, (N_KV * GATE_ROWS, tm))
    bq = jnp.broadcast_to(b_qg[:nq].reshape(nq, 1), (nq, tm))
    return pl.pallas_call(
        _qg_t_kernel,
        grid=(t // tm,),
        in_specs=[pl.BlockSpec((tm, d), lambda i: (i, 0)), _const_spec((nq, d)), _const_spec((nq, tm)),
                  _const_spec(wg.shape), _const_spec(bg.shape)],
        out_specs=[pl.BlockSpec((1, nq, tm), lambda i: (i // tiles, 0, i % tiles)),
                   pl.BlockSpec((1, N_KV * GATE_ROWS, tm), lambda i: (i // tiles, 0, i % tiles))],
        out_shape=[jax.ShapeDtypeStruct((n_seq, nq, seq_len), MXU_DTYPE),
                   jax.ShapeDtypeStruct((n_seq, N_KV * GATE_ROWS, seq_len), F32)],
        compiler_params=_params(("parallel",)),
        name="qg_proj_t",
    )(x, w_qg[:, :nq].T.astype(MXU_DTYPE), bq, wg.astype(MXU_DTYPE), bg)


def _qg_flat_kernel(x_ref, wq_ref, bq_ref, wg_ref, bg_ref, q_ref, gate_ref):
    xb = x_ref[...].astype(MXU_DTYPE)
    q_ref[...] = (_mm(xb, wq_ref[...]) + bq_ref[...]) * (HEAD_DIM ** -0.5)
    gate_ref[...] = _sigmoid(_mm(xb, wg_ref[...]) + bg_ref[...])


def _qg_proj_flat(x, w_qg, b_qg):
    t, d = x.shape
    n_gate = w_qg.shape[1] // (HEAD_DIM + 3) * 3
    nq = w_qg.shape[1] - n_gate
    tm = ROW_TILE
    assert t % tm == 0 and n_gate <= LANES
    wg = jnp.pad(w_qg[:, nq:], ((0, 0), (0, LANES - n_gate))).astype(MXU_DTYPE)
    bg = jnp.pad(b_qg[nq:], (0, LANES - n_gate)).reshape(1, LANES)
    return pl.pallas_call(
        _qg_flat_kernel,
        grid=(t // tm,),
        in_specs=[pl.BlockSpec((tm, d), lambda i: (i, 0)), _const_spec((d, nq)), _const_spec((1, nq)),
                  _const_spec((d, LANES)), _const_spec((1, LANES))],
        out_specs=[pl.BlockSpec((tm, nq), lambda i: (i, 0)), pl.BlockSpec((tm, LANES), lambda i: (i, 0))],
        out_shape=[jax.ShapeDtypeStruct((t, nq), F32), jax.ShapeDtypeStruct((t, LANES), F32)],
        compiler_params=_params(("parallel",)),
        name="qg_proj_flat",
    )(x, w_qg[:, :nq].astype(MXU_DTYPE), b_qg[:nq].reshape(1, nq), wg, bg)


def _oproj_t_kernel(o_ref, x_ref, wo_ref, bo_ref, lng_ref, lnb_ref, y_ref):
    y = _tn(o_ref[0], wo_ref[...]) + bo_ref[...]
    y_ref[...] = _layer_norm(ALPHA * x_ref[...] + y, lng_ref[...], lnb_ref[...])


def _oproj_t(o_t, x, w_o, b_o, ln_g, ln_b):
    n_seq, nq, seq_len = o_t.shape
    t, d = x.shape
    tm = ROW_TILE
    tiles = seq_len // tm
    row = lambda a: a.reshape(1, -1)
    return pl.pallas_call(
        _oproj_t_kernel,
        grid=(t // tm,),
        in_specs=[pl.BlockSpec((1, nq, tm), lambda i: (i // tiles, 0, i % tiles)),
                  pl.BlockSpec((tm, d), lambda i: (i, 0)), _const_spec((nq, d)),
                  _const_spec((1, d)), _const_spec((1, d)), _const_spec((1, d))],
        out_specs=pl.BlockSpec((tm, d), lambda i: (i, 0)),
        out_shape=jax.ShapeDtypeStruct((t, d), F32),
        compiler_params=_params(("parallel",)),
        name="oproj_t",
    )(o_t, x, w_o.astype(MXU_DTYPE), row(b_o), row(ln_g), row(ln_b))


def _oproj_flat_kernel(o_ref, x_ref, wo_ref, bo_ref, lng_ref, lnb_ref, y_ref):
    y = _mm(o_ref[...].astype(MXU_DTYPE), wo_ref[...]) + bo_ref[...]
    y_ref[...] = _layer_norm(ALPHA * x_ref[...] + y, lng_ref[...], lnb_ref[...])


def _oproj_flat(o, x, w_o, b_o, ln_g, ln_b):
    t, d = x.shape
    tm = ROW_TILE
    row = lambda a: a.reshape(1, -1)
    return pl.pallas_call(
        _oproj_flat_kernel,
        grid=(t // tm,),
        in_specs=[pl.BlockSpec((tm, o.shape[1]), lambda i: (i, 0)), pl.BlockSpec((tm, d), lambda i: (i, 0)),
                  _const_spec(w_o.shape), _const_spec((1, d)), _const_spec((1, d)), _const_spec((1, d))],
        out_specs=pl.BlockSpec((tm, d), lambda i: (i, 0)),
        out_shape=jax.ShapeDtypeStruct((t, d), F32),
        compiler_params=_params(("parallel",)),
        name="oproj_flat",
    )(o, x, w_o.astype(MXU_DTYPE), row(b_o), row(ln_g), row(ln_b))


def _bias_tile_kernel(rel_ref, o_ref):
    h = pl.program_id(0)
    far = rel_ref[NUM_BUCKETS - 1, h]
    key = lax.broadcasted_iota(jnp.int32, (LANES, Q_BLOCK), 0)
    qry = lax.broadcasted_iota(jnp.int32, (LANES, Q_BLOCK), 1)
    for dl in range(o_ref.shape[1]):
        dist = dl * LANES + qry - key
        val = jnp.full((LANES, Q_BLOCK), (rel_ref[0, h] - far) * LOG2E, F32)
        for j in range(1, NUM_BUCKETS):
            val = jnp.where(dist >= BUCKET_THR[j], (rel_ref[j, h] - far) * LOG2E, val)
        o_ref[0, dl] = val


def _bias_tiles(rel_bias):
    n_heads = rel_bias.shape[1]
    return pl.pallas_call(
        _bias_tile_kernel,
        grid=(n_heads,),
        in_specs=[pl.BlockSpec(memory_space=pltpu.SMEM)],
        out_specs=pl.BlockSpec((1, N_BIAS_TILES + 1, LANES, Q_BLOCK), lambda h: (h, 0, 0, 0)),
        out_shape=jax.ShapeDtypeStruct((n_heads, N_BIAS_TILES + 1, LANES, Q_BLOCK), F32),
        compiler_params=_params(("parallel",)),
        name="bias_tiles",
    )(rel_bias)


CMP_PAGES = 16
PE_ROWS = 8


def _compress_kernel(pt_ref, *refs, n_pages):
    del pt_ref
    page_refs = refs[:n_pages]
    pe_ref, w1_ref, b1_ref, w2_ref, b2_ref, o_ref, buf_ref = refs[n_pages:]
    page_rows = 2 * N_KV * HEAD_DIM
    for j in range(n_pages):
        buf_ref[j * page_rows:(j + 1) * page_rows, :] = page_refs[j][0].reshape(page_rows, PAGE_SIZE)
    rows = N_KV * n_pages

    def head_dim_row(k, g, d):
        return buf_ref[pl.ds((k * N_KV + g) * HEAD_DIM + d, n_pages, stride=page_rows), :]

    for k in range(2):
        acc = jnp.zeros((rows + PE_ROWS, 2 * LANES), F32)
        for dd in range(HEAD_DIM // 2):
            parts = [jnp.concatenate([head_dim_row(k, g, 2 * dd), head_dim_row(k, g, 2 * dd + 1)], axis=1)
                     for g in range(N_KV)]
            parts.append(pe_ref[k, dd])
            acc = acc + _mm(jnp.concatenate(parts, axis=0).astype(MXU_DTYPE), w1_ref[k, dd])
        pre = acc[:rows] + acc[rows:rows + 1] + b1_ref[k]
        o_ref[0, 0, k] = _mm(_gelu(pre).astype(MXU_DTYPE), w2_ref[k]) + b2_ref[k]


def _compress_pages(pages, page_table, cmp_pe, cmp_w1, cmp_b1, cmp_w2, cmp_b2):
    n_seq, pps = page_table.shape
    bpp = PAGE_SIZE // BLK
    hid = cmp_w1.shape[-1]
    assert pps % CMP_PAGES == 0 and bpp * hid == 2 * LANES and bpp * HEAD_DIM == LANES
    n_chunks = pps // CMP_PAGES
    w1r = cmp_w1.reshape(2, BLK, HEAD_DIM // 2, 2, hid)
    eye = jnp.eye(bpp, dtype=cmp_w1.dtype)
    w1bd = jnp.einsum('kldeh,nm->kdenlmh', w1r, eye).reshape(2, HEAD_DIM // 2, 2 * PAGE_SIZE, bpp * hid).astype(MXU_DTYPE)
    pe_t = jnp.tile(jnp.swapaxes(cmp_pe, 1, 2), (1, 1, bpp))
    pe_rows = jnp.pad(pe_t.reshape(2, HEAD_DIM // 2, 1, 2 * PAGE_SIZE), ((0, 0), (0, 0), (0, PE_ROWS - 1), (0, 0)))
    b1 = jnp.tile(cmp_b1, (1, bpp)).reshape(2, 1, bpp * hid)
    w2bd = jnp.einsum('khd,nm->knhmd', cmp_w2, eye).reshape(2, bpp * hid, bpp * HEAD_DIM).astype(MXU_DTYPE)
    b2 = jnp.tile(cmp_b2, (1, bpp)).reshape(2, 1, bpp * HEAD_DIM)
    rows = N_KV * CMP_PAGES
    page_spec = lambda j: pl.BlockSpec((1, 2, N_KV, HEAD_DIM, PAGE_SIZE),
                                       lambda b, c, pt: (pt[b, c * CMP_PAGES + j], 0, 0, 0, 0))
    const = lambda a: pl.BlockSpec(a.shape, lambda b, c, pt: (0,) * a.ndim, pipeline_mode=pl.Buffered(1))
    out = pl.pallas_call(
        functools.partial(_compress_kernel, n_pages=CMP_PAGES),
        grid_spec=pltpu.PrefetchScalarGridSpec(
            num_scalar_prefetch=1,
            grid=(n_seq, n_chunks),
            in_specs=[page_spec(j) for j in range(CMP_PAGES)] + [const(a) for a in (pe_rows, w1bd, b1, w2bd, b2)],
            out_specs=pl.BlockSpec((1, 1, 2, rows, bpp * HEAD_DIM), lambda b, c, pt: (b, c, 0, 0, 0)),
            scratch_shapes=[pltpu.VMEM((CMP_PAGES * 2 * N_KV * HEAD_DIM, PAGE_SIZE), F32)]),
        out_shape=jax.ShapeDtypeStruct((n_seq, n_chunks, 2, rows, bpp * HEAD_DIM), F32),
        compiler_params=_params(("parallel", "arbitrary")),
        name="compress_pages",
    )(page_table, *([pages] * CMP_PAGES), pe_rows, w1bd, b1, w2bd, b2)
    out = out.reshape(n_seq, n_chunks, 2, N_KV, CMP_PAGES, bpp, HEAD_DIM)
    return out.transpose(0, 2, 3, 1, 4, 5, 6).reshape(n_seq, 2, N_KV, pps * bpp, HEAD_DIM)


WIN_SUBS = WINDOW // LANES + 1
CMP_LOOKBACK = (FAR_DIST + BLK - 1) // BLK + 1
CMP_BAND = -(-(CMP_LOOKBACK + Q_BLOCK // BLK + 6) // 8) * 8


def _attn_prompt_kernel(rel_ref, qt_ref, gate_ref, kc_ref, vct_ref, sk_ref, svt_ref, wk_ref, wvt_ref, tb_ref, o_ref,
                        sel_ref, lc_ref, s_ref, p_ref, *, group, n_cmp, n_sub):
    g = pl.program_id(1)
    i = pl.program_id(2)
    qt = jnp.concatenate([qt_ref[0, r] for r in range(group)], axis=1)
    lanes = lambda x, r: x[:, r * Q_BLOCK:(r + 1) * Q_BLOCK]

    nl = group * Q_BLOCK
    key = lax.broadcasted_iota(jnp.int32, (LANES, Q_BLOCK), 0)
    qpos = i * Q_BLOCK + lax.broadcasted_iota(jnp.int32, (LANES, Q_BLOCK), 1)
    zero_rows = jnp.zeros((LANES - HEAD_DIM, nl), MXU_DTYPE)
    qt_plain = jnp.concatenate([qt, zero_rows], axis=0)

    def sub_keys(k_ref, t):
        off = pl.multiple_of(jnp.clip(t, 0, n_sub - 1) * LANES, LANES)
        return k_ref[0, 0, 0, pl.ds(off, LANES), :]

    def with_bias(s, dl, madd=None):
        cols = [lanes(s, r) + tb_ref[r, dl] for r in range(group)]
        if madd is not None:
            cols = [c + madd for c in cols]
        return jnp.concatenate(cols, axis=1)

    lc_ref[...] = _mm(kc_ref[0, 0], qt)
    c0 = pl.multiple_of(jnp.clip((i * (Q_BLOCK // BLK) - CMP_LOOKBACK) // 8 * 8, 0, n_cmp - CMP_BAND), 8)
    row_b = c0 + lax.broadcasted_iota(jnp.int32, (CMP_BAND, Q_BLOCK), 0)
    dist_b = i * Q_BLOCK + lax.broadcasted_iota(jnp.int32, (CMP_BAND, Q_BLOCK), 1) - (row_b * BLK + BLK - 1)
    bkt = _bucket_of(dist_b)
    band = []
    for r in range(group):
        h = g * group + r
        far = rel_ref[NUM_BUCKETS - 1, h]
        bias = jnp.full((CMP_BAND, Q_BLOCK), (rel_ref[0, h] - far) * LOG2E, F32)
        for j in range(1, NUM_BUCKETS - 1):
            bias = jnp.where(bkt == j, (rel_ref[j, h] - far) * LOG2E, bias)
        band.append(jnp.where(bkt == NUM_BUCKETS - 1, 0.0, bias))
    lc_ref[pl.ds(c0, CMP_BAND), :] = lc_ref[pl.ds(c0, CMP_BAND), :] + jnp.concatenate(band, axis=1)
    blk = lax.broadcasted_iota(jnp.int32, (n_cmp, Q_BLOCK), 0)
    qpos_c = i * Q_BLOCK + lax.broadcasted_iota(jnp.int32, (n_cmp, Q_BLOCK), 1)
    ok_c = qpos_c >= blk * BLK + BLK - 1
    ok_f = ok_c.astype(F32)
    imp = jnp.zeros((n_cmp, Q_BLOCK), F32)
    pcs = []
    for r in range(group):
        lg = jnp.where(ok_c, lc_ref[:, r * Q_BLOCK:(r + 1) * Q_BLOCK], NEG_INF)
        p = jnp.exp2(lg - jnp.max(lg, axis=0, keepdims=True)) * ok_f
        pc = p * (1.0 / jnp.maximum(jnp.sum(p, axis=0, keepdims=True), 1e-30))
        imp = imp + pc
        pcs.append(pc.astype(MXU_DTYPE))
    o_c = _mm(vct_ref[0, 0], jnp.concatenate(pcs, axis=1))

    parts = []
    for u in range(WIN_SUBS):
        t = i - (WIN_SUBS - 1) + u
        dw = qpos - (t * LANES + key)
        madd = jnp.where((dw >= 0) & (dw < WINDOW) & (t >= 0), 0.0, NEG_INF)
        parts.append(with_bias(_mm(sub_keys(wk_ref, t), qt_plain), WIN_SUBS - 1 - u, madd))
    sw = jnp.concatenate(parts, axis=0)
    pw = jnp.exp2(sw - jnp.max(sw, axis=0, keepdims=True))
    acc_w = jnp.zeros((HEAD_DIM, nl), F32)
    for u in range(WIN_SUBS):
        off = pl.multiple_of(jnp.maximum(i - (WIN_SUBS - 1) + u, 0) * LANES, LANES)
        acc_w = acc_w + _mm(wvt_ref[0, 0, 0, :, pl.ds(off, LANES)], pw[u * LANES:(u + 1) * LANES].astype(MXU_DTYPE))
    o_w = acc_w * (1.0 / jnp.sum(pw, axis=0, keepdims=True))

    cur = qpos_c // BLK
    forced = (blk == 0) | (blk == cur) | (blk == cur - 1)
    score = jnp.where(forced, FORCED_SCORE, jnp.where(blk <= cur, imp, -1.0))
    sel = jnp.zeros((n_cmp, Q_BLOCK), F32)
    for _ in range(min(N_SEL, n_cmp)):
        mx = jnp.max(score, axis=0, keepdims=True)
        idx = jnp.min(jnp.where(score == mx, blk, n_cmp), axis=0, keepdims=True)
        hit = blk == idx
        sel = jnp.where(hit, 1.0, sel)
        score = jnp.where(hit, -2.0, score)
    sel_ref[...] = (sel - 1.0) * (-NEG_INF)

    def step_logits(step):
        mrows = sel_ref[pl.ds(pl.multiple_of(step * STEP_BLKS, STEP_BLKS), STEP_BLKS), :]
        mrows = jnp.concatenate([mrows] * group, axis=1)
        pad = jnp.zeros((LANES - HEAD_DIM - STEP_BLKS, nl), F32)
        q_aug = jnp.concatenate([qt, jnp.concatenate([mrows, pad], axis=0).astype(MXU_DTYPE)], axis=0)
        return [_mm(sub_keys(sk_ref, step * STEP_SUBS + u), q_aug) for u in range(STEP_SUBS)]

    step_keys = STEP_SUBS * LANES

    def qk_store(step, kind):
        cm = jnp.full((1, nl), NEG_INF, F32)
        for u, s in enumerate(step_logits(step)):
            t = step * STEP_SUBS + u
            if kind == 'near':
                s = with_bias(s, jnp.clip(i - t, 0, N_BIAS_TILES))
            elif kind == 'diag':
                s = with_bias(s, jnp.clip(i - t, 0, N_BIAS_TILES), jnp.where(t * LANES + key <= qpos, 0.0, NEG_INF))
            s_ref[pl.ds(pl.multiple_of(t * LANES, LANES), LANES), :] = s
            cm = jnp.maximum(cm, jnp.max(s, axis=0, keepdims=True))
        return cm

    def values(step):
        off = pl.multiple_of(jnp.maximum(step, 0) * step_keys, step_keys)
        return svt_ref[0, 0, 0, :, pl.ds(off, step_keys)]

    def softmax_step(step, m, l, cm):
        m_new = jnp.maximum(m, cm)
        a = jnp.exp2(m - m_new)
        p = jnp.exp2(s_ref[pl.ds(pl.multiple_of(step * step_keys, step_keys), step_keys), :] - m_new)
        return m_new, a * l + jnp.sum(p, axis=0, keepdims=True), a, p.astype(MXU_DTYPE)

    def pipe_step(j, carry, kind):
        m, l, acc, cm, a_prev = carry
        acc = a_prev * acc + _mm(values(j - 1), p_ref[...])
        m, l, a, p = softmax_step(j, m, l, cm)
        p_ref[...] = p
        return m, l, acc, qk_store(j + 1, kind), a

    last = i // STEP_SUBS
    n_far = jnp.maximum((i - (N_BIAS_TILES + STEP_SUBS - 1)) // STEP_SUBS + 1, 0)
    p_ref[...] = jnp.zeros(p_ref.shape, p_ref.dtype)
    carry = (jnp.full((1, nl), NEG_INF, F32), jnp.zeros((1, nl), F32), jnp.zeros((HEAD_DIM, nl), F32),
             qk_store(0, 'diag'), jnp.ones((1, nl), F32))
    n_a = jnp.maximum(n_far - 1, 0)
    carry = lax.fori_loop(0, n_a, functools.partial(pipe_step, kind='far'), carry)
    carry = lax.fori_loop(n_a, last - 1, functools.partial(pipe_step, kind='near'), carry)
    m, l_s, acc_s, cm, a_prev = lax.fori_loop(jnp.maximum(last - 1, 0), last,
                                              functools.partial(pipe_step, kind='diag'), carry)
    acc_s = a_prev * acc_s + _mm(values(last - 1), p_ref[...])
    _, l_s, a, p = softmax_step(last, m, l_s, cm)
    acc_s = a * acc_s + _mm(values(last), p)
    o_s = acc_s * (1.0 / l_s)

    gt = gate_ref[0]
    for r in range(group):
        o = (gt[3 * r:3 * r + 1, :] * lanes(o_c, r) + gt[3 * r + 1:3 * r + 2, :] * lanes(o_s, r)
             + gt[3 * r + 2:3 * r + 3, :] * lanes(o_w, r))
        o_ref[0, r] = o.astype(o_ref.dtype)


def _attn_prompt(q_t, gates_t, kvc, k_rows, v_t, bias_tiles, rel_bias):
    bsz, nq, seq = q_t.shape
    n_heads = nq // HEAD_DIM
    group = n_heads // N_KV
    n_cmp = kvc.shape[3]
    n_sub = seq // LANES
    assert seq % Q_BLOCK == 0 and Q_BLOCK == LANES and n_cmp * BLK == seq and n_sub % STEP_SUBS == 0
    assert WIN_SUBS - 1 <= N_BIAS_TILES and n_cmp >= CMP_BAND and n_cmp % 8 == 0
    kc = kvc[:, 0].astype(MXU_DTYPE)
    vct = jnp.swapaxes(kvc[:, 1], -1, -2).astype(MXU_DTYPE)
    q4 = q_t.reshape(bsz, n_heads, HEAD_DIM, seq)
    v4 = v_t.reshape(bsz, 2, N_KV, HEAD_DIM, seq)
    k_spec = lambda s: pl.BlockSpec((1, 1, 1, seq, LANES), lambda b, g, i: (b, s, g, 0, 0))
    v_spec = lambda s: pl.BlockSpec((1, 1, 1, HEAD_DIM, seq), lambda b, g, i: (b, s, g, 0, 0))
    o = pl.pallas_call(
        functools.partial(_attn_prompt_kernel, group=group, n_cmp=n_cmp, n_sub=n_sub),
        grid=(bsz, N_KV, seq // Q_BLOCK),
        in_specs=[pl.BlockSpec(memory_space=pltpu.SMEM),
                  pl.BlockSpec((1, group, HEAD_DIM, Q_BLOCK), lambda b, g, i: (b, g, 0, i)),
                  pl.BlockSpec((1, GATE_ROWS, Q_BLOCK), lambda b, g, i: (b, g, i)),
                  pl.BlockSpec((1, 1, n_cmp, HEAD_DIM), lambda b, g, i: (b, g, 0, 0)),
                  pl.BlockSpec((1, 1, HEAD_DIM, n_cmp), lambda b, g, i: (b, g, 0, 0)),
                  k_spec(0), v_spec(0), k_spec(1), v_spec(1),
                  pl.BlockSpec((group, N_BIAS_TILES + 1, LANES, Q_BLOCK), lambda b, g, i: (g, 0, 0, 0))],
        out_specs=pl.BlockSpec((1, group, HEAD_DIM, Q_BLOCK), lambda b, g, i: (b, g, 0, i)),
        out_shape=jax.ShapeDtypeStruct((bsz, n_heads, HEAD_DIM, seq), MXU_DTYPE),
        scratch_shapes=[pltpu.VMEM((n_cmp, Q_BLOCK), F32), pltpu.VMEM((n_cmp, group * Q_BLOCK), F32),
                        pltpu.VMEM((seq, group * Q_BLOCK), F32),
                        pltpu.VMEM((STEP_SUBS * LANES, group * Q_BLOCK), MXU_DTYPE)],
        compiler_params=_params(("parallel", "parallel", "arbitrary")),
        name="attn_prompt",
    )(rel_bias, q4, gates_t, kc, vct, k_rows, v4, k_rows, v4, bias_tiles)
    return o.reshape(bsz, nq, seq)


SLC_PAGES = 16
NEW_PAD = 16


def _bias_rows(trow_ref, dist):
    bkt = _bucket_of(dist)
    bias = jnp.broadcast_to(trow_ref[0:1, :], dist.shape)
    for j in range(1, NUM_BUCKETS):
        bias = jnp.where(bkt == j, trow_ref[j:j + 1, :], bias)
    return bias


def _attn_sample_kernel(pt_ref, *refs, n_pages, past_len, n_new, n_cmp, n_blk):
    del pt_ref
    page_refs = refs[:n_pages]
    (qbd_ref, gate_ref, trow_ref, kc_ref, vct_ref, knew_ref, vnewt_ref, wkt_ref, wvt_ref, wknew_ref, wvnewt_ref,
     o_ref, sel_ref, oc_ref, m_ref, l_ref, acc_ref, s_ref) = refs[n_pages:]
    c = pl.program_id(1)
    qbd = qbd_ref[0]
    gd = qbd.shape[0]
    bpp = PAGE_SIZE // BLK
    npb = past_len // BLK

    def qpos_like(shape):
        return past_len + lax.broadcasted_iota(jnp.int32, shape, 1) % n_new

    @pl.when(c == 0)
    def _():
        blk = lax.broadcasted_iota(jnp.int32, (n_cmp, LANES), 0)
        dist = qpos_like((n_cmp, LANES)) - (blk * BLK + BLK - 1)
        ok = dist >= 0
        lg = jnp.where(ok, _mm(kc_ref[0], qbd) + _bias_rows(trow_ref, dist), NEG_INF)
        p = jnp.exp(lg - jnp.max(lg, axis=0, keepdims=True)) * ok.astype(F32)
        pc = p / jnp.maximum(jnp.sum(p, axis=0, keepdims=True), 1e-30)
        oc_ref[...] = _mm(vct_ref[0], pc.astype(MXU_DTYPE))
        quarter = LANES // 4
        imp = pc + pltpu.roll(pc, quarter, 1) + pltpu.roll(pc, 2 * quarter, 1) + pltpu.roll(pc, 3 * quarter, 1)
        nbp = sel_ref.shape[0]
        imp = jnp.concatenate([imp, jnp.zeros((nbp - n_cmp, LANES), F32)], axis=0)
        blk = lax.broadcasted_iota(jnp.int32, (nbp, LANES), 0)
        cur = qpos_like((nbp, LANES)) // BLK
        forced = (blk == 0) | (blk == cur) | (blk == cur - 1)
        score = jnp.where(forced, FORCED_SCORE, jnp.where(blk <= cur, imp, -1.0))
        score = jnp.where(blk >= n_blk, -2.0, score)
        sel = jnp.zeros((nbp, LANES), F32)
        for _ in range(min(N_SEL, n_blk)):
            mx = jnp.max(score, axis=0, keepdims=True)
            idx = jnp.min(jnp.where(score == mx, blk, nbp), axis=0, keepdims=True)
            hit = blk == idx
            sel = jnp.where(hit, 1.0, sel)
            score = jnp.where(hit, -3.0, score)
        sel_ref[...] = sel
        m_ref[...] = jnp.full(m_ref.shape, NEG_INF, F32)
        l_ref[...] = jnp.zeros(l_ref.shape, F32)
        acc_ref[...] = jnp.zeros(acc_ref.shape, F32)

    tok = lax.broadcasted_iota(jnp.int32, (PAGE_SIZE, LANES), 0)
    qpos_p = qpos_like((PAGE_SIZE, LANES))

    def page_logits(near):
        for j in range(n_pages):
            pg = c * n_pages + j
            st = _tn(page_refs[j][0, 0].reshape(gd, PAGE_SIZE).astype(MXU_DTYPE), qbd)
            if near:
                st = st + _bias_rows(trow_ref, qpos_p - (pg * PAGE_SIZE + tok))
            else:
                st = st + trow_ref[NUM_BUCKETS - 1:NUM_BUCKETS, :]
            chosen = jnp.broadcast_to(sel_ref[pl.ds(pg * bpp + bpp - 1, 1), :], (PAGE_SIZE, LANES))
            for t in reversed(range(bpp - 1)):
                chosen = jnp.where(tok < (t + 1) * BLK, sel_ref[pl.ds(pg * bpp + t, 1), :], chosen)
            s_ref[j * PAGE_SIZE:(j + 1) * PAGE_SIZE, :] = jnp.where(chosen > 0.5, st, NEG_INF)

    all_far = past_len - ((c + 1) * n_pages * PAGE_SIZE - 1) >= FAR_DIST
    pl.when(all_far)(functools.partial(page_logits, False))
    pl.when(jnp.logical_not(all_far))(functools.partial(page_logits, True))
    s_all = s_ref[...]
    m_old = m_ref[...]
    m_new = jnp.maximum(m_old, jnp.max(s_all, axis=0, keepdims=True))
    a = jnp.exp(m_old - m_new)
    p = jnp.exp(s_all - m_new)
    acc = acc_ref[...] * a
    for j in range(n_pages):
        vt = page_refs[j][0, 1].reshape(gd, PAGE_SIZE).astype(MXU_DTYPE)
        acc = acc + _mm(vt, p[j * PAGE_SIZE:(j + 1) * PAGE_SIZE].astype(MXU_DTYPE))
    acc_ref[...] = acc
    l_ref[...] = a * l_ref[...] + jnp.sum(p, axis=0, keepdims=True)
    m_ref[...] = m_new

    @pl.when(c == pl.num_programs(1) - 1)
    def _():
        tok_n = lax.broadcasted_iota(jnp.int32, (NEW_PAD, LANES), 0)
        dist_n = qpos_like((NEW_PAD, LANES)) - (past_len + tok_n)
        live_n = (tok_n < n_new) & (dist_n >= 0)
        bias_n = _bias_rows(trow_ref, dist_n)
        chosen_n = sel_ref[pl.ds(npb, 1), :] > 0.5
        s_n = jnp.where(live_n & chosen_n, _mm(knew_ref[0], qbd) + bias_n, NEG_INF)
        m_old = m_ref[...]
        m_fin = jnp.maximum(m_old, jnp.max(s_n, axis=0, keepdims=True))
        a = jnp.exp(m_old - m_fin)
        p_n = jnp.exp(s_n - m_fin)
        acc_s = acc_ref[...] * a + _mm(vnewt_ref[0], p_n.astype(MXU_DTYPE))
        o_s = acc_s / (a * l_ref[...] + jnp.sum(p_n, axis=0, keepdims=True))
        wb = wkt_ref.shape[-1]
        tok_w = lax.broadcasted_iota(jnp.int32, (wb, LANES), 0)
        kwpos = past_len - wb + tok_w
        dw = qpos_like((wb, LANES)) - kwpos
        ok_w = (dw >= 0) & (dw < WINDOW) & (kwpos >= 0)
        s_w = _tn(wkt_ref[0, 0].reshape(gd, wb).astype(MXU_DTYPE), qbd) + _bias_rows(trow_ref, dw)
        s_w = jnp.where(ok_w, s_w, NEG_INF)
        s_wn = jnp.where(live_n & (dist_n < WINDOW), _mm(wknew_ref[0], qbd) + bias_n, NEG_INF)
        m_w = jnp.maximum(jnp.max(s_w, axis=0, keepdims=True), jnp.max(s_wn, axis=0, keepdims=True))
        p_w = jnp.exp(s_w - m_w)
        p_wn = jnp.exp(s_wn - m_w)
        o_w = (_mm(wvt_ref[0, 0].reshape(gd, wb).astype(MXU_DTYPE), p_w.astype(MXU_DTYPE))
               + _mm(wvnewt_ref[0], p_wn.astype(MXU_DTYPE)))
        o_w = o_w / (jnp.sum(p_w, axis=0, keepdims=True) + jnp.sum(p_wn, axis=0, keepdims=True))
        gt = gate_ref[0]
        tot = gt[0:1, :] * oc_ref[...] + gt[1:2, :] * o_s + gt[2:3, :] * o_w
        lane_g = (lax.broadcasted_iota(jnp.int32, (HEAD_DIM, LANES), 1) // n_new) % N_KV
        out = jnp.zeros((HEAD_DIM, LANES), F32)
        for g in range(N_KV):
            out = out + jnp.where(lane_g == g, tot[g * HEAD_DIM:(g + 1) * HEAD_DIM, :], 0.0)
        o_ref[0] = out


def _attn_sample(q, gates, kvc, kv_new_t, slc_pages, page_table, win_t, rel_bias):
    bsz, n_new, nq = q.shape
    n_heads = nq // HEAD_DIM
    group = n_heads // N_KV
    pps = page_table.shape[1]
    past_len = pps * PAGE_SIZE
    n_cmp = kvc.shape[3]
    n_blk = past_len // BLK + -(-n_new // BLK)
    nbp = -(-n_blk // 8) * 8
    gd = N_KV * HEAD_DIM
    wb = win_t.shape[-1]
    assert n_heads * n_new == LANES and group == 4 and n_new <= min(BLK, NEW_PAD) and pps % SLC_PAGES == 0
    assert n_cmp == past_len // BLK and n_cmp % 8 == 0
    q5 = q.reshape(bsz, n_new, N_KV, group, HEAD_DIM)
    qbd = jnp.einsum('hg,bqgrd->bhdrgq', jnp.eye(N_KV, dtype=q.dtype), q5).reshape(bsz, gd, LANES).astype(MXU_DTYPE)
    g5 = gates[:, :, :3 * n_heads].reshape(bsz, n_new, N_KV, group, 3)
    gate_t = jnp.pad(g5.transpose(0, 4, 3, 2, 1).reshape(bsz, 3, LANES), ((0, 0), (0, 5), (0, 0)))
    trow = jnp.broadcast_to(rel_bias.reshape(NUM_BUCKETS, N_KV, group).transpose(0, 2, 1)[..., None],
                            (NUM_BUCKETS, group, N_KV, n_new)).reshape(NUM_BUCKETS, LANES)
    kc = kvc[:, 0].transpose(0, 2, 1, 3).reshape(bsz, n_cmp, gd).astype(MXU_DTYPE)
    vct = kvc[:, 1].transpose(0, 1, 3, 2).reshape(bsz, gd, n_cmp).astype(MXU_DTYPE)
    new = kv_new_t.reshape(6, gd, bsz, n_new).transpose(0, 2, 1, 3)
    new = jnp.pad(new, ((0, 0), (0, 0), (0, 0), (0, NEW_PAD - n_new))).astype(MXU_DTYPE)
    k_rows = lambda s: jnp.swapaxes(new[s], 1, 2)
    page_spec = lambda j: pl.BlockSpec((1, 2, N_KV, HEAD_DIM, PAGE_SIZE),
                                       lambda b, c, pt: (pt[b, c * SLC_PAGES + j], 0, 0, 0, 0))
    per_b = lambda a: pl.BlockSpec((1,) + a.shape[1:], lambda b, c, pt: (b,) + (0,) * (a.ndim - 1))
    win_spec = lambda s: pl.BlockSpec((1, 1, N_KV, HEAD_DIM, wb), lambda b, c, pt: (b, s, 0, 0, 0))
    smalls = (qbd, gate_t)
    o_t = pl.pallas_call(
        functools.partial(_attn_sample_kernel, n_pages=SLC_PAGES, past_len=past_len, n_new=n_new, n_cmp=n_cmp,
                          n_blk=n_blk),
        grid_spec=pltpu.PrefetchScalarGridSpec(
            num_scalar_prefetch=1,
            grid=(bsz, pps // SLC_PAGES),
            in_specs=[page_spec(j) for j in range(SLC_PAGES)]
            + [per_b(qbd), per_b(gate_t), pl.BlockSpec(trow.shape, lambda b, c, pt: (0, 0)),
               per_b(kc), per_b(vct), per_b(k_rows(2)), per_b(new[3]), win_spec(0), win_spec(1),
               per_b(k_rows(4)), per_b(new[5])],
            out_specs=pl.BlockSpec((1, HEAD_DIM, LANES), lambda b, c, pt: (b, 0, 0)),
            scratch_shapes=[pltpu.VMEM((nbp, LANES), F32), pltpu.VMEM((gd, LANES), F32), pltpu.VMEM((1, LANES), F32),
                            pltpu.VMEM((1, LANES), F32), pltpu.VMEM((gd, LANES), F32),
                            pltpu.VMEM((SLC_PAGES * PAGE_SIZE, LANES), F32)]),
        out_shape=jax.ShapeDtypeStruct((bsz, HEAD_DIM, LANES), F32),
        compiler_params=_params(("parallel", "arbitrary")),
        name="attn_sample",
    )(page_table, *([slc_pages] * SLC_PAGES), qbd, gate_t, trow, kc, vct, k_rows(2), new[3], win_t, win_t,
      k_rows(4), new[5])
    del smalls
    return o_t.reshape(bsz, HEAD_DIM, group, N_KV, n_new).transpose(0, 4, 3, 2, 1).reshape(bsz, n_new, nq)


def _tokens_last(x):
    nd = x.ndim
    return jnp.transpose(x, tuple(range(nd - 4)) + (nd - 3, nd - 2, nd - 1, nd - 4))


def _tokens_first(x):
    nd = x.ndim
    return jnp.transpose(x, tuple(range(nd - 4)) + (nd - 1, nd - 4, nd - 3, nd - 2))


def kernel(x_prompt, x_sample, cache_kv_cmp, cache_kv_slc, cache_win, state_conv, page_table, ln_g, ln_b, sg_w_in, sg_b_in, sg_ln_g, sg_ln_b, sg_w_s, sg_b_s, sg_w_out, sg_b_out, ffn_w_up, ffn_b_up, ffn_w_dw, ffn_b_dw, ffn_w_down, ffn_b_down, kv_w, cmp_pe, cmp_w1, cmp_b1, cmp_w2, cmp_b2, nsa_w_qg, nsa_b_qg, nsa_w_o, nsa_b_o, rel_bias):
    depth = ln_g.shape[0]
    n_a = sg_w_in.shape[0]
    assert depth == DEPTH
    bsz, seq, d_model = x_prompt.shape
    dbsz, n_new, _ = x_sample.shape
    d_ff = ffn_w_down.shape[1]
    kvd = (2, N_KV, HEAD_DIM)
    gd = N_KV * HEAD_DIM
    cmp_w = (cmp_pe, cmp_w1, cmp_b1, cmp_w2, cmp_b2)

    def trunk(x, seq_len, conv_state, attend_layer):
        n_seq = x.shape[0] // seq_len
        new_conv, sg_v, kv_t, ctx = [], [], None, None
        for l in range(depth):
            if l < n_a:
                x, v = _sgu_layer(x, seq_len, sg_w_in[l], sg_b_in[l], sg_ln_g[l], sg_ln_b[l], sg_w_s[l], sg_b_s[l],
                                  sg_w_out[l], sg_b_out[l], ln_g[l, 0], ln_b[l, 0])
                sg_v.append(v.reshape(n_seq, seq_len, -1))
            else:
                i = l - n_a
                x = attend_layer(x, ctx, nsa_w_qg[i], nsa_b_qg[i], nsa_w_o[i], nsa_b_o[i], ln_g[l, 0], ln_b[l, 0])
            x, cs = _ffn_layer(x, seq_len, conv_state[l], ffn_w_up[l], ffn_b_up[l], ffn_w_dw[l], ffn_b_dw[l],
                               ffn_w_down[l], ffn_b_down[l], ln_g[l, 1], ln_b[l, 1])
            new_conv.append(cs)
            if l == n_a - 1:
                kv_t, ctx = attend_layer.prepare(x)
        return x, jnp.stack(new_conv), (jnp.stack(sg_v) if sg_v else None), kv_t

    bias_tiles = _bias_tiles(rel_bias)

    def prompt_prepare(x):
        kv_t, v_t, k_rows = _kv_proj(x, bsz, kv_w)
        cmp_t = kv_t[:, :2 * gd].reshape(bsz, 2, N_KV, HEAD_DIM, seq // PAGE_SIZE, PAGE_SIZE)
        pages = cmp_t.transpose(0, 4, 1, 2, 3, 5).reshape(bsz * (seq // PAGE_SIZE), 2, N_KV, HEAD_DIM, PAGE_SIZE)
        ident = jnp.arange(bsz * (seq // PAGE_SIZE), dtype=jnp.int32).reshape(bsz, seq // PAGE_SIZE)
        kvc = _compress_pages(pages, ident, *cmp_w)
        return kv_t, (kvc, k_rows, v_t)

    def prompt_attend(x, ctx, w_qg, b_qg, w_o, b_o, g, b):
        kvc, k_rows, v_t = ctx
        q_t, gates_t = _qg_proj_t(x, bsz, w_qg, b_qg)
        o_t = _attn_prompt(q_t, gates_t, kvc, k_rows, v_t, bias_tiles, rel_bias)
        return _oproj_t(o_t, x, w_o, b_o, g, b)

    prompt_attend.prepare = prompt_prepare
    zeros_state = jnp.zeros((depth, bsz, CONV_W - 1, d_ff), x_prompt.dtype)
    y_p, conv_p, _, kv_p = trunk(x_prompt.reshape(bsz * seq, d_model), seq, zeros_state, prompt_attend)
    kv_p = kv_p.reshape(bsz, 3, 2, N_KV, HEAD_DIM, seq)

    cmp_pages = _tokens_last(cache_kv_cmp)
    slc_pages = _tokens_last(cache_kv_slc)
    win_t = _tokens_last(cache_win)

    def sample_prepare(x):
        kv_t, _, _ = _kv_proj(x, 1, kv_w)
        kvc = _compress_pages(cmp_pages, page_table, *cmp_w)
        return kv_t[0], (kvc, kv_t[0])

    def sample_attend(x, ctx, w_qg, b_qg, w_o, b_o, g, b):
        kvc, kv_new_t = ctx
        q, gates = _qg_proj_flat(x, w_qg, b_qg)
        o = _attn_sample(q.reshape(dbsz, n_new, -1), gates.reshape(dbsz, n_new, -1), kvc, kv_new_t, slc_pages,
                         page_table, win_t, rel_bias)
        return _oproj_flat(o.reshape(dbsz * n_new, -1), x, w_o, b_o, g, b)

    sample_attend.prepare = sample_prepare
    assert n_new < BLK
    y_s, conv_s, sgv_s, kv_s = trunk(x_sample.reshape(dbsz * n_new, d_model), n_new, state_conv, sample_attend)
    kv_s = kv_s.reshape(3, 2, N_KV, HEAD_DIM, dbsz, n_new)

    win_len = min(WINDOW, seq)
    out_p = [_tokens_first(kv_p[:, s]) for s in range(3)]
    out_s = [jnp.transpose(kv_s[s], (3, 4, 0, 1, 2)) for s in range(3)]
    return (y_p.reshape(bsz, seq, d_model), y_s.reshape(dbsz, n_new, d_model),
            out_p[0], out_p[1], out_p[2][:, seq - win_len:], conv_p,
            out_s[0], out_s[1], out_s[2], conv_s, sgv_s)
```

```python
import functools
import math

import numpy as np
import jax
import jax.numpy as jnp
from jax import lax
from jax.experimental import pallas as pl
from jax.experimental.pallas import tpu as pltpu

N_SG_GROUPS = 8
CHUNK = 128
CONV_W = 3
HEAD_DIM = 64
N_KV = 4
BLK = 64
N_SEL = 16
WINDOW = 512
NUM_BUCKETS = 32
REL_MAX_DIST = 1024
Q_BLOCK = 128
PAGE_SIZE = 128
DEPTH = 2
ALPHA = (2 * DEPTH) ** 0.25
LN_EPS = 1e-5
NEG_INF = -1e30
FORCED_SCORE = 1e9
LOG2E = math.log2(math.e)

MXU_DTYPE = jnp.bfloat16
F32 = jnp.float32
LANES = 128
ROW_TILE = 256
VMEM_LIMIT = 56 * 1024 * 1024


def _bucket_thresholds():
    d = np.arange(0, 4 * REL_MAX_DIST)
    max_exact = NUM_BUCKETS // 2
    nf = np.maximum(d, 1).astype(np.float32)
    large = max_exact + (np.log(nf / np.float32(max_exact)) / np.float32(math.log(REL_MAX_DIST / max_exact))
                         * np.float32(NUM_BUCKETS - max_exact)).astype(np.int32)
    b = np.where(d < max_exact, d, np.minimum(large, NUM_BUCKETS - 1))
    assert np.all(np.diff(b) >= 0) and b[-1] == NUM_BUCKETS - 1
    return [int(np.nonzero(b >= j)[0][0]) for j in range(NUM_BUCKETS)]


BUCKET_THR = _bucket_thresholds()
FAR_DIST = BUCKET_THR[-1]
N_BIAS_TILES = -(-(FAR_DIST + LANES) // LANES)


def _gelu(x):
    return 0.5 * x * (1.0 + lax.erf(x * (1.0 / math.sqrt(2.0))))


def _layer_norm(x, g, b):
    mu = jnp.mean(x, axis=-1, keepdims=True)
    xc = x - mu
    var = jnp.mean(xc * xc, axis=-1, keepdims=True)
    return xc * lax.rsqrt(var + LN_EPS) * g + b


def _mm(a, b):
    return jnp.dot(a, b, preferred_element_type=F32)


def _mm_nt(a, b):
    return lax.dot_general(a, b, (((1,), (1,)), ((), ())), preferred_element_type=F32)


def _tn(a, b):
    return lax.dot_general(a, b, (((0,), (0,)), ((), ())), preferred_element_type=F32)


def _const_spec(shape):
    nd = len(shape)
    return pl.BlockSpec(shape, lambda *_: (0,) * nd, pipeline_mode=pl.Buffered(1))


def _params(sem):
    return pltpu.CompilerParams(dimension_semantics=sem, vmem_limit_bytes=VMEM_LIMIT)


def _bucket_of(dist):
    b = jnp.zeros(dist.shape, jnp.int32)
    for j in range(1, NUM_BUCKETS):
        b = b + (dist >= BUCKET_THR[j]).astype(jnp.int32)
    return b


def _sgu_kernel(x_ref, win_ref, bin_ref, gv_ref, bv_ref, wg_ref, bs_ref, wout_ref, bout_ref, lng_ref, lnb_ref,
                o_ref, v_ref, *, d_sg):
    x = x_ref[...]
    xb = x.astype(MXU_DTYPE)
    v = _gelu(_mm(xb, win_ref[:, d_sg:]) + bin_ref[:, d_sg:])
    vn = _layer_norm(v, gv_ref[...], bv_ref[...])
    v_ref[...] = vn
    cg = d_sg // N_SG_GROUPS
    acc = jnp.zeros(o_ref.shape, F32)
    for g in range(N_SG_GROUPS):
        lo, hi = g * cg, (g + 1) * cg
        gate = _mm(wg_ref[g], vn[:, lo:hi].astype(MXU_DTYPE)) + bs_ref[g]
        u = _gelu(_mm(xb, win_ref[:, lo:hi]) + bin_ref[:, lo:hi])
        acc = acc + _mm((u * gate).astype(MXU_DTYPE), wout_ref[lo:hi, :])
    y = acc + bout_ref[...]
    o_ref[...] = _layer_norm(ALPHA * x + y, lng_ref[...], lnb_ref[...])


def _sgu_layer(x, seq_len, w_in, b_in, g_v, b_v, w_s, b_s, w_out, b_out, ln_g, ln_b):
    t, d = x.shape
    d_sg = w_out.shape[0]
    cg = d_sg // N_SG_GROUPS
    cl = min(CHUNK, seq_len)
    assert seq_len % cl == 0 and ROW_TILE % cl == 0 and t % ROW_TILE == 0
    tm = ROW_TILE
    w_tri = jnp.tril(w_s[:, :cl, :cl])
    eye = jnp.eye(tm // cl, dtype=w_s.dtype)
    wg = jnp.einsum('ab,gts->gatbs', eye, w_tri).reshape(N_SG_GROUPS, tm, tm).astype(MXU_DTYPE)
    bs = jnp.broadcast_to(jnp.tile(b_s[:, :cl], (1, tm // cl))[:, :, None], (N_SG_GROUPS, tm, cg))
    row = lambda a: a.reshape(1, -1)
    out, v = pl.pallas_call(
        functools.partial(_sgu_kernel, d_sg=d_sg),
        grid=(t // tm,),
        in_specs=[pl.BlockSpec((tm, d), lambda i: (i, 0)),
                  _const_spec((d, 2 * d_sg)), _const_spec((1, 2 * d_sg)), _const_spec((1, d_sg)), _const_spec((1, d_sg)),
                  _const_spec((N_SG_GROUPS, tm, tm)), _const_spec((N_SG_GROUPS, tm, cg)),
                  _const_spec((d_sg, d)), _const_spec((1, d)), _const_spec((1, d)), _const_spec((1, d))],
        out_specs=[pl.BlockSpec((tm, d), lambda i: (i, 0)), pl.BlockSpec((tm, d_sg), lambda i: (i, 0))],
        out_shape=[jax.ShapeDtypeStruct((t, d), F32), jax.ShapeDtypeStruct((t, d_sg), F32)],
        compiler_params=_params(("parallel",)),
        name="sgu_layer",
    )(x, w_in.astype(MXU_DTYPE), row(b_in), row(g_v), row(b_v), wg, bs, w_out.astype(MXU_DTYPE), row(b_out),
      row(ln_g), row(ln_b))
    return out, v


def _ffn_tail(x, a, gt, s1, s2, wdw_ref, bdw_ref, wdown_ref, bdown_ref, lng_ref, lnb_ref, o_ref):
    c = bdw_ref[...] + s2 * wdw_ref[0:1, :] + s1 * wdw_ref[1:2, :] + a * wdw_ref[2:3, :]
    h = (_gelu(c) * gt).astype(MXU_DTYPE)
    y = _mm(h, wdown_ref[...]) + bdown_ref[...]
    o_ref[...] = _layer_norm(ALPHA * x + y, lng_ref[...], lnb_ref[...])


def _ffn_long_kernel(x_ref, xh_ref, st_ref, wup_ref, bup_ref, wdw_ref, bdw_ref, wdown_ref, bdown_ref, lng_ref, lnb_ref,
                     o_ref, conv_ref, *, seq_len, d_ff):
    tm = x_ref.shape[0]
    x = x_ref[...]
    xb = x.astype(MXU_DTYPE)
    a = _mm(xb, wup_ref[:, :d_ff]) + bup_ref[:, :d_ff]
    gt = _mm(xb, wup_ref[:, d_ff:]) + bup_ref[:, d_ff:]
    ah = _mm(xh_ref[...].astype(MXU_DTYPE), wup_ref[:, :d_ff]) + bup_ref[:, :d_ff]
    at_start = (pl.program_id(0) * tm) % seq_len == 0
    p1 = jnp.where(at_start, st_ref[0, 1:2, :], ah[7:8, :])
    p0 = jnp.where(at_start, st_ref[0, 0:1, :], ah[6:7, :])
    r = lax.broadcasted_iota(jnp.int32, a.shape, 0)
    s1 = jnp.where(r == 0, p1, pltpu.roll(a, 1, 0))
    s2 = jnp.where(r == 0, p0, jnp.where(r == 1, p1, pltpu.roll(a, 2, 0)))
    conv_ref[0] = a[tm - (CONV_W - 1):, :]
    _ffn_tail(x, a, gt, s1, s2, wdw_ref, bdw_ref, wdown_ref, bdown_ref, lng_ref, lnb_ref, o_ref)


def _ffn_short_kernel(x_ref, p0_ref, p1_ref, wup_ref, bup_ref, wdw_ref, bdw_ref, wdown_ref, bdown_ref, lng_ref, lnb_ref,
                      o_ref, a_ref, *, seq_len, d_ff):
    x = x_ref[...]
    xb = x.astype(MXU_DTYPE)
    a = _mm(xb, wup_ref[:, :d_ff]) + bup_ref[:, :d_ff]
    gt = _mm(xb, wup_ref[:, d_ff:]) + bup_ref[:, d_ff:]
    pos = lax.broadcasted_iota(jnp.int32, a.shape, 0) % seq_len
    s1 = jnp.where(pos == 0, p1_ref[...], pltpu.roll(a, 1, 0))
    s2 = jnp.where(pos == 0, p0_ref[...], jnp.where(pos == 1, p1_ref[...], pltpu.roll(a, 2, 0)))
    a_ref[...] = a
    _ffn_tail(x, a, gt, s1, s2, wdw_ref, bdw_ref, wdown_ref, bdown_ref, lng_ref, lnb_ref, o_ref)


def _ffn_layer(x, seq_len, conv_state, w_up, b_up, w_dw, b_dw, w_down, b_down, ln_g, ln_b):
    t, d = x.shape
    d_ff = w_down.shape[0]
    n_seq = t // seq_len
    tm = ROW_TILE
    assert seq_len >= CONV_W - 1 and t % tm == 0
    row = lambda a: a.reshape(1, -1)
    weights = (w_up.astype(MXU_DTYPE), row(b_up), w_dw, row(b_dw), w_down.astype(MXU_DTYPE), row(b_down), row(ln_g), row(ln_b))
    w_specs = [_const_spec((d, 2 * d_ff)), _const_spec((1, 2 * d_ff)), _const_spec((CONV_W, d_ff)), _const_spec((1, d_ff)),
               _const_spec((d_ff, d)), _const_spec((1, d)), _const_spec((1, d)), _const_spec((1, d))]
    if seq_len % tm == 0:
        tiles_per_seq = seq_len // tm
        out, conv = pl.pallas_call(
            functools.partial(_ffn_long_kernel, seq_len=seq_len, d_ff=d_ff),
            grid=(t // tm,),
            in_specs=[pl.BlockSpec((tm, d), lambda i: (i, 0)),
                      pl.BlockSpec((8, d), lambda i: (jnp.maximum(i * (tm // 8) - 1, 0), 0)),
                      pl.BlockSpec((1, CONV_W - 1, d_ff), lambda i: (i // tiles_per_seq, 0, 0))] + w_specs,
            out_specs=[pl.BlockSpec((tm, d), lambda i: (i, 0)),
                       pl.BlockSpec((1, CONV_W - 1, d_ff), lambda i: (i // tiles_per_seq, 0, 0))],
            out_shape=[jax.ShapeDtypeStruct((t, d), F32), jax.ShapeDtypeStruct((n_seq, CONV_W - 1, d_ff), F32)],
            compiler_params=_params(("arbitrary",)),
            name="ffn_long",
        )(x, x, conv_state, *weights)
        return out, conv
    assert tm % seq_len == 0
    p0 = jnp.repeat(conv_state[:, 0, :], seq_len, axis=0)
    p1 = jnp.repeat(conv_state[:, 1, :], seq_len, axis=0)
    out, a = pl.pallas_call(
        functools.partial(_ffn_short_kernel, seq_len=seq_len, d_ff=d_ff),
        grid=(t // tm,),
        in_specs=[pl.BlockSpec((tm, d), lambda i: (i, 0)), pl.BlockSpec((tm, d_ff), lambda i: (i, 0)),
                  pl.BlockSpec((tm, d_ff), lambda i: (i, 0))] + w_specs,
        out_specs=[pl.BlockSpec((tm, d), lambda i: (i, 0)), pl.BlockSpec((tm, d_ff), lambda i: (i, 0))],
        out_shape=[jax.ShapeDtypeStruct((t, d), F32), jax.ShapeDtypeStruct((t, d_ff), F32)],
        compiler_params=_params(("parallel",)),
        name="ffn_short",
    )(x, p0, p1, *weights)
    return out, a.reshape(n_seq, seq_len, d_ff)[:, seq_len - (CONV_W - 1):, :]


STEP_SUBS = 4
STEP_BLKS = STEP_SUBS * LANES // BLK


def _kv_kernel(x_ref, wt_ref, wk_ref, kv_ref, vt_ref, k_ref, *, tiles):
    tm = x_ref.shape[0]
    xb = x_ref[...].astype(MXU_DTYPE)
    r = _mm_nt(wt_ref[...], xb)
    kv_ref[0] = r
    gd = N_KV * HEAD_DIM
    pos = (pl.program_id(0) % tiles) * tm + lax.broadcasted_iota(jnp.int32, (tm, LANES), 0)
    lane = lax.broadcasted_iota(jnp.int32, (tm, LANES), 1)
    onehot = jnp.where((pos // BLK) % STEP_BLKS + HEAD_DIM == lane, 1.0, 0.0)
    for s in range(2):
        vt_ref[0, s] = r[(2 * s + 3) * gd:(2 * s + 4) * gd, :].astype(MXU_DTYPE)
        for g in range(N_KV):
            k_ref[0, s, g] = (_mm(xb, wk_ref[s, g]) + onehot).astype(MXU_DTYPE)


def _kv_proj(x, n_seq, kv_w):
    t, d = x.shape
    seq_len = t // n_seq
    n_out = kv_w.shape[1]
    gd = N_KV * HEAD_DIM
    tm = ROW_TILE
    assert seq_len % tm == 0 and n_out == 6 * gd and HEAD_DIM + STEP_BLKS <= LANES
    tiles = seq_len // tm
    wk = kv_w.reshape(d, 3, 2, N_KV, HEAD_DIM)[:, 1:, 0].transpose(1, 2, 0, 3)
    wk = jnp.pad(wk, ((0, 0), (0, 0), (0, 0), (0, LANES - HEAD_DIM))).astype(MXU_DTYPE)
    return pl.pallas_call(
        functools.partial(_kv_kernel, tiles=tiles),
        grid=(t // tm,),
        in_specs=[pl.BlockSpec((tm, d), lambda i: (i, 0)), _const_spec((n_out, d)), _const_spec(wk.shape)],
        out_specs=[pl.BlockSpec((1, n_out, tm), lambda i: (i // tiles, 0, i % tiles)),
                   pl.BlockSpec((1, 2, gd, tm), lambda i: (i // tiles, 0, 0, i % tiles)),
                   pl.BlockSpec((1, 2, N_KV, tm, LANES), lambda i: (i // tiles, 0, 0, i % tiles, 0))],
        out_shape=[jax.ShapeDtypeStruct((n_seq, n_out, seq_len), F32),
                   jax.ShapeDtypeStruct((n_seq, 2, gd, seq_len), MXU_DTYPE),
                   jax.ShapeDtypeStruct((n_seq, 2, N_KV, seq_len, LANES), MXU_DTYPE)],
        compiler_params=_params(("parallel",)),
        name="kv_proj",
    )(x, kv_w.T.astype(MXU_DTYPE), wk)


def _sigmoid(x):
    return 1.0 / (1.0 + jnp.exp(-x))


GATE_ROWS = 16


def _qg_t_kernel(x_ref, wq_ref, bq_ref, wg_ref, bg_ref, q_ref, gate_ref):
    xb = x_ref[...].astype(MXU_DTYPE)
    q_ref[0] = ((_mm_nt(wq_ref[...], xb) + bq_ref[...]) * (HEAD_DIM ** -0.5 * LOG2E)).astype(MXU_DTYPE)
    gate_ref[0] = _sigmoid(_mm_nt(wg_ref[...], xb) + bg_ref[...])


def _qg_proj_t(x, n_seq, w_qg, b_qg):
    t, d = x.shape
    seq_len = t // n_seq
    n_heads = (w_qg.shape[1] // (HEAD_DIM + 3))
    group = n_heads // N_KV
    nq = n_heads * HEAD_DIM
    tm = ROW_TILE
    assert seq_len % tm == 0 and 3 * group <= GATE_ROWS
    tiles = seq_len // tm
    pad = GATE_ROWS - 3 * group
    wg = jnp.pad(w_qg[:, nq:].T.reshape(N_KV, 3 * group, d), ((0, 0), (0, pad), (0, 0))).reshape(N_KV * GATE_ROWS, d)
    bg = jnp.pad(b_qg[nq:].reshape(N_KV, 3 * group), ((0, 0), (0, pad))).reshape(N_KV * GATE_ROWS, 1)
    bg = jnp.broadcast_to(bg, (N_KV * GATE_ROWS, tm))
    bq = jnp.broadcast_to(b_qg[:nq].reshape(nq, 1), (nq, tm))
    return pl.pallas_call(
        _qg_t_kernel,
        grid=(t // tm,),
        in_specs=[pl.BlockSpec((tm, d), lambda i: (i, 0)), _const_spec((nq, d)), _const_spec((nq, tm)),
                  _const_spec(wg.shape), _const_spec(bg.shape)],
        out_specs=[pl.BlockSpec((1, nq, tm), lambda i: (i // tiles, 0, i % tiles)),
                   pl.BlockSpec((1, N_KV * GATE_ROWS, tm), lambda i: (i // tiles, 0, i % tiles))],
        out_shape=[jax.ShapeDtypeStruct((n_seq, nq, seq_len), MXU_DTYPE),
                   jax.ShapeDtypeStruct((n_seq, N_KV * GATE_ROWS, seq_len), F32)],
        compiler_params=_params(("parallel",)),
        name="qg_proj_t",
    )(x, w_qg[:, :nq].T.astype(MXU_DTYPE), bq, wg.astype(MXU_DTYPE), bg)


def _qg_flat_kernel(x_ref, wq_ref, bq_ref, wg_ref, bg_ref, q_ref, gate_ref):
    xb = x_ref[...].astype(MXU_DTYPE)
    q_ref[...] = (_mm(xb, wq_ref[...]) + bq_ref[...]) * (HEAD_DIM ** -0.5)
    gate_ref[...] = _sigmoid(_mm(xb, wg_ref[...]) + bg_ref[...])


def _qg_proj_flat(x, w_qg, b_qg):
    t, d = x.shape
    n_gate = w_qg.shape[1] // (HEAD_DIM + 3) * 3
    nq = w_qg.shape[1] - n_gate
    tm = ROW_TILE
    assert t % tm == 0 and n_gate <= LANES
    wg = jnp.pad(w_qg[:, nq:], ((0, 0), (0, LANES - n_gate))).astype(MXU_DTYPE)
    bg = jnp.pad(b_qg[nq:], (0, LANES - n_gate)).reshape(1, LANES)
    return pl.pallas_call(
        _qg_flat_kernel,
        grid=(t // tm,),
        in_specs=[pl.BlockSpec((tm, d), lambda i: (i, 0)), _const_spec((d, nq)), _const_spec((1, nq)),
                  _const_spec((d, LANES)), _const_spec((1, LANES))],
        out_specs=[pl.BlockSpec((tm, nq), lambda i: (i, 0)), pl.BlockSpec((tm, LANES), lambda i: (i, 0))],
        out_shape=[jax.ShapeDtypeStruct((t, nq), F32), jax.ShapeDtypeStruct((t, LANES), F32)],
        compiler_params=_params(("parallel",)),
        name="qg_proj_flat",
    )(x, w_qg[:, :nq].astype(MXU_DTYPE), b_qg[:nq].reshape(1, nq), wg, bg)


def _oproj_t_kernel(o_ref, x_ref, wo_ref, bo_ref, lng_ref, lnb_ref, y_ref):
    y = _tn(o_ref[0], wo_ref[...]) + bo_ref[...]
    y_ref[...] = _layer_norm(ALPHA * x_ref[...] + y, lng_ref[...], lnb_ref[...])


def _oproj_t(o_t, x, w_o, b_o, ln_g, ln_b):
    n_seq, nq, seq_len = o_t.shape
    t, d = x.shape
    tm = ROW_TILE
    tiles = seq_len // tm
    row = lambda a: a.reshape(1, -1)
    return pl.pallas_call(
        _oproj_t_kernel,
        grid=(t // tm,),
        in_specs=[pl.BlockSpec((1, nq, tm), lambda i: (i // tiles, 0, i % tiles)),
                  pl.BlockSpec((tm, d), lambda i: (i, 0)), _const_spec((nq, d)),
                  _const_spec((1, d)), _const_spec((1, d)), _const_spec((1, d))],
        out_specs=pl.BlockSpec((tm, d), lambda i: (i, 0)),
        out_shape=jax.ShapeDtypeStruct((t, d), F32),
        compiler_params=_params(("parallel",)),
        name="oproj_t",
    )(o_t, x, w_o.astype(MXU_DTYPE), row(b_o), row(ln_g), row(ln_b))


def _oproj_flat_kernel(o_ref, x_ref, wo_ref, bo_ref, lng_ref, lnb_ref, y_ref):
    y = _mm(o_ref[...].astype(MXU_DTYPE), wo_ref[...]) + bo_ref[...]
    y_ref[...] = _layer_norm(ALPHA * x_ref[...] + y, lng_ref[...], lnb_ref[...])


def _oproj_flat(o, x, w_o, b_o, ln_g, ln_b):
    t, d = x.shape
    tm = ROW_TILE
    row = lambda a: a.reshape(1, -1)
    return pl.pallas_call(
        _oproj_flat_kernel,
        grid=(t // tm,),
        in_specs=[pl.BlockSpec((tm, o.shape[1]), lambda i: (i, 0)), pl.BlockSpec((tm, d), lambda i: (i, 0)),
                  _const_spec(w_o.shape), _const_spec((1, d)), _const_spec((1, d)), _const_spec((1, d))],
        out_specs=pl.BlockSpec((tm, d), lambda i: (i, 0)),
        out_shape=jax.ShapeDtypeStruct((t, d), F32),
        compiler_params=_params(("parallel",)),
        name="oproj_flat",
    )(o, x, w_o.astype(MXU_DTYPE), row(b_o), row(ln_g), row(ln_b))


def _bias_tile_kernel(rel_ref, o_ref):
    h = pl.program_id(0)
    far = rel_ref[NUM_BUCKETS - 1, h]
    key = lax.broadcasted_iota(jnp.int32, (LANES, Q_BLOCK), 0)
    qry = lax.broadcasted_iota(jnp.int32, (LANES, Q_BLOCK), 1)
    for dl in range(o_ref.shape[1]):
        dist = dl * LANES + qry - key
        val = jnp.full((LANES, Q_BLOCK), (rel_ref[0, h] - far) * LOG2E, F32)
        for j in range(1, NUM_BUCKETS):
            val = jnp.where(dist >= BUCKET_THR[j], (rel_ref[j, h] - far) * LOG2E, val)
        o_ref[0, dl] = val


def _bias_tiles(rel_bias):
    n_heads = rel_bias.shape[1]
    return pl.pallas_call(
        _bias_tile_kernel,
        grid=(n_heads,),
        in_specs=[pl.BlockSpec(memory_space=pltpu.SMEM)],
        out_specs=pl.BlockSpec((1, N_BIAS_TILES + 1, LANES, Q_BLOCK), lambda h: (h, 0, 0, 0)),
        out_shape=jax.ShapeDtypeStruct((n_heads, N_BIAS_TILES + 1, LANES, Q_BLOCK), F32),
        compiler_params=_params(("parallel",)),
        name="bias_tiles",
    )(rel_bias)


CMP_PAGES = 32
PE_ROWS = 8


def _compress_kernel(pt_ref, pages_ref, pe_ref, w1_ref, b1_ref, w2_ref, b2_ref, o_ref, buf_ref, sem_ref,
                     *, n_pages, paged):
    b, c = pl.program_id(0), pl.program_id(1)
    n_chunks = pl.num_programs(1)
    step = b * n_chunks + c
    slot = step % 2
    page_rows = buf_ref.shape[1]

    def page_copy(sb, sc, j, sl):
        page = pt_ref[sb, sc * n_pages + j]
        if paged:
            src = pages_ref.at[page]
        else:
            src = pages_ref.at[sb, pl.ds(0, page_rows), pl.ds(pl.multiple_of(page * PAGE_SIZE, PAGE_SIZE), PAGE_SIZE)]
        return pltpu.make_async_copy(src, buf_ref.at[sl, :, j, :], sem_ref.at[sl])

    @pl.when(step == 0)
    def _():
        for j in range(n_pages):
            page_copy(b, c, j, slot).start()

    @pl.when(step + 1 < pl.num_programs(0) * n_chunks)
    def _():
        nxt_c = jnp.where(c + 1 == n_chunks, 0, c + 1)
        nxt_b = jnp.where(c + 1 == n_chunks, b + 1, b)
        for j in range(n_pages):
            page_copy(nxt_b, nxt_c, j, 1 - slot).start()

    for j in range(n_pages):
        page_copy(b, c, j, slot).wait()
    rows = N_KV * n_pages

    def head_dim_row(k, g, d):
        return buf_ref[slot, (k * N_KV + g) * HEAD_DIM + d]

    for k in range(2):
        acc = jnp.zeros((rows + PE_ROWS, 2 * LANES), F32)
        for dd in range(HEAD_DIM // 2):
            parts = [jnp.concatenate([head_dim_row(k, g, 2 * dd), head_dim_row(k, g, 2 * dd + 1)], axis=1)
                     for g in range(N_KV)]
            parts.append(pe_ref[k, dd])
            acc = acc + _mm(jnp.concatenate(parts, axis=0).astype(MXU_DTYPE), w1_ref[k, dd])
        pre = acc[:rows] + acc[rows:rows + 1] + b1_ref[k]
        o_ref[0, 0, k] = _mm(_gelu(pre).astype(MXU_DTYPE), w2_ref[k]) + b2_ref[k]


def _compress_pages(pages, page_table, cmp_pe, cmp_w1, cmp_b1, cmp_w2, cmp_b2):
    paged = page_table is not None
    if paged:
        pages = pages.reshape(pages.shape[0], 2 * N_KV * HEAD_DIM, PAGE_SIZE)
    else:
        page_table = jnp.broadcast_to(jnp.arange(pages.shape[2] // PAGE_SIZE, dtype=jnp.int32)[None],
                                      (pages.shape[0], pages.shape[2] // PAGE_SIZE))
    n_seq, pps = page_table.shape
    bpp = PAGE_SIZE // BLK
    hid = cmp_w1.shape[-1]
    n_pages = min(CMP_PAGES, pps)
    page_rows = 2 * N_KV * HEAD_DIM
    assert pps % n_pages == 0 and n_pages % 8 == 0 and bpp * hid == 2 * LANES and bpp * HEAD_DIM == LANES
    n_chunks = pps // n_pages
    w1r = cmp_w1.reshape(2, BLK, HEAD_DIM // 2, 2, hid)
    eye = jnp.eye(bpp, dtype=cmp_w1.dtype)
    w1bd = jnp.einsum('kldeh,nm->kdenlmh', w1r, eye).reshape(2, HEAD_DIM // 2, 2 * PAGE_SIZE, bpp * hid).astype(MXU_DTYPE)
    pe_t = jnp.tile(jnp.swapaxes(cmp_pe, 1, 2), (1, 1, bpp))
    pe_rows = jnp.pad(pe_t.reshape(2, HEAD_DIM // 2, 1, 2 * PAGE_SIZE), ((0, 0), (0, 0), (0, PE_ROWS - 1), (0, 0)))
    b1 = jnp.tile(cmp_b1, (1, bpp)).reshape(2, 1, bpp * hid)
    w2bd = jnp.einsum('khd,nm->knhmd', cmp_w2, eye).reshape(2, bpp * hid, bpp * HEAD_DIM).astype(MXU_DTYPE)
    b2 = jnp.tile(cmp_b2, (1, bpp)).reshape(2, 1, bpp * HEAD_DIM)
    rows = N_KV * n_pages
    const = lambda a: pl.BlockSpec(a.shape, lambda b, c, pt: (0,) * a.ndim, pipeline_mode=pl.Buffered(1))
    out = pl.pallas_call(
        functools.partial(_compress_kernel, n_pages=n_pages, paged=paged),
        grid_spec=pltpu.PrefetchScalarGridSpec(
            num_scalar_prefetch=1,
            grid=(n_seq, n_chunks),
            in_specs=[pl.BlockSpec(memory_space=pl.ANY)] + [const(a) for a in (pe_rows, w1bd, b1, w2bd, b2)],
            out_specs=pl.BlockSpec((1, 1, 2, rows, bpp * HEAD_DIM), lambda b, c, pt: (b, c, 0, 0, 0)),
            scratch_shapes=[pltpu.VMEM((2, page_rows, n_pages, PAGE_SIZE), F32), pltpu.SemaphoreType.DMA((2,))]),
        out_shape=jax.ShapeDtypeStruct((n_seq, n_chunks, 2, rows, bpp * HEAD_DIM), F32),
        compiler_params=_params(("arbitrary", "arbitrary")),
        name="compress_pages",
    )(page_table, pages, pe_rows, w1bd, b1, w2bd, b2)
    out = out.reshape(n_seq, n_chunks, 2, N_KV, n_pages, bpp, HEAD_DIM)
    return out.transpose(0, 2, 3, 1, 4, 5, 6).reshape(n_seq, 2, N_KV, pps * bpp, HEAD_DIM)


WIN_SUBS = WINDOW // LANES + 1
CMP_LOOKBACK = (FAR_DIST + BLK - 1) // BLK + 1
CMP_BAND = -(-(CMP_LOOKBACK + Q_BLOCK // BLK + 6) // 8) * 8


def _attn_prompt_kernel(rel_ref, qt_ref, gate_ref, kc_ref, vct_ref, sk_ref, svt_ref, wk_ref, wvt_ref, tb_ref, o_ref,
                        sel_ref, lc_ref, s_ref, p_ref, *, group, n_cmp, n_sub):
    g = pl.program_id(1)
    i = pl.program_id(2)
    qt = jnp.concatenate([qt_ref[0, r] for r in range(group)], axis=1)
    lanes = lambda x, r: x[:, r * Q_BLOCK:(r + 1) * Q_BLOCK]

    nl = group * Q_BLOCK
    key = lax.broadcasted_iota(jnp.int32, (LANES, Q_BLOCK), 0)
    qpos = i * Q_BLOCK + lax.broadcasted_iota(jnp.int32, (LANES, Q_BLOCK), 1)
    zero_rows = jnp.zeros((LANES - HEAD_DIM, nl), MXU_DTYPE)
    qt_plain = jnp.concatenate([qt, zero_rows], axis=0)

    def sub_keys(k_ref, t):
        off = pl.multiple_of(jnp.clip(t, 0, n_sub - 1) * LANES, LANES)
        return k_ref[0, 0, 0, pl.ds(off, LANES), :]

    def with_bias(s, dl, madd=None):
        cols = [lanes(s, r) + tb_ref[r, dl] for r in range(group)]
        if madd is not None:
            cols = [c + madd for c in cols]
        return jnp.concatenate(cols, axis=1)

    lc_ref[...] = _mm(kc_ref[0, 0], qt)
    c0 = pl.multiple_of(jnp.clip((i * (Q_BLOCK // BLK) - CMP_LOOKBACK) // 8 * 8, 0, n_cmp - CMP_BAND), 8)
    row_b = c0 + lax.broadcasted_iota(jnp.int32, (CMP_BAND, Q_BLOCK), 0)
    dist_b = i * Q_BLOCK + lax.broadcasted_iota(jnp.int32, (CMP_BAND, Q_BLOCK), 1) - (row_b * BLK + BLK - 1)
    bkt = _bucket_of(dist_b)
    band = []
    for r in range(group):
        h = g * group + r
        far = rel_ref[NUM_BUCKETS - 1, h]
        bias = jnp.full((CMP_BAND, Q_BLOCK), (rel_ref[0, h] - far) * LOG2E, F32)
        for j in range(1, NUM_BUCKETS - 1):
            bias = jnp.where(bkt == j, (rel_ref[j, h] - far) * LOG2E, bias)
        band.append(jnp.where(bkt == NUM_BUCKETS - 1, 0.0, bias))
    lc_ref[pl.ds(c0, CMP_BAND), :] = lc_ref[pl.ds(c0, CMP_BAND), :] + jnp.concatenate(band, axis=1)
    blk = lax.broadcasted_iota(jnp.int32, (n_cmp, Q_BLOCK), 0)
    qpos_c = i * Q_BLOCK + lax.broadcasted_iota(jnp.int32, (n_cmp, Q_BLOCK), 1)
    ok_c = qpos_c >= blk * BLK + BLK - 1
    ok_f = ok_c.astype(F32)
    imp = jnp.zeros((n_cmp, Q_BLOCK), F32)
    pcs = []
    for r in range(group):
        lg = jnp.where(ok_c, lc_ref[:, r * Q_BLOCK:(r + 1) * Q_BLOCK], NEG_INF)
        p = jnp.exp2(lg - jnp.max(lg, axis=0, keepdims=True)) * ok_f
        pc = p * (1.0 / jnp.maximum(jnp.sum(p, axis=0, keepdims=True), 1e-30))
        imp = imp + pc
        pcs.append(pc.astype(MXU_DTYPE))
    o_c = _mm(vct_ref[0, 0], jnp.concatenate(pcs, axis=1))

    parts = []
    for u in range(WIN_SUBS):
        t = i - (WIN_SUBS - 1) + u
        dw = qpos - (t * LANES + key)
        madd = jnp.where((dw >= 0) & (dw < WINDOW) & (t >= 0), 0.0, NEG_INF)
        parts.append(with_bias(_mm(sub_keys(wk_ref, t), qt_plain), WIN_SUBS - 1 - u, madd))
    sw = jnp.concatenate(parts, axis=0)
    pw = jnp.exp2(sw - jnp.max(sw, axis=0, keepdims=True))
    acc_w = jnp.zeros((HEAD_DIM, nl), F32)
    for u in range(WIN_SUBS):
        off = pl.multiple_of(jnp.maximum(i - (WIN_SUBS - 1) + u, 0) * LANES, LANES)
        acc_w = acc_w + _mm(wvt_ref[0, 0, 0, :, pl.ds(off, LANES)], pw[u * LANES:(u + 1) * LANES].astype(MXU_DTYPE))
    o_w = acc_w * (1.0 / jnp.sum(pw, axis=0, keepdims=True))

    cur = qpos_c // BLK
    forced = (blk == 0) | (blk == cur) | (blk == cur - 1)
    score = jnp.where(forced, FORCED_SCORE, jnp.where(blk <= cur, imp, -1.0))
    sel = jnp.zeros((n_cmp, Q_BLOCK), F32)
    for _ in range(min(N_SEL, n_cmp)):
        mx = jnp.max(score, axis=0, keepdims=True)
        idx = jnp.min(jnp.where(score == mx, blk, n_cmp), axis=0, keepdims=True)
        hit = blk == idx
        sel = jnp.where(hit, 1.0, sel)
        score = jnp.where(hit, -2.0, score)
    sel_ref[...] = (sel - 1.0) * (-NEG_INF)

    def step_logits(step):
        mrows = sel_ref[pl.ds(pl.multiple_of(step * STEP_BLKS, STEP_BLKS), STEP_BLKS), :]
        mrows = jnp.concatenate([mrows] * group, axis=1)
        pad = jnp.zeros((LANES - HEAD_DIM - STEP_BLKS, nl), F32)
        q_aug = jnp.concatenate([qt, jnp.concatenate([mrows, pad], axis=0).astype(MXU_DTYPE)], axis=0)
        return [_mm(sub_keys(sk_ref, step * STEP_SUBS + u), q_aug) for u in range(STEP_SUBS)]

    step_keys = STEP_SUBS * LANES

    def qk_store(step, kind):
        cm = jnp.full((1, nl), NEG_INF, F32)
        for u, s in enumerate(step_logits(step)):
            t = step * STEP_SUBS + u
            if kind == 'near':
                s = with_bias(s, jnp.clip(i - t, 0, N_BIAS_TILES))
            elif kind == 'diag':
                s = with_bias(s, jnp.clip(i - t, 0, N_BIAS_TILES), jnp.where(t * LANES + key <= qpos, 0.0, NEG_INF))
            s_ref[pl.ds(pl.multiple_of(t * LANES, LANES), LANES), :] = s
            cm = jnp.maximum(cm, jnp.max(s, axis=0, keepdims=True))
        return cm

    def values(step):
        off = pl.multiple_of(jnp.maximum(step, 0) * step_keys, step_keys)
        return svt_ref[0, 0, 0, :, pl.ds(off, step_keys)]

    def softmax_step(step, m, l, cm):
        m_new = jnp.maximum(m, cm)
        a = jnp.exp2(m - m_new)
        p = jnp.exp2(s_ref[pl.ds(pl.multiple_of(step * step_keys, step_keys), step_keys), :] - m_new)
        return m_new, a * l + jnp.sum(p, axis=0, keepdims=True), a, p.astype(MXU_DTYPE)

    def pipe_step(j, carry, kind):
        m, l, acc, cm, a_prev = carry
        acc = a_prev * acc + _mm(values(j - 1), p_ref[...])
        m, l, a, p = softmax_step(j, m, l, cm)
        p_ref[...] = p
        return m, l, acc, qk_store(j + 1, kind), a

    last = i // STEP_SUBS
    n_far = jnp.maximum((i - (N_BIAS_TILES + STEP_SUBS - 1)) // STEP_SUBS + 1, 0)
    p_ref[...] = jnp.zeros(p_ref.shape, p_ref.dtype)
    carry = (jnp.full((1, nl), NEG_INF, F32), jnp.zeros((1, nl), F32), jnp.zeros((HEAD_DIM, nl), F32),
             qk_store(0, 'diag'), jnp.ones((1, nl), F32))
    n_a = jnp.maximum(n_far - 1, 0)
    carry = lax.fori_loop(0, n_a, functools.partial(pipe_step, kind='far'), carry)
    carry = lax.fori_loop(n_a, last - 1, functools.partial(pipe_step, kind='near'), carry)
    m, l_s, acc_s, cm, a_prev = lax.fori_loop(jnp.maximum(last - 1, 0), last,
                                              functools.partial(pipe_step, kind='diag'), carry)
    acc_s = a_prev * acc_s + _mm(values(last - 1), p_ref[...])
    _, l_s, a, p = softmax_step(last, m, l_s, cm)
    acc_s = a * acc_s + _mm(values(last), p)
    o_s = acc_s * (1.0 / l_s)

    gt = gate_ref[0]
    for r in range(group):
        o = (gt[3 * r:3 * r + 1, :] * lanes(o_c, r) + gt[3 * r + 1:3 * r + 2, :] * lanes(o_s, r)
             + gt[3 * r + 2:3 * r + 3, :] * lanes(o_w, r))
        o_ref[0, r] = o.astype(o_ref.dtype)


def _attn_prompt(q_t, gates_t, kvc, k_rows, v_t, bias_tiles, rel_bias):
    bsz, nq, seq = q_t.shape
    n_heads = nq // HEAD_DIM
    group = n_heads // N_KV
    n_cmp = kvc.shape[3]
    n_sub = seq // LANES
    assert seq % Q_BLOCK == 0 and Q_BLOCK == LANES and n_cmp * BLK == seq and n_sub % STEP_SUBS == 0
    assert WIN_SUBS - 1 <= N_BIAS_TILES and n_cmp >= CMP_BAND and n_cmp % 8 == 0
    kc = kvc[:, 0].astype(MXU_DTYPE)
    vct = jnp.swapaxes(kvc[:, 1], -1, -2).astype(MXU_DTYPE)
    q4 = q_t.reshape(bsz, n_heads, HEAD_DIM, seq)
    v4 = v_t.reshape(bsz, 2, N_KV, HEAD_DIM, seq)
    k_spec = lambda s: pl.BlockSpec((1, 1, 1, seq, LANES), lambda b, g, i: (b, s, g, 0, 0))
    v_spec = lambda s: pl.BlockSpec((1, 1, 1, HEAD_DIM, seq), lambda b, g, i: (b, s, g, 0, 0))
    o = pl.pallas_call(
        functools.partial(_attn_prompt_kernel, group=group, n_cmp=n_cmp, n_sub=n_sub),
        grid=(bsz, N_KV, seq // Q_BLOCK),
        in_specs=[pl.BlockSpec(memory_space=pltpu.SMEM),
                  pl.BlockSpec((1, group, HEAD_DIM, Q_BLOCK), lambda b, g, i: (b, g, 0, i)),
                  pl.BlockSpec((1, GATE_ROWS, Q_BLOCK), lambda b, g, i: (b, g, i)),
                  pl.BlockSpec((1, 1, n_cmp, HEAD_DIM), lambda b, g, i: (b, g, 0, 0)),
                  pl.BlockSpec((1, 1, HEAD_DIM, n_cmp), lambda b, g, i: (b, g, 0, 0)),
                  k_spec(0), v_spec(0), k_spec(1), v_spec(1),
                  pl.BlockSpec((group, N_BIAS_TILES + 1, LANES, Q_BLOCK), lambda b, g, i: (g, 0, 0, 0))],
        out_specs=pl.BlockSpec((1, group, HEAD_DIM, Q_BLOCK), lambda b, g, i: (b, g, 0, i)),
        out_shape=jax.ShapeDtypeStruct((bsz, n_heads, HEAD_DIM, seq), MXU_DTYPE),
        scratch_shapes=[pltpu.VMEM((n_cmp, Q_BLOCK), F32), pltpu.VMEM((n_cmp, group * Q_BLOCK), F32),
                        pltpu.VMEM((seq, group * Q_BLOCK), F32),
                        pltpu.VMEM((STEP_SUBS * LANES, group * Q_BLOCK), MXU_DTYPE)],
        compiler_params=_params(("parallel", "parallel", "arbitrary")),
        name="attn_prompt",
    )(rel_bias, q4, gates_t, kc, vct, k_rows, v4, k_rows, v4, bias_tiles)
    return o.reshape(bsz, nq, seq)


SLC_PAGES = 16
NEW_PAD = 16


def _bias_rows(trow_ref, dist):
    bkt = _bucket_of(dist)
    bias = jnp.broadcast_to(trow_ref[0:1, :], dist.shape)
    for j in range(1, NUM_BUCKETS):
        bias = jnp.where(bkt == j, trow_ref[j:j + 1, :], bias)
    return bias


def _attn_sample_kernel(pt_ref, *refs, n_pages, past_len, n_new, n_cmp, n_blk):
    del pt_ref
    page_refs = refs[:n_pages]
    (qbd_ref, gate_ref, trow_ref, kc_ref, vct_ref, knew_ref, vnewt_ref, wkt_ref, wvt_ref, wknew_ref, wvnewt_ref,
     o_ref, sel_ref, oc_ref, m_ref, l_ref, acc_ref, s_ref) = refs[n_pages:]
    c = pl.program_id(1)
    qbd = qbd_ref[0]
    gd = qbd.shape[0]
    bpp = PAGE_SIZE // BLK
    npb = past_len // BLK

    def qpos_like(shape):
        return past_len + lax.broadcasted_iota(jnp.int32, shape, 1) % n_new

    @pl.when(c == 0)
    def _():
        blk = lax.broadcasted_iota(jnp.int32, (n_cmp, LANES), 0)
        dist = qpos_like((n_cmp, LANES)) - (blk * BLK + BLK - 1)
        ok = dist >= 0
        lg = jnp.where(ok, _mm(kc_ref[0], qbd) + _bias_rows(trow_ref, dist), NEG_INF)
        p = jnp.exp(lg - jnp.max(lg, axis=0, keepdims=True)) * ok.astype(F32)
        pc = p / jnp.maximum(jnp.sum(p, axis=0, keepdims=True), 1e-30)
        oc_ref[...] = _mm(vct_ref[0], pc.astype(MXU_DTYPE))
        quarter = LANES // 4
        imp = pc + pltpu.roll(pc, quarter, 1) + pltpu.roll(pc, 2 * quarter, 1) + pltpu.roll(pc, 3 * quarter, 1)
        nbp = sel_ref.shape[0]
        imp = jnp.concatenate([imp, jnp.zeros((nbp - n_cmp, LANES), F32)], axis=0)
        blk = lax.broadcasted_iota(jnp.int32, (nbp, LANES), 0)
        cur = qpos_like((nbp, LANES)) // BLK
        forced = (blk == 0) | (blk == cur) | (blk == cur - 1)
        score = jnp.where(forced, FORCED_SCORE, jnp.where(blk <= cur, imp, -1.0))
        score = jnp.where(blk >= n_blk, -2.0, score)
        sel = jnp.zeros((nbp, LANES), F32)
        for _ in range(min(N_SEL, n_blk)):
            mx = jnp.max(score, axis=0, keepdims=True)
            idx = jnp.min(jnp.where(score == mx, blk, nbp), axis=0, keepdims=True)
            hit = blk == idx
            sel = jnp.where(hit, 1.0, sel)
            score = jnp.where(hit, -3.0, score)
        sel_ref[...] = sel
        m_ref[...] = jnp.full(m_ref.shape, NEG_INF, F32)
        l_ref[...] = jnp.zeros(l_ref.shape, F32)
        acc_ref[...] = jnp.zeros(acc_ref.shape, F32)

    tok = lax.broadcasted_iota(jnp.int32, (PAGE_SIZE, LANES), 0)
    qpos_p = qpos_like((PAGE_SIZE, LANES))

    far_pages = min(max((past_len - FAR_DIST - (PAGE_SIZE - 1)) // PAGE_SIZE + 1, 0), past_len // PAGE_SIZE)
    assert past_len // PAGE_SIZE - far_pages <= n_pages
    first_near = far_pages - (past_len // PAGE_SIZE - n_pages)

    def page_logits(near):
        for j in range(n_pages):
            pg = c * n_pages + j
            st = _tn(page_refs[j][0, 0].reshape(gd, PAGE_SIZE).astype(MXU_DTYPE), qbd)
            if near and j >= first_near:
                st = st + _bias_rows(trow_ref, qpos_p - (pg * PAGE_SIZE + tok))
            else:
                st = st + trow_ref[NUM_BUCKETS - 1:NUM_BUCKETS, :]
            chosen = jnp.broadcast_to(sel_ref[pl.ds(pg * bpp + bpp - 1, 1), :], (PAGE_SIZE, LANES))
            for t in reversed(range(bpp - 1)):
                chosen = jnp.where(tok < (t + 1) * BLK, sel_ref[pl.ds(pg * bpp + t, 1), :], chosen)
            s_ref[j * PAGE_SIZE:(j + 1) * PAGE_SIZE, :] = jnp.where(chosen > 0.5, st, NEG_INF)

    is_last = c == pl.num_programs(1) - 1
    pl.when(jnp.logical_not(is_last))(functools.partial(page_logits, False))
    pl.when(is_last)(functools.partial(page_logits, True))
    s_all = s_ref[...]
    m_old = m_ref[...]
    m_new = jnp.maximum(m_old, jnp.max(s_all, axis=0, keepdims=True))
    a = jnp.exp(m_old - m_new)
    p = jnp.exp(s_all - m_new)
    acc = acc_ref[...] * a
    for j in range(n_pages):
        vt = page_refs[j][0, 1].reshape(gd, PAGE_SIZE).astype(MXU_DTYPE)
        acc = acc + _mm(vt, p[j * PAGE_SIZE:(j + 1) * PAGE_SIZE].astype(MXU_DTYPE))
    acc_ref[...] = acc
    l_ref[...] = a * l_ref[...] + jnp.sum(p, axis=0, keepdims=True)
    m_ref[...] = m_new

    @pl.when(c == pl.num_programs(1) - 1)
    def _():
        tok_n = lax.broadcasted_iota(jnp.int32, (NEW_PAD, LANES), 0)
        dist_n = qpos_like((NEW_PAD, LANES)) - (past_len + tok_n)
        live_n = (tok_n < n_new) & (dist_n >= 0)
        bias_n = _bias_rows(trow_ref, dist_n)
        chosen_n = sel_ref[pl.ds(npb, 1), :] > 0.5
        s_n = jnp.where(live_n & chosen_n, _mm(knew_ref[0], qbd) + bias_n, NEG_INF)
        m_old = m_ref[...]
        m_fin = jnp.maximum(m_old, jnp.max(s_n, axis=0, keepdims=True))
        a = jnp.exp(m_old - m_fin)
        p_n = jnp.exp(s_n - m_fin)
        acc_s = acc_ref[...] * a + _mm(vnewt_ref[0], p_n.astype(MXU_DTYPE))
        o_s = acc_s / (a * l_ref[...] + jnp.sum(p_n, axis=0, keepdims=True))
        wb = wkt_ref.shape[-1]
        tok_w = lax.broadcasted_iota(jnp.int32, (wb, LANES), 0)
        kwpos = past_len - wb + tok_w
        dw = qpos_like((wb, LANES)) - kwpos
        ok_w = (dw >= 0) & (dw < WINDOW) & (kwpos >= 0)
        s_w = _tn(wkt_ref[0, 0].reshape(gd, wb).astype(MXU_DTYPE), qbd) + _bias_rows(trow_ref, dw)
        s_w = jnp.where(ok_w, s_w, NEG_INF)
        s_wn = jnp.where(live_n & (dist_n < WINDOW), _mm(wknew_ref[0], qbd) + bias_n, NEG_INF)
        m_w = jnp.maximum(jnp.max(s_w, axis=0, keepdims=True), jnp.max(s_wn, axis=0, keepdims=True))
        p_w = jnp.exp(s_w - m_w)
        p_wn = jnp.exp(s_wn - m_w)
        o_w = (_mm(wvt_ref[0, 0].reshape(gd, wb).astype(MXU_DTYPE), p_w.astype(MXU_DTYPE))
               + _mm(wvnewt_ref[0], p_wn.astype(MXU_DTYPE)))
        o_w = o_w / (jnp.sum(p_w, axis=0, keepdims=True) + jnp.sum(p_wn, axis=0, keepdims=True))
        gt = gate_ref[0]
        tot = gt[0:1, :] * oc_ref[...] + gt[1:2, :] * o_s + gt[2:3, :] * o_w
        lane_g = (lax.broadcasted_iota(jnp.int32, (HEAD_DIM, LANES), 1) // n_new) % N_KV
        out = jnp.zeros((HEAD_DIM, LANES), F32)
        for g in range(N_KV):
            out = out + jnp.where(lane_g == g, tot[g * HEAD_DIM:(g + 1) * HEAD_DIM, :], 0.0)
        o_ref[0] = out


def _attn_sample(q, gates, kvc, kv_new_t, slc_pages, page_table, win_t, rel_bias):
    bsz, n_new, nq = q.shape
    n_heads = nq // HEAD_DIM
    group = n_heads // N_KV
    pps = page_table.shape[1]
    past_len = pps * PAGE_SIZE
    n_cmp = kvc.shape[3]
    n_blk = past_len // BLK + -(-n_new // BLK)
    nbp = -(-n_blk // 8) * 8
    gd = N_KV * HEAD_DIM
    wb = win_t.shape[-1]
    assert n_heads * n_new == LANES and group == 4 and n_new <= min(BLK, NEW_PAD) and pps % SLC_PAGES == 0
    assert n_cmp == past_len // BLK and n_cmp % 8 == 0
    q5 = q.reshape(bsz, n_new, N_KV, group, HEAD_DIM)
    qbd = jnp.einsum('hg,bqgrd->bhdrgq', jnp.eye(N_KV, dtype=q.dtype), q5).reshape(bsz, gd, LANES).astype(MXU_DTYPE)
    g5 = gates[:, :, :3 * n_heads].reshape(bsz, n_new, N_KV, group, 3)
    gate_t = jnp.pad(g5.transpose(0, 4, 3, 2, 1).reshape(bsz, 3, LANES), ((0, 0), (0, 5), (0, 0)))
    trow = jnp.broadcast_to(rel_bias.reshape(NUM_BUCKETS, N_KV, group).transpose(0, 2, 1)[..., None],
                            (NUM_BUCKETS, group, N_KV, n_new)).reshape(NUM_BUCKETS, LANES)
    kc = kvc[:, 0].transpose(0, 2, 1, 3).reshape(bsz, n_cmp, gd).astype(MXU_DTYPE)
    vct = kvc[:, 1].transpose(0, 1, 3, 2).reshape(bsz, gd, n_cmp).astype(MXU_DTYPE)
    new = kv_new_t.reshape(6, gd, bsz, n_new).transpose(0, 2, 1, 3)
    new = jnp.pad(new, ((0, 0), (0, 0), (0, 0), (0, NEW_PAD - n_new))).astype(MXU_DTYPE)
    k_rows = lambda s: jnp.swapaxes(new[s], 1, 2)
    page_spec = lambda j: pl.BlockSpec((1, 2, N_KV, HEAD_DIM, PAGE_SIZE),
                                       lambda b, c, pt: (pt[b, c * SLC_PAGES + j], 0, 0, 0, 0))
    per_b = lambda a: pl.BlockSpec((1,) + a.shape[1:], lambda b, c, pt: (b,) + (0,) * (a.ndim - 1))
    win_spec = lambda s: pl.BlockSpec((1, 1, N_KV, HEAD_DIM, wb), lambda b, c, pt: (b, s, 0, 0, 0))
    smalls = (qbd, gate_t)
    o_t = pl.pallas_call(
        functools.partial(_attn_sample_kernel, n_pages=SLC_PAGES, past_len=past_len, n_new=n_new, n_cmp=n_cmp,
                          n_blk=n_blk),
        grid_spec=pltpu.PrefetchScalarGridSpec(
            num_scalar_prefetch=1,
            grid=(bsz, pps // SLC_PAGES),
            in_specs=[page_spec(j) for j in range(SLC_PAGES)]
            + [per_b(qbd), per_b(gate_t), pl.BlockSpec(trow.shape, lambda b, c, pt: (0, 0)),
               per_b(kc), per_b(vct), per_b(k_rows(2)), per_b(new[3]), win_spec(0), win_spec(1),
               per_b(k_rows(4)), per_b(new[5])],
            out_specs=pl.BlockSpec((1, HEAD_DIM, LANES), lambda b, c, pt: (b, 0, 0)),
            scratch_shapes=[pltpu.VMEM((nbp, LANES), F32), pltpu.VMEM((gd, LANES), F32), pltpu.VMEM((1, LANES), F32),
                            pltpu.VMEM((1, LANES), F32), pltpu.VMEM((gd, LANES), F32),
                            pltpu.VMEM((SLC_PAGES * PAGE_SIZE, LANES), F32)]),
        out_shape=jax.ShapeDtypeStruct((bsz, HEAD_DIM, LANES), F32),
        compiler_params=_params(("parallel", "arbitrary")),
        name="attn_sample",
    )(page_table, *([slc_pages] * SLC_PAGES), qbd, gate_t, trow, kc, vct, k_rows(2), new[3], win_t, win_t,
      k_rows(4), new[5])
    del smalls
    return o_t.reshape(bsz, HEAD_DIM, group, N_KV, n_new).transpose(0, 4, 3, 2, 1).reshape(bsz, n_new, nq)


def _tokens_last(x):
    nd = x.ndim
    return jnp.transpose(x, tuple(range(nd - 4)) + (nd - 3, nd - 2, nd - 1, nd - 4))


def _tokens_first(x):
    nd = x.ndim
    return jnp.transpose(x, tuple(range(nd - 4)) + (nd - 1, nd - 4, nd - 3, nd - 2))


def kernel(x_prompt, x_sample, cache_kv_cmp, cache_kv_slc, cache_win, state_conv, page_table, ln_g, ln_b, sg_w_in, sg_b_in, sg_ln_g, sg_ln_b, sg_w_s, sg_b_s, sg_w_out, sg_b_out, ffn_w_up, ffn_b_up, ffn_w_dw, ffn_b_dw, ffn_w_down, ffn_b_down, kv_w, cmp_pe, cmp_w1, cmp_b1, cmp_w2, cmp_b2, nsa_w_qg, nsa_b_qg, nsa_w_o, nsa_b_o, rel_bias):
    depth = ln_g.shape[0]
    n_a = sg_w_in.shape[0]
    assert depth == DEPTH
    bsz, seq, d_model = x_prompt.shape
    dbsz, n_new, _ = x_sample.shape
    d_ff = ffn_w_down.shape[1]
    kvd = (2, N_KV, HEAD_DIM)
    gd = N_KV * HEAD_DIM
    cmp_w = (cmp_pe, cmp_w1, cmp_b1, cmp_w2, cmp_b2)

    def trunk(x, seq_len, conv_state, attend_layer):
        n_seq = x.shape[0] // seq_len
        new_conv, sg_v, kv_t, ctx = [], [], None, None
        for l in range(depth):
            if l < n_a:
                x, v = _sgu_layer(x, seq_len, sg_w_in[l], sg_b_in[l], sg_ln_g[l], sg_ln_b[l], sg_w_s[l], sg_b_s[l],
                                  sg_w_out[l], sg_b_out[l], ln_g[l, 0], ln_b[l, 0])
                sg_v.append(v.reshape(n_seq, seq_len, -1))
            else:
                i = l - n_a
                x = attend_layer(x, ctx, nsa_w_qg[i], nsa_b_qg[i], nsa_w_o[i], nsa_b_o[i], ln_g[l, 0], ln_b[l, 0])
            x, cs = _ffn_layer(x, seq_len, conv_state[l], ffn_w_up[l], ffn_b_up[l], ffn_w_dw[l], ffn_b_dw[l],
                               ffn_w_down[l], ffn_b_down[l], ln_g[l, 1], ln_b[l, 1])
            new_conv.append(cs)
            if l == n_a - 1:
                kv_t, ctx = attend_layer.prepare(x)
        return x, jnp.stack(new_conv), (jnp.stack(sg_v) if sg_v else None), kv_t

    bias_tiles = _bias_tiles(rel_bias)

    def prompt_prepare(x):
        kv_t, v_t, k_rows = _kv_proj(x, bsz, kv_w)
        kvc = _compress_pages(kv_t, None, *cmp_w)
        return kv_t, (kvc, k_rows, v_t)

    def prompt_attend(x, ctx, w_qg, b_qg, w_o, b_o, g, b):
        kvc, k_rows, v_t = ctx
        q_t, gates_t = _qg_proj_t(x, bsz, w_qg, b_qg)
        o_t = _attn_prompt(q_t, gates_t, kvc, k_rows, v_t, bias_tiles, rel_bias)
        return _oproj_t(o_t, x, w_o, b_o, g, b)

    prompt_attend.prepare = prompt_prepare
    zeros_state = jnp.zeros((depth, bsz, CONV_W - 1, d_ff), x_prompt.dtype)
    y_p, conv_p, _, kv_p = trunk(x_prompt.reshape(bsz * seq, d_model), seq, zeros_state, prompt_attend)
    kv_p = kv_p.reshape(bsz, 3, 2, N_KV, HEAD_DIM, seq)

    cmp_pages = _tokens_last(cache_kv_cmp)
    slc_pages = _tokens_last(cache_kv_slc)
    win_t = _tokens_last(cache_win)

    def sample_prepare(x):
        kv_t, _, _ = _kv_proj(x, 1, kv_w)
        kvc = _compress_pages(cmp_pages, page_table, *cmp_w)
        return kv_t[0], (kvc, kv_t[0])

    def sample_attend(x, ctx, w_qg, b_qg, w_o, b_o, g, b):
        kvc, kv_new_t = ctx
        q, gates = _qg_proj_flat(x, w_qg, b_qg)
        o = _attn_sample(q.reshape(dbsz, n_new, -1), gates.reshape(dbsz, n_new, -1), kvc, kv_new_t, slc_pages,
                         page_table, win_t, rel_bias)
        return _oproj_flat(o.reshape(dbsz * n_new, -1), x, w_o, b_o, g, b)

    sample_attend.prepare = sample_prepare
    assert n_new < BLK
    y_s, conv_s, sgv_s, kv_s = trunk(x_sample.reshape(dbsz * n_new, d_model), n_new, state_conv, sample_attend)
    kv_s = kv_s.reshape(3, 2, N_KV, HEAD_DIM, dbsz, n_new)

    win_len = min(WINDOW, seq)
    out_p = [_tokens_first(kv_p[:, s]) for s in range(3)]
    out_s = [jnp.transpose(kv_s[s], (3, 4, 0, 1, 2)) for s in range(3)]
    return (y_p.reshape(bsz, seq, d_model), y_s.reshape(dbsz, n_new, d_model),
            out_p[0], out_p[1], out_p[2][:, seq - win_len:], conv_p,
            out_s[0], out_s[1], out_s[2], conv_s, sgv_s)
```

```python
import functools
import math

import numpy as np
import jax
import jax.numpy as jnp
from jax import lax
from jax.experimental import pallas as pl
from jax.experimental.pallas import tpu as pltpu

N_SG_GROUPS = 8
CHUNK = 128
CONV_W = 3
HEAD_DIM = 64
N_KV = 4
BLK = 64
N_SEL = 16
WINDOW = 512
NUM_BUCKETS = 32
REL_MAX_DIST = 1024
Q_BLOCK = 128
PAGE_SIZE = 128
DEPTH = 2
ALPHA = (2 * DEPTH) ** 0.25
LN_EPS = 1e-5
NEG_INF = -1e30
FORCED_SCORE = 1e9
LOG2E = math.log2(math.e)

MXU_DTYPE = jnp.bfloat16
F32 = jnp.float32
LANES = 128
ROW_TILE = 256
SGU_ROW_TILE = 512
VMEM_LIMIT = 56 * 1024 * 1024


def _bucket_thresholds():
    d = np.arange(0, 4 * REL_MAX_DIST)
    max_exact = NUM_BUCKETS // 2
    nf = np.maximum(d, 1).astype(np.float32)
    large = max_exact + (np.log(nf / np.float32(max_exact)) / np.float32(math.log(REL_MAX_DIST / max_exact))
                         * np.float32(NUM_BUCKETS - max_exact)).astype(np.int32)
    b = np.where(d < max_exact, d, np.minimum(large, NUM_BUCKETS - 1))
    assert np.all(np.diff(b) >= 0) and b[-1] == NUM_BUCKETS - 1
    return [int(np.nonzero(b >= j)[0][0]) for j in range(NUM_BUCKETS)]


BUCKET_THR = _bucket_thresholds()
FAR_DIST = BUCKET_THR[-1]
N_BIAS_TILES = -(-(FAR_DIST + LANES) // LANES)


def _gelu(x):
    return 0.5 * x * (1.0 + lax.erf(x * (1.0 / math.sqrt(2.0))))


def _layer_norm(x, g, b):
    mu = jnp.mean(x, axis=-1, keepdims=True)
    xc = x - mu
    var = jnp.mean(xc * xc, axis=-1, keepdims=True)
    return xc * lax.rsqrt(var + LN_EPS) * g + b


def _mm(a, b):
    return jnp.dot(a, b, preferred_element_type=F32)


def _mm_nt(a, b):
    return lax.dot_general(a, b, (((1,), (1,)), ((), ())), preferred_element_type=F32)


def _tn(a, b):
    return lax.dot_general(a, b, (((0,), (0,)), ((), ())), preferred_element_type=F32)


def _const_spec(shape):
    nd = len(shape)
    return pl.BlockSpec(shape, lambda *_: (0,) * nd, pipeline_mode=pl.Buffered(1))


def _params(sem):
    return pltpu.CompilerParams(dimension_semantics=sem, vmem_limit_bytes=VMEM_LIMIT)


def _bucket_of(dist):
    b = jnp.zeros(dist.shape, jnp.int32)
    for j in range(1, NUM_BUCKETS):
        b = b + (dist >= BUCKET_THR[j]).astype(jnp.int32)
    return b


def _sgu_kernel(x_ref, win_ref, bin_ref, gv_ref, bv_ref, wg_ref, bs_ref, wout_ref, bout_ref, lng_ref, lnb_ref,
                o_ref, *maybe_v_ref, d_sg):
    x = x_ref[...]
    xb = x.astype(MXU_DTYPE)
    v = _gelu(_mm(xb, win_ref[:, d_sg:]) + bin_ref[:, d_sg:])
    vn = _layer_norm(v, gv_ref[...], bv_ref[...])
    if maybe_v_ref:
        maybe_v_ref[0][...] = vn
    cg = d_sg // N_SG_GROUPS
    mix = wg_ref.shape[1]
    acc = jnp.zeros(o_ref.shape, F32)
    for g in range(N_SG_GROUPS):
        lo, hi = g * cg, (g + 1) * cg
        vg = vn[:, lo:hi].astype(MXU_DTYPE)
        gate = jnp.concatenate([_mm(wg_ref[g], vg[r:r + mix]) + bs_ref[g] for r in range(0, x.shape[0], mix)], axis=0)
        u = _gelu(_mm(xb, win_ref[:, lo:hi]) + bin_ref[:, lo:hi])
        acc = acc + _mm((u * gate).astype(MXU_DTYPE), wout_ref[lo:hi, :])
    y = acc + bout_ref[...]
    o_ref[...] = _layer_norm(ALPHA * x + y, lng_ref[...], lnb_ref[...])


def _sgu_layer(x, seq_len, w_in, b_in, g_v, b_v, w_s, b_s, w_out, b_out, ln_g, ln_b, emit_v):
    t, d = x.shape
    d_sg = w_out.shape[0]
    cg = d_sg // N_SG_GROUPS
    cl = min(CHUNK, seq_len)
    tm = min(SGU_ROW_TILE, t)
    mix = min(ROW_TILE, tm)
    assert seq_len % cl == 0 and mix % cl == 0 and tm % mix == 0 and t % tm == 0
    w_tri = jnp.tril(w_s[:, :cl, :cl])
    eye = jnp.eye(mix // cl, dtype=w_s.dtype)
    wg = jnp.einsum('ab,gts->gatbs', eye, w_tri).reshape(N_SG_GROUPS, mix, mix).astype(MXU_DTYPE)
    bs = jnp.broadcast_to(jnp.tile(b_s[:, :cl], (1, mix // cl))[:, :, None], (N_SG_GROUPS, mix, cg))
    row = lambda a: a.reshape(1, -1)
    out_specs = [pl.BlockSpec((tm, d), lambda i: (i, 0))]
    out_shape = [jax.ShapeDtypeStruct((t, d), F32)]
    if emit_v:
        out_specs.append(pl.BlockSpec((tm, d_sg), lambda i: (i, 0)))
        out_shape.append(jax.ShapeDtypeStruct((t, d_sg), F32))
    outs = pl.pallas_call(
        functools.partial(_sgu_kernel, d_sg=d_sg),
        grid=(t // tm,),
        in_specs=[pl.BlockSpec((tm, d), lambda i: (i, 0)),
                  _const_spec((d, 2 * d_sg)), _const_spec((1, 2 * d_sg)), _const_spec((1, d_sg)), _const_spec((1, d_sg)),
                  _const_spec((N_SG_GROUPS, mix, mix)), _const_spec((N_SG_GROUPS, mix, cg)),
                  _const_spec((d_sg, d)), _const_spec((1, d)), _const_spec((1, d)), _const_spec((1, d))],
        out_specs=out_specs,
        out_shape=out_shape,
        compiler_params=_params(("parallel",)),
        name="sgu_layer",
    )(x, w_in.astype(MXU_DTYPE), row(b_in), row(g_v), row(b_v), wg, bs, w_out.astype(MXU_DTYPE), row(b_out),
      row(ln_g), row(ln_b))
    return outs[0], (outs[1] if emit_v else None)


def _ffn_tail(x, a, gt, s1, s2, wdw_ref, bdw_ref, wdown_ref, bdown_ref, lng_ref, lnb_ref, o_ref):
    c = bdw_ref[...] + s2 * wdw_ref[0:1, :] + s1 * wdw_ref[1:2, :] + a * wdw_ref[2:3, :]
    h = (_gelu(c) * gt).astype(MXU_DTYPE)
    y = _mm(h, wdown_ref[...]) + bdown_ref[...]
    o_ref[...] = _layer_norm(ALPHA * x + y, lng_ref[...], lnb_ref[...])


def _ffn_long_kernel(x_ref, xh_ref, st_ref, wup_ref, bup_ref, wdw_ref, bdw_ref, wdown_ref, bdown_ref, lng_ref, lnb_ref,
                     o_ref, conv_ref, *, seq_len, d_ff):
    tm = x_ref.shape[0]
    x = x_ref[...]
    xb = x.astype(MXU_DTYPE)
    a = _mm(xb, wup_ref[:, :d_ff]) + bup_ref[:, :d_ff]
    gt = _mm(xb, wup_ref[:, d_ff:]) + bup_ref[:, d_ff:]
    ah = _mm(xh_ref[...].astype(MXU_DTYPE), wup_ref[:, :d_ff]) + bup_ref[:, :d_ff]
    at_start = (pl.program_id(0) * tm) % seq_len == 0
    p1 = jnp.where(at_start, st_ref[0, 1:2, :], ah[7:8, :])
    p0 = jnp.where(at_start, st_ref[0, 0:1, :], ah[6:7, :])
    r = lax.broadcasted_iota(jnp.int32, a.shape, 0)
    s1 = jnp.where(r == 0, p1, pltpu.roll(a, 1, 0))
    s2 = jnp.where(r == 0, p0, jnp.where(r == 1, p1, pltpu.roll(a, 2, 0)))
    conv_ref[0] = a[tm - (CONV_W - 1):, :]
    _ffn_tail(x, a, gt, s1, s2, wdw_ref, bdw_ref, wdown_ref, bdown_ref, lng_ref, lnb_ref, o_ref)


def _ffn_short_kernel(x_ref, p0_ref, p1_ref, wup_ref, bup_ref, wdw_ref, bdw_ref, wdown_ref, bdown_ref, lng_ref, lnb_ref,
                      o_ref, a_ref, *, seq_len, d_ff):
    x = x_ref[...]
    xb = x.astype(MXU_DTYPE)
    a = _mm(xb, wup_ref[:, :d_ff]) + bup_ref[:, :d_ff]
    gt = _mm(xb, wup_ref[:, d_ff:]) + bup_ref[:, d_ff:]
    pos = lax.broadcasted_iota(jnp.int32, a.shape, 0) % seq_len
    s1 = jnp.where(pos == 0, p1_ref[...], pltpu.roll(a, 1, 0))
    s2 = jnp.where(pos == 0, p0_ref[...], jnp.where(pos == 1, p1_ref[...], pltpu.roll(a, 2, 0)))
    a_ref[...] = a
    _ffn_tail(x, a, gt, s1, s2, wdw_ref, bdw_ref, wdown_ref, bdown_ref, lng_ref, lnb_ref, o_ref)


def _ffn_layer(x, seq_len, conv_state, w_up, b_up, w_dw, b_dw, w_down, b_down, ln_g, ln_b):
    t, d = x.shape
    d_ff = w_down.shape[0]
    n_seq = t // seq_len
    tm = ROW_TILE
    assert seq_len >= CONV_W - 1 and t % tm == 0
    row = lambda a: a.reshape(1, -1)
    weights = (w_up.astype(MXU_DTYPE), row(b_up), w_dw, row(b_dw), w_down.astype(MXU_DTYPE), row(b_down), row(ln_g), row(ln_b))
    w_specs = [_const_spec((d, 2 * d_ff)), _const_spec((1, 2 * d_ff)), _const_spec((CONV_W, d_ff)), _const_spec((1, d_ff)),
               _const_spec((d_ff, d)), _const_spec((1, d)), _const_spec((1, d)), _const_spec((1, d))]
    if seq_len % tm == 0:
        tiles_per_seq = seq_len // tm
        out, conv = pl.pallas_call(
            functools.partial(_ffn_long_kernel, seq_len=seq_len, d_ff=d_ff),
            grid=(t // tm,),
            in_specs=[pl.BlockSpec((tm, d), lambda i: (i, 0)),
                      pl.BlockSpec((8, d), lambda i: (jnp.maximum(i * (tm // 8) - 1, 0), 0)),
                      pl.BlockSpec((1, CONV_W - 1, d_ff), lambda i: (i // tiles_per_seq, 0, 0))] + w_specs,
            out_specs=[pl.BlockSpec((tm, d), lambda i: (i, 0)),
                       pl.BlockSpec((1, CONV_W - 1, d_ff), lambda i: (i // tiles_per_seq, 0, 0))],
            out_shape=[jax.ShapeDtypeStruct((t, d), F32), jax.ShapeDtypeStruct((n_seq, CONV_W - 1, d_ff), F32)],
            compiler_params=_params(("arbitrary",)),
            name="ffn_long",
        )(x, x, conv_state, *weights)
        return out, conv
    assert tm % seq_len == 0
    p0 = jnp.repeat(conv_state[:, 0, :], seq_len, axis=0)
    p1 = jnp.repeat(conv_state[:, 1, :], seq_len, axis=0)
    out, a = pl.pallas_call(
        functools.partial(_ffn_short_kernel, seq_len=seq_len, d_ff=d_ff),
        grid=(t // tm,),
        in_specs=[pl.BlockSpec((tm, d), lambda i: (i, 0)), pl.BlockSpec((tm, d_ff), lambda i: (i, 0)),
                  pl.BlockSpec((tm, d_ff), lambda i: (i, 0))] + w_specs,
        out_specs=[pl.BlockSpec((tm, d), lambda i: (i, 0)), pl.BlockSpec((tm, d_ff), lambda i: (i, 0))],
        out_shape=[jax.ShapeDtypeStruct((t, d), F32), jax.ShapeDtypeStruct((t, d_ff), F32)],
        compiler_params=_params(("parallel",)),
        name="ffn_short",
    )(x, p0, p1, *weights)
    return out, a.reshape(n_seq, seq_len, d_ff)[:, seq_len - (CONV_W - 1):, :]


STEP_SUBS = 4
STEP_BLKS = STEP_SUBS * LANES // BLK


def _kv_kernel(x_ref, wt_ref, wk_ref, kv_ref, vt_ref, k_ref, *, tiles):
    tm = x_ref.shape[0]
    xb = x_ref[...].astype(MXU_DTYPE)
    r = _mm_nt(wt_ref[...], xb)
    gd = N_KV * HEAD_DIM
    for pair in range(3):
        kv_ref[pair, 0] = r[pair * 2 * gd:(pair + 1) * 2 * gd, :]
    pos = (pl.program_id(0) % tiles) * tm + lax.broadcasted_iota(jnp.int32, (tm, LANES), 0)
    lane = lax.broadcasted_iota(jnp.int32, (tm, LANES), 1)
    onehot = jnp.where((pos // BLK) % STEP_BLKS + HEAD_DIM == lane, 1.0, 0.0)
    for s in range(2):
        vt_ref[0, s] = r[(2 * s + 3) * gd:(2 * s + 4) * gd, :].astype(MXU_DTYPE)
        for g in range(N_KV):
            k_ref[0, s, g] = (_mm(xb, wk_ref[s, g]) + onehot).astype(MXU_DTYPE)


def _kv_proj(x, n_seq, kv_w):
    t, d = x.shape
    seq_len = t // n_seq
    n_out = kv_w.shape[1]
    gd = N_KV * HEAD_DIM
    tm = ROW_TILE
    assert seq_len % tm == 0 and n_out == 6 * gd and HEAD_DIM + STEP_BLKS <= LANES
    tiles = seq_len // tm
    wk = kv_w.reshape(d, 3, 2, N_KV, HEAD_DIM)[:, 1:, 0].transpose(1, 2, 0, 3)
    wk = jnp.pad(wk, ((0, 0), (0, 0), (0, 0), (0, LANES - HEAD_DIM))).astype(MXU_DTYPE)
    return pl.pallas_call(
        functools.partial(_kv_kernel, tiles=tiles),
        grid=(t // tm,),
        in_specs=[pl.BlockSpec((tm, d), lambda i: (i, 0)), _const_spec((n_out, d)), _const_spec(wk.shape)],
        out_specs=[pl.BlockSpec((3, 1, 2 * gd, tm), lambda i: (0, i // tiles, 0, i % tiles)),
                   pl.BlockSpec((1, 2, gd, tm), lambda i: (i // tiles, 0, 0, i % tiles)),
                   pl.BlockSpec((1, 2, N_KV, tm, LANES), lambda i: (i // tiles, 0, 0, i % tiles, 0))],
        out_shape=[jax.ShapeDtypeStruct((3, n_seq, 2 * gd, seq_len), F32),
                   jax.ShapeDtypeStruct((n_seq, 2, gd, seq_len), MXU_DTYPE),
                   jax.ShapeDtypeStruct((n_seq, 2, N_KV, seq_len, LANES), MXU_DTYPE)],
        compiler_params=_params(("parallel",)),
        name="kv_proj",
    )(x, kv_w.T.astype(MXU_DTYPE), wk)


def _sigmoid(x):
    return 1.0 / (1.0 + jnp.exp(-x))


GATE_ROWS = 16


def _qg_t_kernel(x_ref, wq_ref, bq_ref, wg_ref, bg_ref, q_ref, gate_ref):
    xb = x_ref[...].astype(MXU_DTYPE)
    q_ref[0] = ((_mm_nt(wq_ref[...], xb) + bq_ref[...]) * (HEAD_DIM ** -0.5 * LOG2E)).astype(MXU_DTYPE)
    gate_ref[0] = _sigmoid(_mm_nt(wg_ref[...], xb) + bg_ref[...])


def _qg_proj_t(x, n_seq, w_qg, b_qg):
    t, d = x.shape
    seq_len = t // n_seq
    n_heads = (w_qg.shape[1] // (HEAD_DIM + 3))
    group = n_heads // N_KV
    nq = n_heads * HEAD_DIM
    tm = ROW_TILE
    assert seq_len % tm == 0 and 3 * group <= GATE_ROWS
    tiles = seq_len // tm
    pad = GATE_ROWS - 3 * group
    wg = jnp.pad(w_qg[:, nq:].T.reshape(N_KV, 3 * group, d), ((0, 0), (0, pad), (0, 0))).reshape(N_KV * GATE_ROWS, d)
    bg = jnp.pad(b_qg[nq:].reshape(N_KV, 3 * group), ((0, 0), (0, pad))).reshape(N_KV * GATE_ROWS, 1)
    bg = jnp.broadcast_to(bg, (N_KV * GATE_ROWS, tm))
    bq = jnp.broadcast_to(b_qg[:nq].reshape(nq, 1), (nq, tm))
    return pl.pallas_call(
        _qg_t_kernel,
        grid=(t // tm,),
        in_specs=[pl.BlockSpec((tm, d), lambda i: (i, 0)), _const_spec((nq, d)), _const_spec((nq, tm)),
                  _const_spec(wg.shape), _const_spec(bg.shape)],
        out_specs=[pl.BlockSpec((1, nq, tm), lambda i: (i // tiles, 0, i % tiles)),
                   pl.BlockSpec((1, N_KV * GATE_ROWS, tm), lambda i: (i // tiles, 0, i % tiles))],
        out_shape=[jax.ShapeDtypeStruct((n_seq, nq, seq_len), MXU_DTYPE),
                   jax.ShapeDtypeStruct((n_seq, N_KV * GATE_ROWS, seq_len), F32)],
        compiler_params=_params(("parallel",)),
        name="qg_proj_t",
    )(x, w_qg[:, :nq].T.astype(MXU_DTYPE), bq, wg.astype(MXU_DTYPE), bg)


def _qg_flat_kernel(x_ref, wq_ref, bq_ref, wg_ref, bg_ref, q_ref, gate_ref):
    xb = x_ref[...].astype(MXU_DTYPE)
    q_ref[...] = (_mm(xb, wq_ref[...]) + bq_ref[...]) * (HEAD_DIM ** -0.5)
    gate_ref[...] = _sigmoid(_mm(xb, wg_ref[...]) + bg_ref[...])


def _qg_proj_flat(x, w_qg, b_qg):
    t, d = x.shape
    n_gate = w_qg.shape[1] // (HEAD_DIM + 3) * 3
    nq = w_qg.shape[1] - n_gate
    tm = ROW_TILE
    assert t % tm == 0 and n_gate <= LANES
    wg = jnp.pad(w_qg[:, nq:], ((0, 0), (0, LANES - n_gate))).astype(MXU_DTYPE)
    bg = jnp.pad(b_qg[nq:], (0, LANES - n_gate)).reshape(1, LANES)
    return pl.pallas_call(
        _qg_flat_kernel,
        grid=(t // tm,),
        in_specs=[pl.BlockSpec((tm, d), lambda i: (i, 0)), _const_spec((d, nq)), _const_spec((1, nq)),
                  _const_spec((d, LANES)), _const_spec((1, LANES))],
        out_specs=[pl.BlockSpec((tm, nq), lambda i: (i, 0)), pl.BlockSpec((tm, LANES), lambda i: (i, 0))],
        out_shape=[jax.ShapeDtypeStruct((t, nq), F32), jax.ShapeDtypeStruct((t, LANES), F32)],
        compiler_params=_params(("parallel",)),
        name="qg_proj_flat",
    )(x, w_qg[:, :nq].astype(MXU_DTYPE), b_qg[:nq].reshape(1, nq), wg, bg)


def _oproj_t_kernel(o_ref, x_ref, wo_ref, bo_ref, lng_ref, lnb_ref, y_ref):
    y = _tn(o_ref[0], wo_ref[...]) + bo_ref[...]
    y_ref[...] = _layer_norm(ALPHA * x_ref[...] + y, lng_ref[...], lnb_ref[...])


def _oproj_t(o_t, x, w_o, b_o, ln_g, ln_b):
    n_seq, nq, seq_len = o_t.shape
    t, d = x.shape
    tm = ROW_TILE
    tiles = seq_len // tm
    row = lambda a: a.reshape(1, -1)
    return pl.pallas_call(
        _oproj_t_kernel,
        grid=(t // tm,),
        in_specs=[pl.BlockSpec((1, nq, tm), lambda i: (i // tiles, 0, i % tiles)),
                  pl.BlockSpec((tm, d), lambda i: (i, 0)), _const_spec((nq, d)),
                  _const_spec((1, d)), _const_spec((1, d)), _const_spec((1, d))],
        out_specs=pl.BlockSpec((tm, d), lambda i: (i, 0)),
        out_shape=jax.ShapeDtypeStruct((t, d), F32),
        compiler_params=_params(("parallel",)),
        name="oproj_t",
    )(o_t, x, w_o.astype(MXU_DTYPE), row(b_o), row(ln_g), row(ln_b))


def _oproj_flat_kernel(o_ref, x_ref, wo_ref, bo_ref, lng_ref, lnb_ref, y_ref):
    y = _mm(o_ref[...].astype(MXU_DTYPE), wo_ref[...]) + bo_ref[...]
    y_ref[...] = _layer_norm(ALPHA * x_ref[...] + y, lng_ref[...], lnb_ref[...])


def _oproj_flat(o, x, w_o, b_o, ln_g, ln_b):
    t, d = x.shape
    tm = ROW_TILE
    row = lambda a: a.reshape(1, -1)
    return pl.pallas_call(
        _oproj_flat_kernel,
        grid=(t // tm,),
        in_specs=[pl.BlockSpec((tm, o.shape[1]), lambda i: (i, 0)), pl.BlockSpec((tm, d), lambda i: (i, 0)),
                  _const_spec(w_o.shape), _const_spec((1, d)), _const_spec((1, d)), _const_spec((1, d))],
        out_specs=pl.BlockSpec((tm, d), lambda i: (i, 0)),
        out_shape=jax.ShapeDtypeStruct((t, d), F32),
        compiler_params=_params(("parallel",)),
        name="oproj_flat",
    )(o, x, w_o.astype(MXU_DTYPE), row(b_o), row(ln_g), row(ln_b))


def _bias_tile_kernel(rel_ref, o_ref):
    h = pl.program_id(0)
    far = rel_ref[NUM_BUCKETS - 1, h]
    key = lax.broadcasted_iota(jnp.int32, (LANES, Q_BLOCK), 0)
    qry = lax.broadcasted_iota(jnp.int32, (LANES, Q_BLOCK), 1)
    for dl in range(o_ref.shape[1]):
        dist = dl * LANES + qry - key
        val = jnp.full((LANES, Q_BLOCK), (rel_ref[0, h] - far) * LOG2E, F32)
        for j in range(1, NUM_BUCKETS):
            val = jnp.where(dist >= BUCKET_THR[j], (rel_ref[j, h] - far) * LOG2E, val)
        o_ref[0, dl] = val


def _bias_tiles(rel_bias):
    n_heads = rel_bias.shape[1]
    return pl.pallas_call(
        _bias_tile_kernel,
        grid=(n_heads,),
        in_specs=[pl.BlockSpec(memory_space=pltpu.SMEM)],
        out_specs=pl.BlockSpec((1, N_BIAS_TILES + 1, LANES, Q_BLOCK), lambda h: (h, 0, 0, 0)),
        out_shape=jax.ShapeDtypeStruct((n_heads, N_BIAS_TILES + 1, LANES, Q_BLOCK), F32),
        compiler_params=_params(("parallel",)),
        name="bias_tiles",
    )(rel_bias)


CMP_PAGES = 32
PE_ROWS = 8


def _compress_kernel(pt_ref, pages_ref, pe_ref, w1_ref, b1_ref, w2_ref, b2_ref, o_ref, buf_ref, sem_ref,
                     *, n_pages, paged):
    b, c = pl.program_id(0), pl.program_id(1)
    n_chunks = pl.num_programs(1)
    step = b * n_chunks + c
    slot = step % 2
    page_rows = buf_ref.shape[1]

    def page_copy(sb, sc, j, sl):
        page = pt_ref[sb, sc * n_pages + j]
        if paged:
            src = pages_ref.at[page]
        else:
            src = pages_ref.at[0, sb, :, pl.ds(pl.multiple_of(page * PAGE_SIZE, PAGE_SIZE), PAGE_SIZE)]
        return pltpu.make_async_copy(src, buf_ref.at[sl, :, j, :], sem_ref.at[sl])

    @pl.when(step == 0)
    def _():
        for j in range(n_pages):
            page_copy(b, c, j, slot).start()

    @pl.when(step + 1 < pl.num_programs(0) * n_chunks)
    def _():
        nxt_c = jnp.where(c + 1 == n_chunks, 0, c + 1)
        nxt_b = jnp.where(c + 1 == n_chunks, b + 1, b)
        for j in range(n_pages):
            page_copy(nxt_b, nxt_c, j, 1 - slot).start()

    for j in range(n_pages):
        page_copy(b, c, j, slot).wait()
    rows = N_KV * n_pages

    def head_dim_row(k, g, d):
        return buf_ref[slot, (k * N_KV + g) * HEAD_DIM + d]

    for k in range(2):
        acc = jnp.zeros((rows + PE_ROWS, 2 * LANES), F32)
        for dd in range(HEAD_DIM // 2):
            parts = [jnp.concatenate([head_dim_row(k, g, 2 * dd), head_dim_row(k, g, 2 * dd + 1)], axis=1)
                     for g in range(N_KV)]
            parts.append(pe_ref[k, dd])
            acc = acc + _mm(jnp.concatenate(parts, axis=0).astype(MXU_DTYPE), w1_ref[k, dd])
        pre = acc[:rows] + acc[rows:rows + 1] + b1_ref[k]
        o_ref[0, 0, k] = _mm(_gelu(pre).astype(MXU_DTYPE), w2_ref[k]) + b2_ref[k]


def _compress_pages(pages, page_table, cmp_pe, cmp_w1, cmp_b1, cmp_w2, cmp_b2):
    paged = page_table is not None
    if paged:
        pages = pages.reshape(pages.shape[0], 2 * N_KV * HEAD_DIM, PAGE_SIZE)
    else:
        assert pages.shape[2] == 2 * N_KV * HEAD_DIM
        page_table = jnp.broadcast_to(jnp.arange(pages.shape[3] // PAGE_SIZE, dtype=jnp.int32)[None],
                                      (pages.shape[1], pages.shape[3] // PAGE_SIZE))
    n_seq, pps = page_table.shape
    bpp = PAGE_SIZE // BLK
    hid = cmp_w1.shape[-1]
    n_pages = min(CMP_PAGES, pps)
    page_rows = 2 * N_KV * HEAD_DIM
    assert pps % n_pages == 0 and n_pages % 8 == 0 and bpp * hid == 2 * LANES and bpp * HEAD_DIM == LANES
    n_chunks = pps // n_pages
    w1r = cmp_w1.reshape(2, BLK, HEAD_DIM // 2, 2, hid)
    eye = jnp.eye(bpp, dtype=cmp_w1.dtype)
    w1bd = jnp.einsum('kldeh,nm->kdenlmh', w1r, eye).reshape(2, HEAD_DIM // 2, 2 * PAGE_SIZE, bpp * hid).astype(MXU_DTYPE)
    pe_t = jnp.tile(jnp.swapaxes(cmp_pe, 1, 2), (1, 1, bpp))
    pe_rows = jnp.pad(pe_t.reshape(2, HEAD_DIM // 2, 1, 2 * PAGE_SIZE), ((0, 0), (0, 0), (0, PE_ROWS - 1), (0, 0)))
    b1 = jnp.tile(cmp_b1, (1, bpp)).reshape(2, 1, bpp * hid)
    w2bd = jnp.einsum('khd,nm->knhmd', cmp_w2, eye).reshape(2, bpp * hid, bpp * HEAD_DIM).astype(MXU_DTYPE)
    b2 = jnp.tile(cmp_b2, (1, bpp)).reshape(2, 1, bpp * HEAD_DIM)
    rows = N_KV * n_pages
    const = lambda a: pl.BlockSpec(a.shape, lambda b, c, pt: (0,) * a.ndim, pipeline_mode=pl.Buffered(1))
    out = pl.pallas_call(
        functools.partial(_compress_kernel, n_pages=n_pages, paged=paged),
        grid_spec=pltpu.PrefetchScalarGridSpec(
            num_scalar_prefetch=1,
            grid=(n_seq, n_chunks),
            in_specs=[pl.BlockSpec(memory_space=pl.ANY)] + [const(a) for a in (pe_rows, w1bd, b1, w2bd, b2)],
            out_specs=pl.BlockSpec((1, 1, 2, rows, bpp * HEAD_DIM), lambda b, c, pt: (b, c, 0, 0, 0)),
            scratch_shapes=[pltpu.VMEM((2, page_rows, n_pages, PAGE_SIZE), F32), pltpu.SemaphoreType.DMA((2,))]),
        out_shape=jax.ShapeDtypeStruct((n_seq, n_chunks, 2, rows, bpp * HEAD_DIM), F32),
        compiler_params=_params(("arbitrary", "arbitrary")),
        name="compress_pages",
    )(page_table, pages, pe_rows, w1bd, b1, w2bd, b2)
    out = out.reshape(n_seq, n_chunks, 2, N_KV, n_pages, bpp, HEAD_DIM)
    return out.transpose(0, 2, 3, 1, 4, 5, 6).reshape(n_seq, 2, N_KV, pps * bpp, HEAD_DIM)


WIN_SUBS = WINDOW // LANES + 1
CMP_LOOKBACK = (FAR_DIST + BLK - 1) // BLK + 1
CMP_BAND = -(-(CMP_LOOKBACK + Q_BLOCK // BLK + 6) // 8) * 8


def _attn_prompt_kernel(rel_ref, qt_ref, gate_ref, kc_ref, vct_ref, sk_ref, svt_ref, wk_ref, wvt_ref, tb_ref, o_ref,
                        sel_ref, lc_ref, s_ref, p_ref, *, group, n_cmp, n_sub):
    g = pl.program_id(1)
    i = pl.program_id(2)
    qt = jnp.concatenate([qt_ref[0, r] for r in range(group)], axis=1)
    lanes = lambda x, r: x[:, r * Q_BLOCK:(r + 1) * Q_BLOCK]

    nl = group * Q_BLOCK
    key = lax.broadcasted_iota(jnp.int32, (LANES, Q_BLOCK), 0)
    qpos = i * Q_BLOCK + lax.broadcasted_iota(jnp.int32, (LANES, Q_BLOCK), 1)
    zero_rows = jnp.zeros((LANES - HEAD_DIM, nl), MXU_DTYPE)
    qt_plain = jnp.concatenate([qt, zero_rows], axis=0)

    def sub_keys(k_ref, t):
        off = pl.multiple_of(jnp.clip(t, 0, n_sub - 1) * LANES, LANES)
        return k_ref[0, 0, 0, pl.ds(off, LANES), :]

    def with_bias(s, dl, madd=None):
        cols = [lanes(s, r) + tb_ref[r, dl] for r in range(group)]
        if madd is not None:
            cols = [c + madd for c in cols]
        return jnp.concatenate(cols, axis=1)

    lc_ref[...] = _mm(kc_ref[0, 0], qt)
    c0 = pl.multiple_of(jnp.clip((i * (Q_BLOCK // BLK) - CMP_LOOKBACK) // 8 * 8, 0, n_cmp - CMP_BAND), 8)
    row_b = c0 + lax.broadcasted_iota(jnp.int32, (CMP_BAND, Q_BLOCK), 0)
    dist_b = i * Q_BLOCK + lax.broadcasted_iota(jnp.int32, (CMP_BAND, Q_BLOCK), 1) - (row_b * BLK + BLK - 1)
    bkt = _bucket_of(dist_b)
    band = []
    for r in range(group):
        h = g * group + r
        far = rel_ref[NUM_BUCKETS - 1, h]
        bias = jnp.full((CMP_BAND, Q_BLOCK), (rel_ref[0, h] - far) * LOG2E, F32)
        for j in range(1, NUM_BUCKETS - 1):
            bias = jnp.where(bkt == j, (rel_ref[j, h] - far) * LOG2E, bias)
        band.append(jnp.where(bkt == NUM_BUCKETS - 1, 0.0, bias))
    lc_ref[pl.ds(c0, CMP_BAND), :] = lc_ref[pl.ds(c0, CMP_BAND), :] + jnp.concatenate(band, axis=1)
    blk = lax.broadcasted_iota(jnp.int32, (n_cmp, Q_BLOCK), 0)
    qpos_c = i * Q_BLOCK + lax.broadcasted_iota(jnp.int32, (n_cmp, Q_BLOCK), 1)
    ok_c = qpos_c >= blk * BLK + BLK - 1
    ok_f = ok_c.astype(F32)
    imp = jnp.zeros((n_cmp, Q_BLOCK), F32)
    pcs = []
    for r in range(group):
        lg = jnp.where(ok_c, lc_ref[:, r * Q_BLOCK:(r + 1) * Q_BLOCK], NEG_INF)
        p = jnp.exp2(lg - jnp.max(lg, axis=0, keepdims=True)) * ok_f
        pc = p * (1.0 / jnp.maximum(jnp.sum(p, axis=0, keepdims=True), 1e-30))
        imp = imp + pc
        pcs.append(pc.astype(MXU_DTYPE))
    o_c = _mm(vct_ref[0, 0], jnp.concatenate(pcs, axis=1))

    parts = []
    for u in range(WIN_SUBS):
        t = i - (WIN_SUBS - 1) + u
        dw = qpos - (t * LANES + key)
        madd = jnp.where((dw >= 0) & (dw < WINDOW) & (t >= 0), 0.0, NEG_INF)
        parts.append(with_bias(_mm(sub_keys(wk_ref, t), qt_plain), WIN_SUBS - 1 - u, madd))
    sw = jnp.concatenate(parts, axis=0)
    pw = jnp.exp2(sw - jnp.max(sw, axis=0, keepdims=True))
    acc_w = jnp.zeros((HEAD_DIM, nl), F32)
    for u in range(WIN_SUBS):
        off = pl.multiple_of(jnp.maximum(i - (WIN_SUBS - 1) + u, 0) * LANES, LANES)
        acc_w = acc_w + _mm(wvt_ref[0, 0, 0, :, pl.ds(off, LANES)], pw[u * LANES:(u + 1) * LANES].astype(MXU_DTYPE))
    o_w = acc_w * (1.0 / jnp.sum(pw, axis=0, keepdims=True))

    cur = qpos_c // BLK
    forced = (blk == 0) | (blk == cur) | (blk == cur - 1)
    score = jnp.where(forced, FORCED_SCORE, jnp.where(blk <= cur, imp, -1.0))
    sel = jnp.zeros((n_cmp, Q_BLOCK), F32)
    for _ in range(min(N_SEL, n_cmp)):
        mx = jnp.max(score, axis=0, keepdims=True)
        idx = jnp.min(jnp.where(score == mx, blk, n_cmp), axis=0, keepdims=True)
        hit = blk == idx
        sel = jnp.where(hit, 1.0, sel)
        score = jnp.where(hit, -2.0, score)
    sel_ref[...] = (sel - 1.0) * (-NEG_INF)

    def step_logits(step):
        mrows = sel_ref[pl.ds(pl.multiple_of(step * STEP_BLKS, STEP_BLKS), STEP_BLKS), :]
        mrows = jnp.concatenate([mrows] * group, axis=1)
        pad = jnp.zeros((LANES - HEAD_DIM - STEP_BLKS, nl), F32)
        q_aug = jnp.concatenate([qt, jnp.concatenate([mrows, pad], axis=0).astype(MXU_DTYPE)], axis=0)
        return [_mm(sub_keys(sk_ref, step * STEP_SUBS + u), q_aug) for u in range(STEP_SUBS)]

    step_keys = STEP_SUBS * LANES

    def qk_store(step, kind):
        cm = jnp.full((1, nl), NEG_INF, F32)
        for u, s in enumerate(step_logits(step)):
            t = step * STEP_SUBS + u
            if kind == 'near':
                s = with_bias(s, jnp.clip(i - t, 0, N_BIAS_TILES))
            elif kind == 'diag':
                s = with_bias(s, jnp.clip(i - t, 0, N_BIAS_TILES), jnp.where(t * LANES + key <= qpos, 0.0, NEG_INF))
            s_ref[pl.ds(pl.multiple_of(t * LANES, LANES), LANES), :] = s
            cm = jnp.maximum(cm, jnp.max(s, axis=0, keepdims=True))
        return cm

    def values(step):
        off = pl.multiple_of(jnp.maximum(step, 0) * step_keys, step_keys)
        return svt_ref[0, 0, 0, :, pl.ds(off, step_keys)]

    def softmax_step(step, m, l, cm):
        m_new = jnp.maximum(m, cm)
        a = jnp.exp2(m - m_new)
        p = jnp.exp2(s_ref[pl.ds(pl.multiple_of(step * step_keys, step_keys), step_keys), :] - m_new)
        return m_new, a * l + jnp.sum(p, axis=0, keepdims=True), a, p.astype(MXU_DTYPE)

    def pipe_step(j, carry, kind):
        m, l, acc, cm, a_prev = carry
        acc = a_prev * acc + _mm(values(j - 1), p_ref[...])
        m, l, a, p = softmax_step(j, m, l, cm)
        p_ref[...] = p
        return m, l, acc, qk_store(j + 1, kind), a

    last = i // STEP_SUBS
    n_far = jnp.maximum((i - (N_BIAS_TILES + STEP_SUBS - 1)) // STEP_SUBS + 1, 0)
    p_ref[...] = jnp.zeros(p_ref.shape, p_ref.dtype)
    carry = (jnp.full((1, nl), NEG_INF, F32), jnp.zeros((1, nl), F32), jnp.zeros((HEAD_DIM, nl), F32),
             qk_store(0, 'diag'), jnp.ones((1, nl), F32))
    n_a = jnp.maximum(n_far - 1, 0)
    carry = lax.fori_loop(0, n_a, functools.partial(pipe_step, kind='far'), carry)
    carry = lax.fori_loop(n_a, last - 1, functools.partial(pipe_step, kind='near'), carry)
    m, l_s, acc_s, cm, a_prev = lax.fori_loop(jnp.maximum(last - 1, 0), last,
                                              functools.partial(pipe_step, kind='diag'), carry)
    acc_s = a_prev * acc_s + _mm(values(last - 1), p_ref[...])
    _, l_s, a, p = softmax_step(last, m, l_s, cm)
    acc_s = a * acc_s + _mm(values(last), p)
    o_s = acc_s * (1.0 / l_s)

    gt = gate_ref[0]
    for r in range(group):
        o = (gt[3 * r:3 * r + 1, :] * lanes(o_c, r) + gt[3 * r + 1:3 * r + 2, :] * lanes(o_s, r)
             + gt[3 * r + 2:3 * r + 3, :] * lanes(o_w, r))
        o_ref[0, r] = o.astype(o_ref.dtype)


def _attn_prompt(q_t, gates_t, kvc, k_rows, v_t, bias_tiles, rel_bias):
    bsz, nq, seq = q_t.shape
    n_heads = nq // HEAD_DIM
    group = n_heads // N_KV
    n_cmp = kvc.shape[3]
    n_sub = seq // LANES
    assert seq % Q_BLOCK == 0 and Q_BLOCK == LANES and n_cmp * BLK == seq and n_sub % STEP_SUBS == 0
    assert WIN_SUBS - 1 <= N_BIAS_TILES and n_cmp >= CMP_BAND and n_cmp % 8 == 0
    kc = kvc[:, 0].astype(MXU_DTYPE)
    vct = jnp.swapaxes(kvc[:, 1], -1, -2).astype(MXU_DTYPE)
    q4 = q_t.reshape(bsz, n_heads, HEAD_DIM, seq)
    v4 = v_t.reshape(bsz, 2, N_KV, HEAD_DIM, seq)
    k_spec = lambda s: pl.BlockSpec((1, 1, 1, seq, LANES), lambda b, g, i: (b, s, g, 0, 0))
    v_spec = lambda s: pl.BlockSpec((1, 1, 1, HEAD_DIM, seq), lambda b, g, i: (b, s, g, 0, 0))
    o = pl.pallas_call(
        functools.partial(_attn_prompt_kernel, group=group, n_cmp=n_cmp, n_sub=n_sub),
        grid=(bsz, N_KV, seq // Q_BLOCK),
        in_specs=[pl.BlockSpec(memory_space=pltpu.SMEM),
                  pl.BlockSpec((1, group, HEAD_DIM, Q_BLOCK), lambda b, g, i: (b, g, 0, i)),
                  pl.BlockSpec((1, GATE_ROWS, Q_BLOCK), lambda b, g, i: (b, g, i)),
                  pl.BlockSpec((1, 1, n_cmp, HEAD_DIM), lambda b, g, i: (b, g, 0, 0)),
                  pl.BlockSpec((1, 1, HEAD_DIM, n_cmp), lambda b, g, i: (b, g, 0, 0)),
                  k_spec(0), v_spec(0), k_spec(1), v_spec(1),
                  pl.BlockSpec((group, N_BIAS_TILES + 1, LANES, Q_BLOCK), lambda b, g, i: (g, 0, 0, 0))],
        out_specs=pl.BlockSpec((1, group, HEAD_DIM, Q_BLOCK), lambda b, g, i: (b, g, 0, i)),
        out_shape=jax.ShapeDtypeStruct((bsz, n_heads, HEAD_DIM, seq), MXU_DTYPE),
        scratch_shapes=[pltpu.VMEM((n_cmp, Q_BLOCK), F32), pltpu.VMEM((n_cmp, group * Q_BLOCK), F32),
                        pltpu.VMEM((seq, group * Q_BLOCK), F32),
                        pltpu.VMEM((STEP_SUBS * LANES, group * Q_BLOCK), MXU_DTYPE)],
        compiler_params=_params(("parallel", "parallel", "arbitrary")),
        name="attn_prompt",
    )(rel_bias, q4, gates_t, kc, vct, k_rows, v4, k_rows, v4, bias_tiles)
    return o.reshape(bsz, nq, seq)


SLC_PAGES = 32
NEW_PAD = 16


def _bias_rows(trow_ref, dist):
    bkt = _bucket_of(dist)
    bias = jnp.broadcast_to(trow_ref[0:1, :], dist.shape)
    for j in range(1, NUM_BUCKETS):
        bias = jnp.where(bkt == j, trow_ref[j:j + 1, :], bias)
    return bias


def _attn_sample_kernel(pt_ref, *refs, n_pages, past_len, n_new, n_cmp, n_blk):
    del pt_ref
    page_refs = refs[:n_pages]
    (qbd_ref, gate_ref, trow_ref, kc_ref, vct_ref, knew_ref, vnewt_ref, wkt_ref, wvt_ref, wknew_ref, wvnewt_ref,
     o_ref, sel_ref, oc_ref, m_ref, l_ref, acc_ref, s_ref) = refs[n_pages:]
    c = pl.program_id(1)
    qbd = qbd_ref[0]
    gd = qbd.shape[0]
    bpp = PAGE_SIZE // BLK
    npb = past_len // BLK

    def qpos_like(shape):
        return past_len + lax.broadcasted_iota(jnp.int32, shape, 1) % n_new

    @pl.when(c == 0)
    def _():
        blk = lax.broadcasted_iota(jnp.int32, (n_cmp, LANES), 0)
        dist = qpos_like((n_cmp, LANES)) - (blk * BLK + BLK - 1)
        ok = dist >= 0
        lg = jnp.where(ok, _mm(kc_ref[0], qbd) + _bias_rows(trow_ref, dist), NEG_INF)
        p = jnp.exp(lg - jnp.max(lg, axis=0, keepdims=True)) * ok.astype(F32)
        pc = p / jnp.maximum(jnp.sum(p, axis=0, keepdims=True), 1e-30)
        oc_ref[...] = _mm(vct_ref[0], pc.astype(MXU_DTYPE))
        quarter = LANES // 4
        imp = pc + pltpu.roll(pc, quarter, 1) + pltpu.roll(pc, 2 * quarter, 1) + pltpu.roll(pc, 3 * quarter, 1)
        nbp = sel_ref.shape[0]
        imp = jnp.concatenate([imp, jnp.zeros((nbp - n_cmp, LANES), F32)], axis=0)
        blk = lax.broadcasted_iota(jnp.int32, (nbp, LANES), 0)
        cur = qpos_like((nbp, LANES)) // BLK
        forced = (blk == 0) | (blk == cur) | (blk == cur - 1)
        score = jnp.where(forced, FORCED_SCORE, jnp.where(blk <= cur, imp, -1.0))
        score = jnp.where(blk >= n_blk, -2.0, score)
        sel = jnp.zeros((nbp, LANES), F32)
        for _ in range(min(N_SEL, n_blk)):
            mx = jnp.max(score, axis=0, keepdims=True)
            idx = jnp.min(jnp.where(score == mx, blk, nbp), axis=0, keepdims=True)
            hit = blk == idx
            sel = jnp.where(hit, 1.0, sel)
            score = jnp.where(hit, -3.0, score)
        sel_ref[...] = sel
        m_ref[...] = jnp.full(m_ref.shape, NEG_INF, F32)
        l_ref[...] = jnp.zeros(l_ref.shape, F32)
        acc_ref[...] = jnp.zeros(acc_ref.shape, F32)

    tok = lax.broadcasted_iota(jnp.int32, (PAGE_SIZE, LANES), 0)
    qpos_p = qpos_like((PAGE_SIZE, LANES))

    far_pages = min(max((past_len - FAR_DIST - (PAGE_SIZE - 1)) // PAGE_SIZE + 1, 0), past_len // PAGE_SIZE)
    assert past_len // PAGE_SIZE - far_pages <= n_pages
    first_near = far_pages - (past_len // PAGE_SIZE - n_pages)

    def page_logits(near):
        for j in range(n_pages):
            pg = c * n_pages + j
            st = _tn(page_refs[j][0, 0].reshape(gd, PAGE_SIZE).astype(MXU_DTYPE), qbd)
            if near and j >= first_near:
                st = st + _bias_rows(trow_ref, qpos_p - (pg * PAGE_SIZE + tok))
            else:
                st = st + trow_ref[NUM_BUCKETS - 1:NUM_BUCKETS, :]
            chosen = jnp.broadcast_to(sel_ref[pl.ds(pg * bpp + bpp - 1, 1), :], (PAGE_SIZE, LANES))
            for t in reversed(range(bpp - 1)):
                chosen = jnp.where(tok < (t + 1) * BLK, sel_ref[pl.ds(pg * bpp + t, 1), :], chosen)
            s_ref[j * PAGE_SIZE:(j + 1) * PAGE_SIZE, :] = jnp.where(chosen > 0.5, st, NEG_INF)

    is_last = c == pl.num_programs(1) - 1
    pl.when(jnp.logical_not(is_last))(functools.partial(page_logits, False))
    pl.when(is_last)(functools.partial(page_logits, True))
    s_all = s_ref[...]
    m_old = m_ref[...]
    m_new = jnp.maximum(m_old, jnp.max(s_all, axis=0, keepdims=True))
    a = jnp.exp(m_old - m_new)
    p = jnp.exp(s_all - m_new)
    acc = acc_ref[...] * a
    for j in range(n_pages):
        vt = page_refs[j][0, 1].reshape(gd, PAGE_SIZE).astype(MXU_DTYPE)
        acc = acc + _mm(vt, p[j * PAGE_SIZE:(j + 1) * PAGE_SIZE].astype(MXU_DTYPE))
    acc_ref[...] = acc
    l_ref[...] = a * l_ref[...] + jnp.sum(p, axis=0, keepdims=True)
    m_ref[...] = m_new

    @pl.when(c == pl.num_programs(1) - 1)
    def _():
        tok_n = lax.broadcasted_iota(jnp.int32, (NEW_PAD, LANES), 0)
        dist_n = qpos_like((NEW_PAD, LANES)) - (past_len + tok_n)
        live_n = (tok_n < n_new) & (dist_n >= 0)
        bias_n = _bias_rows(trow_ref, dist_n)
        chosen_n = sel_ref[pl.ds(npb, 1), :] > 0.5
        s_n = jnp.where(live_n & chosen_n, _mm(knew_ref[0], qbd) + bias_n, NEG_INF)
        m_old = m_ref[...]
        m_fin = jnp.maximum(m_old, jnp.max(s_n, axis=0, keepdims=True))
        a = jnp.exp(m_old - m_fin)
        p_n = jnp.exp(s_n - m_fin)
        acc_s = acc_ref[...] * a + _mm(vnewt_ref[0], p_n.astype(MXU_DTYPE))
        o_s = acc_s / (a * l_ref[...] + jnp.sum(p_n, axis=0, keepdims=True))
        wb = wkt_ref.shape[-1]
        tok_w = lax.broadcasted_iota(jnp.int32, (wb, LANES), 0)
        kwpos = past_len - wb + tok_w
        dw = qpos_like((wb, LANES)) - kwpos
        ok_w = (dw >= 0) & (dw < WINDOW) & (kwpos >= 0)
        s_w = _tn(wkt_ref[0, 0].reshape(gd, wb).astype(MXU_DTYPE), qbd) + _bias_rows(trow_ref, dw)
        s_w = jnp.where(ok_w, s_w, NEG_INF)
        s_wn = jnp.where(live_n & (dist_n < WINDOW), _mm(wknew_ref[0], qbd) + bias_n, NEG_INF)
        m_w = jnp.maximum(jnp.max(s_w, axis=0, keepdims=True), jnp.max(s_wn, axis=0, keepdims=True))
        p_w = jnp.exp(s_w - m_w)
        p_wn = jnp.exp(s_wn - m_w)
        o_w = (_mm(wvt_ref[0, 0].reshape(gd, wb).astype(MXU_DTYPE), p_w.astype(MXU_DTYPE))
               + _mm(wvnewt_ref[0], p_wn.astype(MXU_DTYPE)))
        o_w = o_w / (jnp.sum(p_w, axis=0, keepdims=True) + jnp.sum(p_wn, axis=0, keepdims=True))
        gt = gate_ref[0]
        tot = gt[0:1, :] * oc_ref[...] + gt[1:2, :] * o_s + gt[2:3, :] * o_w
        lane_g = (lax.broadcasted_iota(jnp.int32, (HEAD_DIM, LANES), 1) // n_new) % N_KV
        out = jnp.zeros((HEAD_DIM, LANES), F32)
        for g in range(N_KV):
            out = out + jnp.where(lane_g == g, tot[g * HEAD_DIM:(g + 1) * HEAD_DIM, :], 0.0)
        o_ref[0] = out


def _attn_sample(q, gates, kvc, kv_new_t, slc_pages, page_table, win_t, rel_bias):
    bsz, n_new, nq = q.shape
    n_heads = nq // HEAD_DIM
    group = n_heads // N_KV
    pps = page_table.shape[1]
    past_len = pps * PAGE_SIZE
    n_cmp = kvc.shape[3]
    n_blk = past_len // BLK + -(-n_new // BLK)
    nbp = -(-n_blk // 8) * 8
    gd = N_KV * HEAD_DIM
    wb = win_t.shape[-1]
    n_pages = min(SLC_PAGES, pps)
    assert n_heads * n_new == LANES and group == 4 and n_new <= min(BLK, NEW_PAD) and pps % n_pages == 0
    assert n_cmp == past_len // BLK and n_cmp % 8 == 0
    q5 = q.reshape(bsz, n_new, N_KV, group, HEAD_DIM)
    qbd = jnp.einsum('hg,bqgrd->bhdrgq', jnp.eye(N_KV, dtype=q.dtype), q5).reshape(bsz, gd, LANES).astype(MXU_DTYPE)
    g5 = gates[:, :, :3 * n_heads].reshape(bsz, n_new, N_KV, group, 3)
    gate_t = jnp.pad(g5.transpose(0, 4, 3, 2, 1).reshape(bsz, 3, LANES), ((0, 0), (0, 5), (0, 0)))
    trow = jnp.broadcast_to(rel_bias.reshape(NUM_BUCKETS, N_KV, group).transpose(0, 2, 1)[..., None],
                            (NUM_BUCKETS, group, N_KV, n_new)).reshape(NUM_BUCKETS, LANES)
    kc = kvc[:, 0].transpose(0, 2, 1, 3).reshape(bsz, n_cmp, gd).astype(MXU_DTYPE)
    vct = kvc[:, 1].transpose(0, 1, 3, 2).reshape(bsz, gd, n_cmp).astype(MXU_DTYPE)
    new = kv_new_t.reshape(6, gd, bsz, n_new).transpose(0, 2, 1, 3)
    new = jnp.pad(new, ((0, 0), (0, 0), (0, 0), (0, NEW_PAD - n_new))).astype(MXU_DTYPE)
    k_rows = lambda s: jnp.swapaxes(new[s], 1, 2)
    page_spec = lambda j: pl.BlockSpec((1, 2, N_KV, HEAD_DIM, PAGE_SIZE),
                                       lambda b, c, pt: (pt[b, c * n_pages + j], 0, 0, 0, 0))
    per_b = lambda a: pl.BlockSpec((1,) + a.shape[1:], lambda b, c, pt: (b,) + (0,) * (a.ndim - 1))
    win_spec = lambda s: pl.BlockSpec((1, 1, N_KV, HEAD_DIM, wb), lambda b, c, pt: (b, s, 0, 0, 0))
    smalls = (qbd, gate_t)
    o_t = pl.pallas_call(
        functools.partial(_attn_sample_kernel, n_pages=n_pages, past_len=past_len, n_new=n_new, n_cmp=n_cmp,
                          n_blk=n_blk),
        grid_spec=pltpu.PrefetchScalarGridSpec(
            num_scalar_prefetch=1,
            grid=(bsz, pps // n_pages),
            in_specs=[page_spec(j) for j in range(n_pages)]
            + [per_b(qbd), per_b(gate_t), pl.BlockSpec(trow.shape, lambda b, c, pt: (0, 0)),
               per_b(kc), per_b(vct), per_b(k_rows(2)), per_b(new[3]), win_spec(0), win_spec(1),
               per_b(k_rows(4)), per_b(new[5])],
            out_specs=pl.BlockSpec((1, HEAD_DIM, LANES), lambda b, c, pt: (b, 0, 0)),
            scratch_shapes=[pltpu.VMEM((nbp, LANES), F32), pltpu.VMEM((gd, LANES), F32), pltpu.VMEM((1, LANES), F32),
                            pltpu.VMEM((1, LANES), F32), pltpu.VMEM((gd, LANES), F32),
                            pltpu.VMEM((n_pages * PAGE_SIZE, LANES), F32)]),
        out_shape=jax.ShapeDtypeStruct((bsz, HEAD_DIM, LANES), F32),
        compiler_params=_params(("parallel", "arbitrary")),
        name="attn_sample",
    )(page_table, *([slc_pages] * n_pages), qbd, gate_t, trow, kc, vct, k_rows(2), new[3], win_t, win_t,
      k_rows(4), new[5])
    del smalls
    return o_t.reshape(bsz, HEAD_DIM, group, N_KV, n_new).transpose(0, 4, 3, 2, 1).reshape(bsz, n_new, nq)


def _tokens_last(x):
    nd = x.ndim
    return jnp.transpose(x, tuple(range(nd - 4)) + (nd - 3, nd - 2, nd - 1, nd - 4))


def _tokens_first(x):
    nd = x.ndim
    return jnp.transpose(x, tuple(range(nd - 4)) + (nd - 1, nd - 4, nd - 3, nd - 2))


def kernel(x_prompt, x_sample, cache_kv_cmp, cache_kv_slc, cache_win, state_conv, page_table, ln_g, ln_b, sg_w_in, sg_b_in, sg_ln_g, sg_ln_b, sg_w_s, sg_b_s, sg_w_out, sg_b_out, ffn_w_up, ffn_b_up, ffn_w_dw, ffn_b_dw, ffn_w_down, ffn_b_down, kv_w, cmp_pe, cmp_w1, cmp_b1, cmp_w2, cmp_b2, nsa_w_qg, nsa_b_qg, nsa_w_o, nsa_b_o, rel_bias):
    depth = ln_g.shape[0]
    n_a = sg_w_in.shape[0]
    assert depth == DEPTH
    bsz, seq, d_model = x_prompt.shape
    dbsz, n_new, _ = x_sample.shape
    d_ff = ffn_w_down.shape[1]
    kvd = (2, N_KV, HEAD_DIM)
    gd = N_KV * HEAD_DIM
    cmp_w = (cmp_pe, cmp_w1, cmp_b1, cmp_w2, cmp_b2)

    def trunk(x, seq_len, conv_state, attend_layer, emit_v):
        n_seq = x.shape[0] // seq_len
        new_conv, sg_v, kv_t, ctx = [], [], None, None
        for l in range(depth):
            if l < n_a:
                x, v = _sgu_layer(x, seq_len, sg_w_in[l], sg_b_in[l], sg_ln_g[l], sg_ln_b[l], sg_w_s[l], sg_b_s[l],
                                  sg_w_out[l], sg_b_out[l], ln_g[l, 0], ln_b[l, 0], emit_v)
                if emit_v:
                    sg_v.append(v.reshape(n_seq, seq_len, -1))
            else:
                i = l - n_a
                x = attend_layer(x, ctx, nsa_w_qg[i], nsa_b_qg[i], nsa_w_o[i], nsa_b_o[i], ln_g[l, 0], ln_b[l, 0])
            x, cs = _ffn_layer(x, seq_len, conv_state[l], ffn_w_up[l], ffn_b_up[l], ffn_w_dw[l], ffn_b_dw[l],
                               ffn_w_down[l], ffn_b_down[l], ln_g[l, 1], ln_b[l, 1])
            new_conv.append(cs)
            if l == n_a - 1:
                kv_t, ctx = attend_layer.prepare(x)
        return x, jnp.stack(new_conv), (jnp.stack(sg_v) if sg_v else None), kv_t

    bias_tiles = _bias_tiles(rel_bias)

    def prompt_prepare(x):
        kv_t, v_t, k_rows = _kv_proj(x, bsz, kv_w)
        kvc = _compress_pages(kv_t, None, *cmp_w)
        return kv_t, (kvc, k_rows, v_t)

    def prompt_attend(x, ctx, w_qg, b_qg, w_o, b_o, g, b):
        kvc, k_rows, v_t = ctx
        q_t, gates_t = _qg_proj_t(x, bsz, w_qg, b_qg)
        o_t = _attn_prompt(q_t, gates_t, kvc, k_rows, v_t, bias_tiles, rel_bias)
        return _oproj_t(o_t, x, w_o, b_o, g, b)

    prompt_attend.prepare = prompt_prepare
    zeros_state = jnp.zeros((depth, bsz, CONV_W - 1, d_ff), x_prompt.dtype)
    y_p, conv_p, _, kv_p = trunk(x_prompt.reshape(bsz * seq, d_model), seq, zeros_state, prompt_attend, False)
    kv_p = kv_p.reshape(3, bsz, 2, N_KV, HEAD_DIM, seq)

    cmp_pages = _tokens_last(cache_kv_cmp)
    slc_pages = _tokens_last(cache_kv_slc)
    win_t = _tokens_last(cache_win)

    def sample_prepare(x):
        kv_t, _, _ = _kv_proj(x, 1, kv_w)
        kv_t = kv_t.reshape(6 * gd, dbsz * n_new)
        kvc = _compress_pages(cmp_pages, page_table, *cmp_w)
        return kv_t, (kvc, kv_t)

    def sample_attend(x, ctx, w_qg, b_qg, w_o, b_o, g, b):
        kvc, kv_new_t = ctx
        q, gates = _qg_proj_flat(x, w_qg, b_qg)
        o = _attn_sample(q.reshape(dbsz, n_new, -1), gates.reshape(dbsz, n_new, -1), kvc, kv_new_t, slc_pages,
                         page_table, win_t, rel_bias)
        return _oproj_flat(o.reshape(dbsz * n_new, -1), x, w_o, b_o, g, b)

    sample_attend.prepare = sample_prepare
    assert n_new < BLK
    y_s, conv_s, sgv_s, kv_s = trunk(x_sample.reshape(dbsz * n_new, d_model), n_new, state_conv, sample_attend, True)
    kv_s = kv_s.reshape(3, 2, N_KV, HEAD_DIM, dbsz, n_new)

    win_len = min(WINDOW, seq)
    out_p = [_tokens_first(kv_p[s]) for s in range(3)]
    out_s = [jnp.transpose(kv_s[s], (3, 4, 0, 1, 2)) for s in range(3)]
    return (y_p.reshape(bsz, seq, d_model), y_s.reshape(dbsz, n_new, d_model),
            out_p[0], out_p[1], out_p[2][:, seq - win_len:], conv_p,
            out_s[0], out_s[1], out_s[2], conv_s, sgv_s)
```

```python
import functools
import math

import numpy as np
import jax
import jax.numpy as jnp
from jax import lax
from jax.experimental import pallas as pl
from jax.experimental.pallas import tpu as pltpu

N_SG_GROUPS = 8
CHUNK = 128
CONV_W = 3
HEAD_DIM = 64
N_KV = 4
BLK = 64
N_SEL = 16
WINDOW = 512
NUM_BUCKETS = 32
REL_MAX_DIST = 1024
Q_BLOCK = 128
PAGE_SIZE = 128
DEPTH = 2
ALPHA = (2 * DEPTH) ** 0.25
LN_EPS = 1e-5
NEG_INF = -1e30
FORCED_SCORE = 1e9
LOG2E = math.log2(math.e)

MXU_DTYPE = jnp.bfloat16
F32 = jnp.float32
LANES = 128
ROW_TILE = 256
SGU_ROW_TILE = 512
FFN_ROW_TILE = 512
VMEM_LIMIT = 56 * 1024 * 1024


def _bucket_thresholds():
    d = np.arange(0, 4 * REL_MAX_DIST)
    max_exact = NUM_BUCKETS // 2
    nf = np.maximum(d, 1).astype(np.float32)
    large = max_exact + (np.log(nf / np.float32(max_exact)) / np.float32(math.log(REL_MAX_DIST / max_exact))
                         * np.float32(NUM_BUCKETS - max_exact)).astype(np.int32)
    b = np.where(d < max_exact, d, np.minimum(large, NUM_BUCKETS - 1))
    assert np.all(np.diff(b) >= 0) and b[-1] == NUM_BUCKETS - 1
    return [int(np.nonzero(b >= j)[0][0]) for j in range(NUM_BUCKETS)]


BUCKET_THR = _bucket_thresholds()
FAR_DIST = BUCKET_THR[-1]
N_BIAS_TILES = -(-(FAR_DIST + LANES) // LANES)


def _gelu(x):
    return 0.5 * x * (1.0 + lax.erf(x * (1.0 / math.sqrt(2.0))))


def _layer_norm(x, g, b):
    mu = jnp.mean(x, axis=-1, keepdims=True)
    xc = x - mu
    var = jnp.mean(xc * xc, axis=-1, keepdims=True)
    return xc * lax.rsqrt(var + LN_EPS) * g + b


def _mm(a, b):
    return jnp.dot(a, b, preferred_element_type=F32)


def _mm_nt(a, b):
    return lax.dot_general(a, b, (((1,), (1,)), ((), ())), preferred_element_type=F32)


def _tn(a, b):
    return lax.dot_general(a, b, (((0,), (0,)), ((), ())), preferred_element_type=F32)


def _const_spec(shape):
    nd = len(shape)
    return pl.BlockSpec(shape, lambda *_: (0,) * nd, pipeline_mode=pl.Buffered(1))


def _params(sem):
    return pltpu.CompilerParams(dimension_semantics=sem, vmem_limit_bytes=VMEM_LIMIT)


def _bucket_of(dist):
    b = jnp.zeros(dist.shape, jnp.int32)
    for j in range(1, NUM_BUCKETS):
        b = b + (dist >= BUCKET_THR[j]).astype(jnp.int32)
    return b


def _sgu_kernel(x_ref, win_ref, bin_ref, gv_ref, bv_ref, wg_ref, bs_ref, wout_ref, bout_ref, lng_ref, lnb_ref,
                o_ref, *maybe_v_ref, d_sg):
    x = x_ref[...]
    xb = x.astype(MXU_DTYPE)
    v = _gelu(_mm(xb, win_ref[:, d_sg:]) + bin_ref[:, d_sg:])
    vn = _layer_norm(v, gv_ref[...], bv_ref[...])
    if maybe_v_ref:
        maybe_v_ref[0][...] = vn
    cg = d_sg // N_SG_GROUPS
    mix = wg_ref.shape[1]
    acc = jnp.zeros(o_ref.shape, F32)
    for g in range(N_SG_GROUPS):
        lo, hi = g * cg, (g + 1) * cg
        vg = vn[:, lo:hi].astype(MXU_DTYPE)
        gate = jnp.concatenate([_mm(wg_ref[g], vg[r:r + mix]) + bs_ref[g] for r in range(0, x.shape[0], mix)], axis=0)
        u = _gelu(_mm(xb, win_ref[:, lo:hi]) + bin_ref[:, lo:hi])
        acc = acc + _mm((u * gate).astype(MXU_DTYPE), wout_ref[lo:hi, :])
    y = acc + bout_ref[...]
    o_ref[...] = _layer_norm(ALPHA * x + y, lng_ref[...], lnb_ref[...])


def _sgu_layer(x, seq_len, w_in, b_in, g_v, b_v, w_s, b_s, w_out, b_out, ln_g, ln_b, emit_v):
    t, d = x.shape
    d_sg = w_out.shape[0]
    cg = d_sg // N_SG_GROUPS
    cl = min(CHUNK, seq_len)
    tm = min(SGU_ROW_TILE, t)
    mix = min(ROW_TILE, tm)
    assert seq_len % cl == 0 and mix % cl == 0 and tm % mix == 0 and t % tm == 0
    w_tri = jnp.tril(w_s[:, :cl, :cl])
    eye = jnp.eye(mix // cl, dtype=w_s.dtype)
    wg = jnp.einsum('ab,gts->gatbs', eye, w_tri).reshape(N_SG_GROUPS, mix, mix).astype(MXU_DTYPE)
    bs = jnp.broadcast_to(jnp.tile(b_s[:, :cl], (1, mix // cl))[:, :, None], (N_SG_GROUPS, mix, cg))
    row = lambda a: a.reshape(1, -1)
    out_specs = [pl.BlockSpec((tm, d), lambda i: (i, 0))]
    out_shape = [jax.ShapeDtypeStruct((t, d), F32)]
    if emit_v:
        out_specs.append(pl.BlockSpec((tm, d_sg), lambda i: (i, 0)))
        out_shape.append(jax.ShapeDtypeStruct((t, d_sg), F32))
    outs = pl.pallas_call(
        functools.partial(_sgu_kernel, d_sg=d_sg),
        grid=(t // tm,),
        in_specs=[pl.BlockSpec((tm, d), lambda i: (i, 0)),
                  _const_spec((d, 2 * d_sg)), _const_spec((1, 2 * d_sg)), _const_spec((1, d_sg)), _const_spec((1, d_sg)),
                  _const_spec((N_SG_GROUPS, mix, mix)), _const_spec((N_SG_GROUPS, mix, cg)),
                  _const_spec((d_sg, d)), _const_spec((1, d)), _const_spec((1, d)), _const_spec((1, d))],
        out_specs=out_specs,
        out_shape=out_shape,
        compiler_params=_params(("parallel",)),
        name="sgu_layer",
    )(x, w_in.astype(MXU_DTYPE), row(b_in), row(g_v), row(b_v), wg, bs, w_out.astype(MXU_DTYPE), row(b_out),
      row(ln_g), row(ln_b))
    return outs[0], (outs[1] if emit_v else None)


def _ffn_tail(x, a, gt, s1, s2, wdw_ref, bdw_ref, wdown_ref, bdown_ref, lng_ref, lnb_ref, o_ref):
    c = bdw_ref[...] + s2 * wdw_ref[0:1, :] + s1 * wdw_ref[1:2, :] + a * wdw_ref[2:3, :]
    h = (_gelu(c) * gt).astype(MXU_DTYPE)
    y = _mm(h, wdown_ref[...]) + bdown_ref[...]
    o_ref[...] = _layer_norm(ALPHA * x + y, lng_ref[...], lnb_ref[...])


def _ffn_long_kernel(x_ref, xh_ref, st_ref, wup_ref, bup_ref, wdw_ref, bdw_ref, wdown_ref, bdown_ref, lng_ref, lnb_ref,
                     o_ref, conv_ref, *, seq_len, d_ff):
    tm = x_ref.shape[0]
    x = x_ref[...]
    xb = x.astype(MXU_DTYPE)
    a = _mm(xb, wup_ref[:, :d_ff]) + bup_ref[:, :d_ff]
    gt = _mm(xb, wup_ref[:, d_ff:]) + bup_ref[:, d_ff:]
    ah = _mm(xh_ref[...].astype(MXU_DTYPE), wup_ref[:, :d_ff]) + bup_ref[:, :d_ff]
    at_start = (pl.program_id(0) * tm) % seq_len == 0
    p1 = jnp.where(at_start, st_ref[0, 1:2, :], ah[7:8, :])
    p0 = jnp.where(at_start, st_ref[0, 0:1, :], ah[6:7, :])
    r = lax.broadcasted_iota(jnp.int32, a.shape, 0)
    s1 = jnp.where(r == 0, p1, pltpu.roll(a, 1, 0))
    s2 = jnp.where(r == 0, p0, jnp.where(r == 1, p1, pltpu.roll(a, 2, 0)))
    conv_ref[0] = a[tm - (CONV_W - 1):, :]
    _ffn_tail(x, a, gt, s1, s2, wdw_ref, bdw_ref, wdown_ref, bdown_ref, lng_ref, lnb_ref, o_ref)


def _ffn_short_kernel(x_ref, p0_ref, p1_ref, wup_ref, bup_ref, wdw_ref, bdw_ref, wdown_ref, bdown_ref, lng_ref, lnb_ref,
                      o_ref, a_ref, *, seq_len, d_ff):
    x = x_ref[...]
    xb = x.astype(MXU_DTYPE)
    a = _mm(xb, wup_ref[:, :d_ff]) + bup_ref[:, :d_ff]
    gt = _mm(xb, wup_ref[:, d_ff:]) + bup_ref[:, d_ff:]
    pos = lax.broadcasted_iota(jnp.int32, a.shape, 0) % seq_len
    s1 = jnp.where(pos == 0, p1_ref[...], pltpu.roll(a, 1, 0))
    s2 = jnp.where(pos == 0, p0_ref[...], jnp.where(pos == 1, p1_ref[...], pltpu.roll(a, 2, 0)))
    a_ref[...] = a
    _ffn_tail(x, a, gt, s1, s2, wdw_ref, bdw_ref, wdown_ref, bdown_ref, lng_ref, lnb_ref, o_ref)


def _ffn_layer(x, seq_len, conv_state, w_up, b_up, w_dw, b_dw, w_down, b_down, ln_g, ln_b):
    t, d = x.shape
    d_ff = w_down.shape[0]
    n_seq = t // seq_len
    tm = min(FFN_ROW_TILE, t)
    assert seq_len >= CONV_W - 1 and t % tm == 0
    row = lambda a: a.reshape(1, -1)
    weights = (w_up.astype(MXU_DTYPE), row(b_up), w_dw, row(b_dw), w_down.astype(MXU_DTYPE), row(b_down), row(ln_g), row(ln_b))
    w_specs = [_const_spec((d, 2 * d_ff)), _const_spec((1, 2 * d_ff)), _const_spec((CONV_W, d_ff)), _const_spec((1, d_ff)),
               _const_spec((d_ff, d)), _const_spec((1, d)), _const_spec((1, d)), _const_spec((1, d))]
    if seq_len % tm == 0:
        tiles_per_seq = seq_len // tm
        out, conv = pl.pallas_call(
            functools.partial(_ffn_long_kernel, seq_len=seq_len, d_ff=d_ff),
            grid=(t // tm,),
            in_specs=[pl.BlockSpec((tm, d), lambda i: (i, 0)),
                      pl.BlockSpec((8, d), lambda i: (jnp.maximum(i * (tm // 8) - 1, 0), 0)),
                      pl.BlockSpec((1, CONV_W - 1, d_ff), lambda i: (i // tiles_per_seq, 0, 0))] + w_specs,
            out_specs=[pl.BlockSpec((tm, d), lambda i: (i, 0)),
                       pl.BlockSpec((1, CONV_W - 1, d_ff), lambda i: (i // tiles_per_seq, 0, 0))],
            out_shape=[jax.ShapeDtypeStruct((t, d), F32), jax.ShapeDtypeStruct((n_seq, CONV_W - 1, d_ff), F32)],
            compiler_params=_params(("arbitrary",)),
            name="ffn_long",
        )(x, x, conv_state, *weights)
        return out, conv
    assert tm % seq_len == 0
    p0 = jnp.repeat(conv_state[:, 0, :], seq_len, axis=0)
    p1 = jnp.repeat(conv_state[:, 1, :], seq_len, axis=0)
    out, a = pl.pallas_call(
        functools.partial(_ffn_short_kernel, seq_len=seq_len, d_ff=d_ff),
        grid=(t // tm,),
        in_specs=[pl.BlockSpec((tm, d), lambda i: (i, 0)), pl.BlockSpec((tm, d_ff), lambda i: (i, 0)),
                  pl.BlockSpec((tm, d_ff), lambda i: (i, 0))] + w_specs,
        out_specs=[pl.BlockSpec((tm, d), lambda i: (i, 0)), pl.BlockSpec((tm, d_ff), lambda i: (i, 0))],
        out_shape=[jax.ShapeDtypeStruct((t, d), F32), jax.ShapeDtypeStruct((t, d_ff), F32)],
        compiler_params=_params(("parallel",)),
        name="ffn_short",
    )(x, p0, p1, *weights)
    return out, a.reshape(n_seq, seq_len, d_ff)[:, seq_len - (CONV_W - 1):, :]


STEP_SUBS = 4
STEP_BLKS = STEP_SUBS * LANES // BLK


def _kv_kernel(x_ref, wt_ref, wk_ref, kv_ref, vt_ref, k_ref, *, tiles):
    tm = x_ref.shape[0]
    xb = x_ref[...].astype(MXU_DTYPE)
    r = _mm_nt(wt_ref[...], xb)
    gd = N_KV * HEAD_DIM
    for pair in range(3):
        kv_ref[pair, 0] = r[pair * 2 * gd:(pair + 1) * 2 * gd, :]
    pos = (pl.program_id(0) % tiles) * tm + lax.broadcasted_iota(jnp.int32, (tm, LANES), 0)
    lane = lax.broadcasted_iota(jnp.int32, (tm, LANES), 1)
    onehot = jnp.where((pos // BLK) % STEP_BLKS + HEAD_DIM == lane, 1.0, 0.0)
    for s in range(2):
        vt_ref[0, s] = r[(2 * s + 3) * gd:(2 * s + 4) * gd, :].astype(MXU_DTYPE)
        for g in range(N_KV):
            k_ref[0, s, g] = (_mm(xb, wk_ref[s, g]) + onehot).astype(MXU_DTYPE)


def _kv_proj(x, n_seq, kv_w):
    t, d = x.shape
    seq_len = t // n_seq
    n_out = kv_w.shape[1]
    gd = N_KV * HEAD_DIM
    tm = ROW_TILE
    assert seq_len % tm == 0 and n_out == 6 * gd and HEAD_DIM + STEP_BLKS <= LANES
    tiles = seq_len // tm
    wk = kv_w.reshape(d, 3, 2, N_KV, HEAD_DIM)[:, 1:, 0].transpose(1, 2, 0, 3)
    wk = jnp.pad(wk, ((0, 0), (0, 0), (0, 0), (0, LANES - HEAD_DIM))).astype(MXU_DTYPE)
    return pl.pallas_call(
        functools.partial(_kv_kernel, tiles=tiles),
        grid=(t // tm,),
        in_specs=[pl.BlockSpec((tm, d), lambda i: (i, 0)), _const_spec((n_out, d)), _const_spec(wk.shape)],
        out_specs=[pl.BlockSpec((3, 1, 2 * gd, tm), lambda i: (0, i // tiles, 0, i % tiles)),
                   pl.BlockSpec((1, 2, gd, tm), lambda i: (i // tiles, 0, 0, i % tiles)),
                   pl.BlockSpec((1, 2, N_KV, tm, LANES), lambda i: (i // tiles, 0, 0, i % tiles, 0))],
        out_shape=[jax.ShapeDtypeStruct((3, n_seq, 2 * gd, seq_len), F32),
                   jax.ShapeDtypeStruct((n_seq, 2, gd, seq_len), MXU_DTYPE),
                   jax.ShapeDtypeStruct((n_seq, 2, N_KV, seq_len, LANES), MXU_DTYPE)],
        compiler_params=_params(("parallel",)),
        name="kv_proj",
    )(x, kv_w.T.astype(MXU_DTYPE), wk)


def _sigmoid(x):
    return 1.0 / (1.0 + jnp.exp(-x))


GATE_ROWS = 16


def _qg_t_kernel(x_ref, wq_ref, bq_ref, wg_ref, bg_ref, q_ref, gate_ref):
    xb = x_ref[...].astype(MXU_DTYPE)
    q_ref[0] = ((_mm_nt(wq_ref[...], xb) + bq_ref[...]) * (HEAD_DIM ** -0.5 * LOG2E)).astype(MXU_DTYPE)
    gate_ref[0] = _sigmoid(_mm_nt(wg_ref[...], xb) + bg_ref[...])


def _qg_proj_t(x, n_seq, w_qg, b_qg):
    t, d = x.shape
    seq_len = t // n_seq
    n_heads = (w_qg.shape[1] // (HEAD_DIM + 3))
    group = n_heads // N_KV
    nq = n_heads * HEAD_DIM
    tm = ROW_TILE
    assert seq_len % tm == 0 and 3 * group <= GATE_ROWS
    tiles = seq_len // tm
    pad = GATE_ROWS - 3 * group
    wg = jnp.pad(w_qg[:, nq:].T.reshape(N_KV, 3 * group, d), ((0, 0), (0, pad), (0, 0))).reshape(N_KV * GATE_ROWS, d)
    bg = jnp.pad(b_qg[nq:].reshape(N_KV, 3 * group), ((0, 0), (0, pad))).reshape(N_KV * GATE_ROWS, 1)
    bg = jnp.broadcast_to(bg, (N_KV * GATE_ROWS, tm))
    bq = jnp.broadcast_to(b_qg[:nq].reshape(nq, 1), (nq, tm))
    return pl.pallas_call(
        _qg_t_kernel,
        grid=(t // tm,),
        in_specs=[pl.BlockSpec((tm, d), lambda i: (i, 0)), _const_spec((nq, d)), _const_spec((nq, tm)),
                  _const_spec(wg.shape), _const_spec(bg.shape)],
        out_specs=[pl.BlockSpec((1, nq, tm), lambda i: (i // tiles, 0, i % tiles)),
                   pl.BlockSpec((1, N_KV * GATE_ROWS, tm), lambda i: (i // tiles, 0, i % tiles))],
        out_shape=[jax.ShapeDtypeStruct((n_seq, nq, seq_len), MXU_DTYPE),
                   jax.ShapeDtypeStruct((n_seq, N_KV * GATE_ROWS, seq_len), F32)],
        compiler_params=_params(("parallel",)),
        name="qg_proj_t",
    )(x, w_qg[:, :nq].T.astype(MXU_DTYPE), bq, wg.astype(MXU_DTYPE), bg)


def _qg_flat_kernel(x_ref, wq_ref, bq_ref, wg_ref, bg_ref, q_ref, gate_ref):
    xb = x_ref[...].astype(MXU_DTYPE)
    q_ref[...] = (_mm(xb, wq_ref[...]) + bq_ref[...]) * (HEAD_DIM ** -0.5)
    gate_ref[...] = _sigmoid(_mm(xb, wg_ref[...]) + bg_ref[...])


def _qg_proj_flat(x, w_qg, b_qg):
    t, d = x.shape
    n_gate = w_qg.shape[1] // (HEAD_DIM + 3) * 3
    nq = w_qg.shape[1] - n_gate
    tm = ROW_TILE
    assert t % tm == 0 and n_gate <= LANES
    wg = jnp.pad(w_qg[:, nq:], ((0, 0), (0, LANES - n_gate))).astype(MXU_DTYPE)
    bg = jnp.pad(b_qg[nq:], (0, LANES - n_gate)).reshape(1, LANES)
    return pl.pallas_call(
        _qg_flat_kernel,
        grid=(t // tm,),
        in_specs=[pl.BlockSpec((tm, d), lambda i: (i, 0)), _const_spec((d, nq)), _const_spec((1, nq)),
                  _const_spec((d, LANES)), _const_spec((1, LANES))],
        out_specs=[pl.BlockSpec((tm, nq), lambda i: (i, 0)), pl.BlockSpec((tm, LANES), lambda i: (i, 0))],
        out_shape=[jax.ShapeDtypeStruct((t, nq), F32), jax.ShapeDtypeStruct((t, LANES), F32)],
        compiler_params=_params(("parallel",)),
        name="qg_proj_flat",
    )(x, w_qg[:, :nq].astype(MXU_DTYPE), b_qg[:nq].reshape(1, nq), wg, bg)


def _oproj_t_kernel(o_ref, x_ref, wo_ref, bo_ref, lng_ref, lnb_ref, y_ref):
    y = _tn(o_ref[0], wo_ref[...]) + bo_ref[...]
    y_ref[...] = _layer_norm(ALPHA * x_ref[...] + y, lng_ref[...], lnb_ref[...])


def _oproj_t(o_t, x, w_o, b_o, ln_g, ln_b):
    n_seq, nq, seq_len = o_t.shape
    t, d = x.shape
    tm = ROW_TILE
    tiles = seq_len // tm
    row = lambda a: a.reshape(1, -1)
    return pl.pallas_call(
        _oproj_t_kernel,
        grid=(t // tm,),
        in_specs=[pl.BlockSpec((1, nq, tm), lambda i: (i // tiles, 0, i % tiles)),
                  pl.BlockSpec((tm, d), lambda i: (i, 0)), _const_spec((nq, d)),
                  _const_spec((1, d)), _const_spec((1, d)), _const_spec((1, d))],
        out_specs=pl.BlockSpec((tm, d), lambda i: (i, 0)),
        out_shape=jax.ShapeDtypeStruct((t, d), F32),
        compiler_params=_params(("parallel",)),
        name="oproj_t",
    )(o_t, x, w_o.astype(MXU_DTYPE), row(b_o), row(ln_g), row(ln_b))


def _oproj_flat_kernel(o_ref, x_ref, wo_ref, bo_ref, lng_ref, lnb_ref, y_ref):
    y = _mm(o_ref[...].astype(MXU_DTYPE), wo_ref[...]) + bo_ref[...]
    y_ref[...] = _layer_norm(ALPHA * x_ref[...] + y, lng_ref[...], lnb_ref[...])


def _oproj_flat(o, x, w_o, b_o, ln_g, ln_b):
    t, d = x.shape
    tm = ROW_TILE
    row = lambda a: a.reshape(1, -1)
    return pl.pallas_call(
        _oproj_flat_kernel,
        grid=(t // tm,),
        in_specs=[pl.BlockSpec((tm, o.shape[1]), lambda i: (i, 0)), pl.BlockSpec((tm, d), lambda i: (i, 0)),
                  _const_spec(w_o.shape), _const_spec((1, d)), _const_spec((1, d)), _const_spec((1, d))],
        out_specs=pl.BlockSpec((tm, d), lambda i: (i, 0)),
        out_shape=jax.ShapeDtypeStruct((t, d), F32),
        compiler_params=_params(("parallel",)),
        name="oproj_flat",
    )(o, x, w_o.astype(MXU_DTYPE), row(b_o), row(ln_g), row(ln_b))


def _bias_tile_kernel(rel_ref, o_ref):
    h = pl.program_id(0)
    far = rel_ref[NUM_BUCKETS - 1, h]
    key = lax.broadcasted_iota(jnp.int32, (LANES, Q_BLOCK), 0)
    qry = lax.broadcasted_iota(jnp.int32, (LANES, Q_BLOCK), 1)
    for dl in range(o_ref.shape[1]):
        dist = dl * LANES + qry - key
        val = jnp.full((LANES, Q_BLOCK), (rel_ref[0, h] - far) * LOG2E, F32)
        for j in range(1, NUM_BUCKETS):
            val = jnp.where(dist >= BUCKET_THR[j], (rel_ref[j, h] - far) * LOG2E, val)
        o_ref[0, dl] = val


def _bias_tiles(rel_bias):
    n_heads = rel_bias.shape[1]
    return pl.pallas_call(
        _bias_tile_kernel,
        grid=(n_heads,),
        in_specs=[pl.BlockSpec(memory_space=pltpu.SMEM)],
        out_specs=pl.BlockSpec((1, N_BIAS_TILES + 1, LANES, Q_BLOCK), lambda h: (h, 0, 0, 0)),
        out_shape=jax.ShapeDtypeStruct((n_heads, N_BIAS_TILES + 1, LANES, Q_BLOCK), F32),
        compiler_params=_params(("parallel",)),
        name="bias_tiles",
    )(rel_bias)


CMP_PAGES = 32
PE_ROWS = 8


def _compress_kernel(pt_ref, pages_ref, pe_ref, w1_ref, b1_ref, w2_ref, b2_ref, o_ref, buf_ref, sem_ref,
                     *, n_pages, paged):
    b, c = pl.program_id(0), pl.program_id(1)
    n_chunks = pl.num_programs(1)
    step = b * n_chunks + c
    slot = step % 2
    page_rows = buf_ref.shape[1]

    def page_copy(sb, sc, j, sl):
        page = pt_ref[sb, sc * n_pages + j]
        if paged:
            src = pages_ref.at[page]
        else:
            src = pages_ref.at[0, sb, :, pl.ds(pl.multiple_of(page * PAGE_SIZE, PAGE_SIZE), PAGE_SIZE)]
        return pltpu.make_async_copy(src, buf_ref.at[sl, :, j, :], sem_ref.at[sl])

    @pl.when(step == 0)
    def _():
        for j in range(n_pages):
            page_copy(b, c, j, slot).start()

    @pl.when(step + 1 < pl.num_programs(0) * n_chunks)
    def _():
        nxt_c = jnp.where(c + 1 == n_chunks, 0, c + 1)
        nxt_b = jnp.where(c + 1 == n_chunks, b + 1, b)
        for j in range(n_pages):
            page_copy(nxt_b, nxt_c, j, 1 - slot).start()

    for j in range(n_pages):
        page_copy(b, c, j, slot).wait()
    rows = N_KV * n_pages

    def head_dim_row(k, g, d):
        return buf_ref[slot, (k * N_KV + g) * HEAD_DIM + d]

    for k in range(2):
        acc = jnp.zeros((rows + PE_ROWS, 2 * LANES), F32)
        for dd in range(HEAD_DIM // 2):
            parts = [jnp.concatenate([head_dim_row(k, g, 2 * dd), head_dim_row(k, g, 2 * dd + 1)], axis=1)
                     for g in range(N_KV)]
            parts.append(pe_ref[k, dd])
            acc = acc + _mm(jnp.concatenate(parts, axis=0).astype(MXU_DTYPE), w1_ref[k, dd])
        pre = acc[:rows] + acc[rows:rows + 1] + b1_ref[k]
        o_ref[0, 0, k] = _mm(_gelu(pre).astype(MXU_DTYPE), w2_ref[k]) + b2_ref[k]


def _compress_pages(pages, page_table, cmp_pe, cmp_w1, cmp_b1, cmp_w2, cmp_b2, raw_layout=False):
    paged = page_table is not None
    if paged:
        pages = pages.reshape(pages.shape[0], 2 * N_KV * HEAD_DIM, PAGE_SIZE)
    else:
        assert pages.shape[2] == 2 * N_KV * HEAD_DIM
        page_table = jnp.broadcast_to(jnp.arange(pages.shape[3] // PAGE_SIZE, dtype=jnp.int32)[None],
                                      (pages.shape[1], pages.shape[3] // PAGE_SIZE))
    n_seq, pps = page_table.shape
    bpp = PAGE_SIZE // BLK
    hid = cmp_w1.shape[-1]
    n_pages = min(CMP_PAGES, pps)
    page_rows = 2 * N_KV * HEAD_DIM
    assert pps % n_pages == 0 and n_pages % 8 == 0 and bpp * hid == 2 * LANES and bpp * HEAD_DIM == LANES
    n_chunks = pps // n_pages
    w1r = cmp_w1.reshape(2, BLK, HEAD_DIM // 2, 2, hid)
    eye = jnp.eye(bpp, dtype=cmp_w1.dtype)
    w1bd = jnp.einsum('kldeh,nm->kdenlmh', w1r, eye).reshape(2, HEAD_DIM // 2, 2 * PAGE_SIZE, bpp * hid).astype(MXU_DTYPE)
    pe_t = jnp.tile(jnp.swapaxes(cmp_pe, 1, 2), (1, 1, bpp))
    pe_rows = jnp.pad(pe_t.reshape(2, HEAD_DIM // 2, 1, 2 * PAGE_SIZE), ((0, 0), (0, 0), (0, PE_ROWS - 1), (0, 0)))
    b1 = jnp.tile(cmp_b1, (1, bpp)).reshape(2, 1, bpp * hid)
    w2bd = jnp.einsum('khd,nm->knhmd', cmp_w2, eye).reshape(2, bpp * hid, bpp * HEAD_DIM).astype(MXU_DTYPE)
    b2 = jnp.tile(cmp_b2, (1, bpp)).reshape(2, 1, bpp * HEAD_DIM)
    rows = N_KV * n_pages
    const = lambda a: pl.BlockSpec(a.shape, lambda b, c, pt: (0,) * a.ndim, pipeline_mode=pl.Buffered(1))
    out = pl.pallas_call(
        functools.partial(_compress_kernel, n_pages=n_pages, paged=paged),
        grid_spec=pltpu.PrefetchScalarGridSpec(
            num_scalar_prefetch=1,
            grid=(n_seq, n_chunks),
            in_specs=[pl.BlockSpec(memory_space=pl.ANY)] + [const(a) for a in (pe_rows, w1bd, b1, w2bd, b2)],
            out_specs=pl.BlockSpec((1, 1, 2, rows, bpp * HEAD_DIM), lambda b, c, pt: (b, c, 0, 0, 0)),
            scratch_shapes=[pltpu.VMEM((2, page_rows, n_pages, PAGE_SIZE), F32), pltpu.SemaphoreType.DMA((2,))]),
        out_shape=jax.ShapeDtypeStruct((n_seq, n_chunks, 2, rows, bpp * HEAD_DIM), F32),
        compiler_params=_params(("arbitrary", "arbitrary")),
        name="compress_pages",
    )(page_table, pages, pe_rows, w1bd, b1, w2bd, b2)
    if raw_layout:
        return out
    out = out.reshape(n_seq, n_chunks, 2, N_KV, n_pages, bpp, HEAD_DIM)
    return out.transpose(0, 2, 3, 1, 4, 5, 6).reshape(n_seq, 2, N_KV, pps * bpp, HEAD_DIM)


WIN_SUBS = WINDOW // LANES + 1
CMP_LOOKBACK = (FAR_DIST + BLK - 1) // BLK + 1
CMP_BAND = -(-(CMP_LOOKBACK + Q_BLOCK // BLK + 6) // 8) * 8


def _attn_prompt_kernel(rel_ref, qt_ref, gate_ref, kc_ref, vct_ref, sk_ref, svt_ref, wk_ref, wvt_ref, tb_ref, o_ref,
                        sel_ref, lc_ref, s_ref, p_ref, *, group, n_cmp, n_sub):
    g = pl.program_id(1)
    i = pl.program_id(2)
    qt = jnp.concatenate([qt_ref[0, r] for r in range(group)], axis=1)
    lanes = lambda x, r: x[:, r * Q_BLOCK:(r + 1) * Q_BLOCK]

    nl = group * Q_BLOCK
    key = lax.broadcasted_iota(jnp.int32, (LANES, Q_BLOCK), 0)
    qpos = i * Q_BLOCK + lax.broadcasted_iota(jnp.int32, (LANES, Q_BLOCK), 1)
    zero_rows = jnp.zeros((LANES - HEAD_DIM, nl), MXU_DTYPE)
    qt_plain = jnp.concatenate([qt, zero_rows], axis=0)

    def sub_keys(k_ref, t):
        off = pl.multiple_of(jnp.clip(t, 0, n_sub - 1) * LANES, LANES)
        return k_ref[0, 0, 0, pl.ds(off, LANES), :]

    def with_bias(s, dl, madd=None):
        cols = [lanes(s, r) + tb_ref[r, dl] for r in range(group)]
        if madd is not None:
            cols = [c + madd for c in cols]
        return jnp.concatenate(cols, axis=1)

    lc_ref[...] = _mm(kc_ref[0, 0], qt)
    c0 = pl.multiple_of(jnp.clip((i * (Q_BLOCK // BLK) - CMP_LOOKBACK) // 8 * 8, 0, n_cmp - CMP_BAND), 8)
    row_b = c0 + lax.broadcasted_iota(jnp.int32, (CMP_BAND, Q_BLOCK), 0)
    dist_b = i * Q_BLOCK + lax.broadcasted_iota(jnp.int32, (CMP_BAND, Q_BLOCK), 1) - (row_b * BLK + BLK - 1)
    bkt = _bucket_of(dist_b)
    band = []
    for r in range(group):
        h = g * group + r
        far = rel_ref[NUM_BUCKETS - 1, h]
        bias = jnp.full((CMP_BAND, Q_BLOCK), (rel_ref[0, h] - far) * LOG2E, F32)
        for j in range(1, NUM_BUCKETS - 1):
            bias = jnp.where(bkt == j, (rel_ref[j, h] - far) * LOG2E, bias)
        band.append(jnp.where(bkt == NUM_BUCKETS - 1, 0.0, bias))
    lc_ref[pl.ds(c0, CMP_BAND), :] = lc_ref[pl.ds(c0, CMP_BAND), :] + jnp.concatenate(band, axis=1)
    blk = lax.broadcasted_iota(jnp.int32, (n_cmp, Q_BLOCK), 0)
    qpos_c = i * Q_BLOCK + lax.broadcasted_iota(jnp.int32, (n_cmp, Q_BLOCK), 1)
    ok_c = qpos_c >= blk * BLK + BLK - 1
    ok_f = ok_c.astype(F32)
    imp = jnp.zeros((n_cmp, Q_BLOCK), F32)
    pcs = []
    for r in range(group):
        lg = jnp.where(ok_c, lc_ref[:, r * Q_BLOCK:(r + 1) * Q_BLOCK], NEG_INF)
        p = jnp.exp2(lg - jnp.max(lg, axis=0, keepdims=True)) * ok_f
        pc = p * (1.0 / jnp.maximum(jnp.sum(p, axis=0, keepdims=True), 1e-30))
        imp = imp + pc
        pcs.append(pc.astype(MXU_DTYPE))
    o_c = _mm(vct_ref[0, 0], jnp.concatenate(pcs, axis=1))

    parts = []
    for u in range(WIN_SUBS):
        t = i - (WIN_SUBS - 1) + u
        dw = qpos - (t * LANES + key)
        madd = jnp.where((dw >= 0) & (dw < WINDOW) & (t >= 0), 0.0, NEG_INF)
        parts.append(with_bias(_mm(sub_keys(wk_ref, t), qt_plain), WIN_SUBS - 1 - u, madd))
    sw = jnp.concatenate(parts, axis=0)
    pw = jnp.exp2(sw - jnp.max(sw, axis=0, keepdims=True))
    acc_w = jnp.zeros((HEAD_DIM, nl), F32)
    for u in range(WIN_SUBS):
        off = pl.multiple_of(jnp.maximum(i - (WIN_SUBS - 1) + u, 0) * LANES, LANES)
        acc_w = acc_w + _mm(wvt_ref[0, 0, 0, :, pl.ds(off, LANES)], pw[u * LANES:(u + 1) * LANES].astype(MXU_DTYPE))
    o_w = acc_w * (1.0 / jnp.sum(pw, axis=0, keepdims=True))

    cur = qpos_c // BLK
    forced = (blk == 0) | (blk == cur) | (blk == cur - 1)
    score = jnp.where(forced, FORCED_SCORE, jnp.where(blk <= cur, imp, -1.0))
    sel = jnp.zeros((n_cmp, Q_BLOCK), F32)
    for _ in range(min(N_SEL, n_cmp)):
        mx = jnp.max(score, axis=0, keepdims=True)
        idx = jnp.min(jnp.where(score == mx, blk, n_cmp), axis=0, keepdims=True)
        hit = blk == idx
        sel = jnp.where(hit, 1.0, sel)
        score = jnp.where(hit, -2.0, score)
    sel_ref[...] = (sel - 1.0) * (-NEG_INF)

    def step_logits(step):
        mrows = sel_ref[pl.ds(pl.multiple_of(step * STEP_BLKS, STEP_BLKS), STEP_BLKS), :]
        mrows = jnp.concatenate([mrows] * group, axis=1)
        pad = jnp.zeros((LANES - HEAD_DIM - STEP_BLKS, nl), F32)
        q_aug = jnp.concatenate([qt, jnp.concatenate([mrows, pad], axis=0).astype(MXU_DTYPE)], axis=0)
        return [_mm(sub_keys(sk_ref, step * STEP_SUBS + u), q_aug) for u in range(STEP_SUBS)]

    step_keys = STEP_SUBS * LANES

    def qk_store(step, kind):
        cm = jnp.full((1, nl), NEG_INF, F32)
        for u, s in enumerate(step_logits(step)):
            t = step * STEP_SUBS + u
            if kind == 'near':
                s = with_bias(s, jnp.clip(i - t, 0, N_BIAS_TILES))
            elif kind == 'diag':
                s = with_bias(s, jnp.clip(i - t, 0, N_BIAS_TILES), jnp.where(t * LANES + key <= qpos, 0.0, NEG_INF))
            s_ref[pl.ds(pl.multiple_of(t * LANES, LANES), LANES), :] = s
            cm = jnp.maximum(cm, jnp.max(s, axis=0, keepdims=True))
        return cm

    def values(step):
        off = pl.multiple_of(jnp.maximum(step, 0) * step_keys, step_keys)
        return svt_ref[0, 0, 0, :, pl.ds(off, step_keys)]

    def softmax_step(step, m, l, cm):
        m_new = jnp.maximum(m, cm)
        a = jnp.exp2(m - m_new)
        p = jnp.exp2(s_ref[pl.ds(pl.multiple_of(step * step_keys, step_keys), step_keys), :] - m_new)
        return m_new, a * l + jnp.sum(p, axis=0, keepdims=True), a, p.astype(MXU_DTYPE)

    def pipe_step(j, carry, kind):
        m, l, acc, cm, a_prev = carry
        acc = a_prev * acc + _mm(values(j - 1), p_ref[...])
        m, l, a, p = softmax_step(j, m, l, cm)
        p_ref[...] = p
        return m, l, acc, qk_store(j + 1, kind), a

    last = i // STEP_SUBS
    n_far = jnp.maximum((i - (N_BIAS_TILES + STEP_SUBS - 1)) // STEP_SUBS + 1, 0)
    p_ref[...] = jnp.zeros(p_ref.shape, p_ref.dtype)
    carry = (jnp.full((1, nl), NEG_INF, F32), jnp.zeros((1, nl), F32), jnp.zeros((HEAD_DIM, nl), F32),
             qk_store(0, 'diag'), jnp.ones((1, nl), F32))
    n_a = jnp.maximum(n_far - 1, 0)
    carry = lax.fori_loop(0, n_a, functools.partial(pipe_step, kind='far'), carry)
    carry = lax.fori_loop(n_a, last - 1, functools.partial(pipe_step, kind='near'), carry)
    m, l_s, acc_s, cm, a_prev = lax.fori_loop(jnp.maximum(last - 1, 0), last,
                                              functools.partial(pipe_step, kind='diag'), carry)
    acc_s = a_prev * acc_s + _mm(values(last - 1), p_ref[...])
    _, l_s, a, p = softmax_step(last, m, l_s, cm)
    acc_s = a * acc_s + _mm(values(last), p)
    o_s = acc_s * (1.0 / l_s)

    gt = gate_ref[0]
    for r in range(group):
        o = (gt[3 * r:3 * r + 1, :] * lanes(o_c, r) + gt[3 * r + 1:3 * r + 2, :] * lanes(o_s, r)
             + gt[3 * r + 2:3 * r + 3, :] * lanes(o_w, r))
        o_ref[0, r] = o.astype(o_ref.dtype)


def _attn_prompt(q_t, gates_t, kvc, k_rows, v_t, bias_tiles, rel_bias):
    bsz, nq, seq = q_t.shape
    n_heads = nq // HEAD_DIM
    group = n_heads // N_KV
    n_cmp = kvc.shape[3]
    n_sub = seq // LANES
    assert seq % Q_BLOCK == 0 and Q_BLOCK == LANES and n_cmp * BLK == seq and n_sub % STEP_SUBS == 0
    assert WIN_SUBS - 1 <= N_BIAS_TILES and n_cmp >= CMP_BAND and n_cmp % 8 == 0
    kc = kvc[:, 0].astype(MXU_DTYPE)
    vct = jnp.swapaxes(kvc[:, 1], -1, -2).astype(MXU_DTYPE)
    q4 = q_t.reshape(bsz, n_heads, HEAD_DIM, seq)
    v4 = v_t.reshape(bsz, 2, N_KV, HEAD_DIM, seq)
    k_spec = lambda s: pl.BlockSpec((1, 1, 1, seq, LANES), lambda b, g, i: (b, s, g, 0, 0))
    v_spec = lambda s: pl.BlockSpec((1, 1, 1, HEAD_DIM, seq), lambda b, g, i: (b, s, g, 0, 0))
    o = pl.pallas_call(
        functools.partial(_attn_prompt_kernel, group=group, n_cmp=n_cmp, n_sub=n_sub),
        grid=(bsz, N_KV, seq // Q_BLOCK),
        in_specs=[pl.BlockSpec(memory_space=pltpu.SMEM),
                  pl.BlockSpec((1, group, HEAD_DIM, Q_BLOCK), lambda b, g, i: (b, g, 0, i)),
                  pl.BlockSpec((1, GATE_ROWS, Q_BLOCK), lambda b, g, i: (b, g, i)),
                  pl.BlockSpec((1, 1, n_cmp, HEAD_DIM), lambda b, g, i: (b, g, 0, 0)),
                  pl.BlockSpec((1, 1, HEAD_DIM, n_cmp), lambda b, g, i: (b, g, 0, 0)),
                  k_spec(0), v_spec(0), k_spec(1), v_spec(1),
                  pl.BlockSpec((group, N_BIAS_TILES + 1, LANES, Q_BLOCK), lambda b, g, i: (g, 0, 0, 0))],
        out_specs=pl.BlockSpec((1, group, HEAD_DIM, Q_BLOCK), lambda b, g, i: (b, g, 0, i)),
        out_shape=jax.ShapeDtypeStruct((bsz, n_heads, HEAD_DIM, seq), MXU_DTYPE),
        scratch_shapes=[pltpu.VMEM((n_cmp, Q_BLOCK), F32), pltpu.VMEM((n_cmp, group * Q_BLOCK), F32),
                        pltpu.VMEM((seq, group * Q_BLOCK), F32),
                        pltpu.VMEM((STEP_SUBS * LANES, group * Q_BLOCK), MXU_DTYPE)],
        compiler_params=_params(("parallel", "parallel", "arbitrary")),
        name="attn_prompt",
    )(rel_bias, q4, gates_t, kc, vct, k_rows, v4, k_rows, v4, bias_tiles)
    return o.reshape(bsz, nq, seq)


SLC_PAGES = 32
NEW_PAD = 16


def _bias_rows(trow_ref, dist):
    bkt = _bucket_of(dist)
    bias = jnp.broadcast_to(trow_ref[0:1, :], dist.shape)
    for j in range(1, NUM_BUCKETS):
        bias = jnp.where(bkt == j, trow_ref[j:j + 1, :], bias)
    return bias


def _attn_sample_kernel(pt_ref, *refs, n_pages, past_len, n_new, n_cmp, n_blk):
    del pt_ref
    page_refs = refs[:n_pages]
    (qbd_ref, gate_ref, trow_ref, cmp_ref, knew_ref, vnewt_ref, wkt_ref, wvt_ref, wknew_ref, wvnewt_ref,
     o_ref, sel_ref, oc_ref, m_ref, l_ref, acc_ref, s_ref) = refs[n_pages:]
    c = pl.program_id(1)
    qbd = qbd_ref[0]
    gd = qbd.shape[0]
    bpp = PAGE_SIZE // BLK
    npb = past_len // BLK
    pps = past_len // PAGE_SIZE

    def qpos_like(shape):
        return past_len + lax.broadcasted_iota(jnp.int32, shape, 1) % n_new

    def block_of_row(row):
        return jnp.where(row < n_cmp, (row % pps) * bpp + row // pps, row)

    @pl.when(c == 0)
    def _():
        ppc = cmp_ref.shape[3] // N_KV
        lane_g = (lax.broadcasted_iota(jnp.int32, (pps, LANES), 1) // n_new) % N_KV

        def slab(kv, g):
            return jnp.concatenate([cmp_ref[0, ch, kv, g * ppc:(g + 1) * ppc, :] for ch in range(cmp_ref.shape[1])],
                                   axis=0).astype(MXU_DTYPE)

        zero_q = jnp.zeros((HEAD_DIM, LANES), MXU_DTYPE)
        parts = []
        for n in range(bpp):
            s_n = jnp.zeros((pps, LANES), F32)
            for g in range(N_KV):
                q_n = jnp.concatenate([qbd[g * HEAD_DIM:(g + 1) * HEAD_DIM] if t == n else zero_q for t in range(bpp)],
                                      axis=0)
                s_n = jnp.where(lane_g == g, _mm(slab(0, g), q_n), s_n)
            parts.append(s_n)
        blk = block_of_row(lax.broadcasted_iota(jnp.int32, (n_cmp, LANES), 0))
        dist = qpos_like((n_cmp, LANES)) - (blk * BLK + BLK - 1)
        ok = dist >= 0
        lg = jnp.where(ok, jnp.concatenate(parts, axis=0) + _bias_rows(trow_ref, dist), NEG_INF)
        p = jnp.exp(lg - jnp.max(lg, axis=0, keepdims=True)) * ok.astype(F32)
        pc = p / jnp.maximum(jnp.sum(p, axis=0, keepdims=True), 1e-30)
        pc_b = pc.astype(MXU_DTYPE)
        o_groups = []
        for g in range(N_KV):
            v_g = slab(1, g)
            o_g = jnp.zeros((HEAD_DIM, LANES), F32)
            for n in range(bpp):
                o_g = o_g + _tn(v_g, pc_b[n * pps:(n + 1) * pps])[n * HEAD_DIM:(n + 1) * HEAD_DIM]
            o_groups.append(o_g)
        oc_ref[...] = jnp.concatenate(o_groups, axis=0)
        quarter = LANES // 4
        imp = pc + pltpu.roll(pc, quarter, 1) + pltpu.roll(pc, 2 * quarter, 1) + pltpu.roll(pc, 3 * quarter, 1)
        nbp = sel_ref.shape[0]
        imp = jnp.concatenate([imp, jnp.zeros((nbp - n_cmp, LANES), F32)], axis=0)
        blk = block_of_row(lax.broadcasted_iota(jnp.int32, (nbp, LANES), 0))
        cur = qpos_like((nbp, LANES)) // BLK
        forced = (blk == 0) | (blk == cur) | (blk == cur - 1)
        score = jnp.where(forced, FORCED_SCORE, jnp.where(blk <= cur, imp, -1.0))
        score = jnp.where(blk >= n_blk, -2.0, score)
        sel = jnp.zeros((nbp, LANES), F32)
        for _ in range(min(N_SEL, n_blk)):
            mx = jnp.max(score, axis=0, keepdims=True)
            idx = jnp.min(jnp.where(score == mx, blk, nbp), axis=0, keepdims=True)
            hit = blk == idx
            sel = jnp.where(hit, 1.0, sel)
            score = jnp.where(hit, -3.0, score)
        sel_ref[...] = sel
        m_ref[...] = jnp.full(m_ref.shape, NEG_INF, F32)
        l_ref[...] = jnp.zeros(l_ref.shape, F32)
        acc_ref[...] = jnp.zeros(acc_ref.shape, F32)

    tok = lax.broadcasted_iota(jnp.int32, (PAGE_SIZE, LANES), 0)
    qpos_p = qpos_like((PAGE_SIZE, LANES))

    far_pages = min(max((past_len - FAR_DIST - (PAGE_SIZE - 1)) // PAGE_SIZE + 1, 0), past_len // PAGE_SIZE)
    assert past_len // PAGE_SIZE - far_pages <= n_pages
    first_near = far_pages - (past_len // PAGE_SIZE - n_pages)

    def page_logits(near):
        for j in range(n_pages):
            pg = c * n_pages + j
            st = _tn(page_refs[j][0, 0].reshape(gd, PAGE_SIZE).astype(MXU_DTYPE), qbd)
            if near and j >= first_near:
                st = st + _bias_rows(trow_ref, qpos_p - (pg * PAGE_SIZE + tok))
            else:
                st = st + trow_ref[NUM_BUCKETS - 1:NUM_BUCKETS, :]
            chosen = jnp.broadcast_to(sel_ref[pl.ds((bpp - 1) * pps + pg, 1), :], (PAGE_SIZE, LANES))
            for t in reversed(range(bpp - 1)):
                chosen = jnp.where(tok < (t + 1) * BLK, sel_ref[pl.ds(t * pps + pg, 1), :], chosen)
            s_ref[j * PAGE_SIZE:(j + 1) * PAGE_SIZE, :] = jnp.where(chosen > 0.5, st, NEG_INF)

    is_last = c == pl.num_programs(1) - 1
    pl.when(jnp.logical_not(is_last))(functools.partial(page_logits, False))
    pl.when(is_last)(functools.partial(page_logits, True))
    s_all = s_ref[...]
    m_old = m_ref[...]
    m_new = jnp.maximum(m_old, jnp.max(s_all, axis=0, keepdims=True))
    a = jnp.exp(m_old - m_new)
    p = jnp.exp(s_all - m_new)
    acc = acc_ref[...] * a
    for j in range(n_pages):
        vt = page_refs[j][0, 1].reshape(gd, PAGE_SIZE).astype(MXU_DTYPE)
        acc = acc + _mm(vt, p[j * PAGE_SIZE:(j + 1) * PAGE_SIZE].astype(MXU_DTYPE))
    acc_ref[...] = acc
    l_ref[...] = a * l_ref[...] + jnp.sum(p, axis=0, keepdims=True)
    m_ref[...] = m_new

    @pl.when(c == pl.num_programs(1) - 1)
    def _():
        tok_n = lax.broadcasted_iota(jnp.int32, (NEW_PAD, LANES), 0)
        dist_n = qpos_like((NEW_PAD, LANES)) - (past_len + tok_n)
        live_n = (tok_n < n_new) & (dist_n >= 0)
        bias_n = _bias_rows(trow_ref, dist_n)
        chosen_n = sel_ref[pl.ds(npb, 1), :] > 0.5
        s_n = jnp.where(live_n & chosen_n, _mm(knew_ref[0], qbd) + bias_n, NEG_INF)
        m_old = m_ref[...]
        m_fin = jnp.maximum(m_old, jnp.max(s_n, axis=0, keepdims=True))
        a = jnp.exp(m_old - m_fin)
        p_n = jnp.exp(s_n - m_fin)
        acc_s = acc_ref[...] * a + _mm(vnewt_ref[0], p_n.astype(MXU_DTYPE))
        o_s = acc_s / (a * l_ref[...] + jnp.sum(p_n, axis=0, keepdims=True))
        wb = wkt_ref.shape[-1]
        tok_w = lax.broadcasted_iota(jnp.int32, (wb, LANES), 0)
        kwpos = past_len - wb + tok_w
        dw = qpos_like((wb, LANES)) - kwpos
        ok_w = (dw >= 0) & (dw < WINDOW) & (kwpos >= 0)
        s_w = _tn(wkt_ref[0, 0].reshape(gd, wb).astype(MXU_DTYPE), qbd) + _bias_rows(trow_ref, dw)
        s_w = jnp.where(ok_w, s_w, NEG_INF)
        s_wn = jnp.where(live_n & (dist_n < WINDOW), _mm(wknew_ref[0], qbd) + bias_n, NEG_INF)
        m_w = jnp.maximum(jnp.max(s_w, axis=0, keepdims=True), jnp.max(s_wn, axis=0, keepdims=True))
        p_w = jnp.exp(s_w - m_w)
        p_wn = jnp.exp(s_wn - m_w)
        o_w = (_mm(wvt_ref[0, 0].reshape(gd, wb).astype(MXU_DTYPE), p_w.astype(MXU_DTYPE))
               + _mm(wvnewt_ref[0], p_wn.astype(MXU_DTYPE)))
        o_w = o_w / (jnp.sum(p_w, axis=0, keepdims=True) + jnp.sum(p_wn, axis=0, keepdims=True))
        gt = gate_ref[0]
        tot = gt[0:1, :] * oc_ref[...] + gt[1:2, :] * o_s + gt[2:3, :] * o_w
        lane_g = (lax.broadcasted_iota(jnp.int32, (HEAD_DIM, LANES), 1) // n_new) % N_KV
        out = jnp.zeros((HEAD_DIM, LANES), F32)
        for g in range(N_KV):
            out = out + jnp.where(lane_g == g, tot[g * HEAD_DIM:(g + 1) * HEAD_DIM, :], 0.0)
        o_ref[0] = out


def _attn_sample(q, gates, cmp_raw, kv_new_t, slc_pages, page_table, win_t, rel_bias):
    bsz, n_new, nq = q.shape
    n_heads = nq // HEAD_DIM
    group = n_heads // N_KV
    pps = page_table.shape[1]
    past_len = pps * PAGE_SIZE
    n_cmp = cmp_raw.shape[1] * (cmp_raw.shape[3] // N_KV) * (PAGE_SIZE // BLK)
    n_blk = past_len // BLK + -(-n_new // BLK)
    nbp = -(-n_blk // 8) * 8
    gd = N_KV * HEAD_DIM
    wb = win_t.shape[-1]
    n_pages = min(SLC_PAGES, pps)
    assert n_heads * n_new == LANES and group == 4 and n_new <= min(BLK, NEW_PAD) and pps % n_pages == 0
    assert n_cmp == past_len // BLK and n_cmp % 8 == 0
    q5 = q.reshape(bsz, n_new, N_KV, group, HEAD_DIM)
    qbd = jnp.einsum('hg,bqgrd->bhdrgq', jnp.eye(N_KV, dtype=q.dtype), q5).reshape(bsz, gd, LANES).astype(MXU_DTYPE)
    g5 = gates[:, :, :3 * n_heads].reshape(bsz, n_new, N_KV, group, 3)
    gate_t = jnp.pad(g5.transpose(0, 4, 3, 2, 1).reshape(bsz, 3, LANES), ((0, 0), (0, 5), (0, 0)))
    trow = jnp.broadcast_to(rel_bias.reshape(NUM_BUCKETS, N_KV, group).transpose(0, 2, 1)[..., None],
                            (NUM_BUCKETS, group, N_KV, n_new)).reshape(NUM_BUCKETS, LANES)
    new = kv_new_t.reshape(6, gd, bsz, n_new).transpose(0, 2, 1, 3)
    new = jnp.pad(new, ((0, 0), (0, 0), (0, 0), (0, NEW_PAD - n_new))).astype(MXU_DTYPE)
    k_rows = lambda s: jnp.swapaxes(new[s], 1, 2)
    page_spec = lambda j: pl.BlockSpec((1, 2, N_KV, HEAD_DIM, PAGE_SIZE),
                                       lambda b, c, pt: (pt[b, c * n_pages + j], 0, 0, 0, 0))
    per_b = lambda a: pl.BlockSpec((1,) + a.shape[1:], lambda b, c, pt: (b,) + (0,) * (a.ndim - 1))
    win_spec = lambda s: pl.BlockSpec((1, 1, N_KV, HEAD_DIM, wb), lambda b, c, pt: (b, s, 0, 0, 0))
    smalls = (qbd, gate_t)
    o_t = pl.pallas_call(
        functools.partial(_attn_sample_kernel, n_pages=n_pages, past_len=past_len, n_new=n_new, n_cmp=n_cmp,
                          n_blk=n_blk),
        grid_spec=pltpu.PrefetchScalarGridSpec(
            num_scalar_prefetch=1,
            grid=(bsz, pps // n_pages),
            in_specs=[page_spec(j) for j in range(n_pages)]
            + [per_b(qbd), per_b(gate_t), pl.BlockSpec(trow.shape, lambda b, c, pt: (0, 0)),
               per_b(cmp_raw), per_b(k_rows(2)), per_b(new[3]), win_spec(0), win_spec(1),
               per_b(k_rows(4)), per_b(new[5])],
            out_specs=pl.BlockSpec((1, HEAD_DIM, LANES), lambda b, c, pt: (b, 0, 0)),
            scratch_shapes=[pltpu.VMEM((nbp, LANES), F32), pltpu.VMEM((gd, LANES), F32), pltpu.VMEM((1, LANES), F32),
                            pltpu.VMEM((1, LANES), F32), pltpu.VMEM((gd, LANES), F32),
                            pltpu.VMEM((n_pages * PAGE_SIZE, LANES), F32)]),
        out_shape=jax.ShapeDtypeStruct((bsz, HEAD_DIM, LANES), F32),
        compiler_params=_params(("parallel", "arbitrary")),
        name="attn_sample",
    )(page_table, *([slc_pages] * n_pages), qbd, gate_t, trow, cmp_raw, k_rows(2), new[3], win_t, win_t,
      k_rows(4), new[5])
    del smalls
    return o_t.reshape(bsz, HEAD_DIM, group, N_KV, n_new).transpose(0, 4, 3, 2, 1).reshape(bsz, n_new, nq)


def _tokens_last(x):
    nd = x.ndim
    return jnp.transpose(x, tuple(range(nd - 4)) + (nd - 3, nd - 2, nd - 1, nd - 4))


def _tokens_first(x):
    nd = x.ndim
    return jnp.transpose(x, tuple(range(nd - 4)) + (nd - 1, nd - 4, nd - 3, nd - 2))


def kernel(x_prompt, x_sample, cache_kv_cmp, cache_kv_slc, cache_win, state_conv, page_table, ln_g, ln_b, sg_w_in, sg_b_in, sg_ln_g, sg_ln_b, sg_w_s, sg_b_s, sg_w_out, sg_b_out, ffn_w_up, ffn_b_up, ffn_w_dw, ffn_b_dw, ffn_w_down, ffn_b_down, kv_w, cmp_pe, cmp_w1, cmp_b1, cmp_w2, cmp_b2, nsa_w_qg, nsa_b_qg, nsa_w_o, nsa_b_o, rel_bias):
    depth = ln_g.shape[0]
    n_a = sg_w_in.shape[0]
    assert depth == DEPTH
    bsz, seq, d_model = x_prompt.shape
    dbsz, n_new, _ = x_sample.shape
    d_ff = ffn_w_down.shape[1]
    kvd = (2, N_KV, HEAD_DIM)
    gd = N_KV * HEAD_DIM
    cmp_w = (cmp_pe, cmp_w1, cmp_b1, cmp_w2, cmp_b2)

    def trunk(x, seq_len, conv_state, attend_layer, emit_v):
        n_seq = x.shape[0] // seq_len
        new_conv, sg_v, kv_t, ctx = [], [], None, None
        for l in range(depth):
            if l < n_a:
                x, v = _sgu_layer(x, seq_len, sg_w_in[l], sg_b_in[l], sg_ln_g[l], sg_ln_b[l], sg_w_s[l], sg_b_s[l],
                                  sg_w_out[l], sg_b_out[l], ln_g[l, 0], ln_b[l, 0], emit_v)
                if emit_v:
                    sg_v.append(v.reshape(n_seq, seq_len, -1))
            else:
                i = l - n_a
                x = attend_layer(x, ctx, nsa_w_qg[i], nsa_b_qg[i], nsa_w_o[i], nsa_b_o[i], ln_g[l, 0], ln_b[l, 0])
            x, cs = _ffn_layer(x, seq_len, conv_state[l], ffn_w_up[l], ffn_b_up[l], ffn_w_dw[l], ffn_b_dw[l],
                               ffn_w_down[l], ffn_b_down[l], ln_g[l, 1], ln_b[l, 1])
            new_conv.append(cs)
            if l == n_a - 1:
                kv_t, ctx = attend_layer.prepare(x)
        return x, jnp.stack(new_conv), (jnp.stack(sg_v) if sg_v else None), kv_t

    bias_tiles = _bias_tiles(rel_bias)

    def prompt_prepare(x):
        kv_t, v_t, k_rows = _kv_proj(x, bsz, kv_w)
        kvc = _compress_pages(kv_t, None, *cmp_w)
        return kv_t, (kvc, k_rows, v_t)

    def prompt_attend(x, ctx, w_qg, b_qg, w_o, b_o, g, b):
        kvc, k_rows, v_t = ctx
        q_t, gates_t = _qg_proj_t(x, bsz, w_qg, b_qg)
        o_t = _attn_prompt(q_t, gates_t, kvc, k_rows, v_t, bias_tiles, rel_bias)
        return _oproj_t(o_t, x, w_o, b_o, g, b)

    prompt_attend.prepare = prompt_prepare
    zeros_state = jnp.zeros((depth, bsz, CONV_W - 1, d_ff), x_prompt.dtype)
    y_p, conv_p, _, kv_p = trunk(x_prompt.reshape(bsz * seq, d_model), seq, zeros_state, prompt_attend, False)
    kv_p = kv_p.reshape(3, bsz, 2, N_KV, HEAD_DIM, seq)

    cmp_pages = _tokens_last(cache_kv_cmp)
    slc_pages = _tokens_last(cache_kv_slc)
    win_t = _tokens_last(cache_win)

    def sample_prepare(x):
        kv_t, _, _ = _kv_proj(x, 1, kv_w)
        kv_t = kv_t.reshape(6 * gd, dbsz * n_new)
        kvc = _compress_pages(cmp_pages, page_table, *cmp_w, raw_layout=True)
        return kv_t, (kvc, kv_t)

    def sample_attend(x, ctx, w_qg, b_qg, w_o, b_o, g, b):
        kvc, kv_new_t = ctx
        q, gates = _qg_proj_flat(x, w_qg, b_qg)
        o = _attn_sample(q.reshape(dbsz, n_new, -1), gates.reshape(dbsz, n_new, -1), kvc, kv_new_t, slc_pages,
                         page_table, win_t, rel_bias)
        return _oproj_flat(o.reshape(dbsz * n_new, -1), x, w_o, b_o, g, b)

    sample_attend.prepare = sample_prepare
    assert n_new < BLK
    y_s, conv_s, sgv_s, kv_s = trunk(x_sample.reshape(dbsz * n_new, d_model), n_new, state_conv, sample_attend, True)
    kv_s = kv_s.reshape(3, 2, N_KV, HEAD_DIM, dbsz, n_new)

    win_len = min(WINDOW, seq)
    out_p = [_tokens_first(kv_p[s]) for s in range(3)]
    out_s = [jnp.transpose(kv_s[s], (3, 4, 0, 1, 2)) for s in range(3)]
    return (y_p.reshape(bsz, seq, d_model), y_s.reshape(dbsz, n_new, d_model),
            out_p[0], out_p[1], out_p[2][:, seq - win_len:], conv_p,
            out_s[0], out_s[1], out_s[2], conv_s, sgv_s)
```

```python
import functools
import math

import numpy as np
import jax
import jax.numpy as jnp
from jax import lax
from jax.experimental import pallas as pl
from jax.experimental.pallas import tpu as pltpu

N_SG_GROUPS = 8
CHUNK = 128
CONV_W = 3
HEAD_DIM = 64
N_KV = 4
BLK = 64
N_SEL = 16
WINDOW = 512
NUM_BUCKETS = 32
REL_MAX_DIST = 1024
Q_BLOCK = 128
PAGE_SIZE = 128
DEPTH = 2
ALPHA = (2 * DEPTH) ** 0.25
LN_EPS = 1e-5
NEG_INF = -1e30
FORCED_SCORE = 1e9
LOG2E = math.log2(math.e)

MXU_DTYPE = jnp.bfloat16
F32 = jnp.float32
LANES = 128
ROW_TILE = 256
SGU_ROW_TILE = 512
FFN_ROW_TILE = 512
VMEM_LIMIT = 56 * 1024 * 1024


def _bucket_thresholds():
    d = np.arange(0, 4 * REL_MAX_DIST)
    max_exact = NUM_BUCKETS // 2
    nf = np.maximum(d, 1).astype(np.float32)
    large = max_exact + (np.log(nf / np.float32(max_exact)) / np.float32(math.log(REL_MAX_DIST / max_exact))
                         * np.float32(NUM_BUCKETS - max_exact)).astype(np.int32)
    b = np.where(d < max_exact, d, np.minimum(large, NUM_BUCKETS - 1))
    assert np.all(np.diff(b) >= 0) and b[-1] == NUM_BUCKETS - 1
    return [int(np.nonzero(b >= j)[0][0]) for j in range(NUM_BUCKETS)]


BUCKET_THR = _bucket_thresholds()
FAR_DIST = BUCKET_THR[-1]
N_BIAS_TILES = -(-(FAR_DIST + LANES) // LANES)


def _gelu(x):
    return 0.5 * x * (1.0 + lax.erf(x * (1.0 / math.sqrt(2.0))))


def _layer_norm(x, g, b):
    mu = jnp.mean(x, axis=-1, keepdims=True)
    xc = x - mu
    var = jnp.mean(xc * xc, axis=-1, keepdims=True)
    return xc * lax.rsqrt(var + LN_EPS) * g + b


def _mm(a, b):
    return jnp.dot(a, b, preferred_element_type=F32)


def _mm_nt(a, b):
    return lax.dot_general(a, b, (((1,), (1,)), ((), ())), preferred_element_type=F32)


def _tn(a, b):
    return lax.dot_general(a, b, (((0,), (0,)), ((), ())), preferred_element_type=F32)


def _const_spec(shape):
    nd = len(shape)
    return pl.BlockSpec(shape, lambda *_: (0,) * nd, pipeline_mode=pl.Buffered(1))


def _params(sem):
    return pltpu.CompilerParams(dimension_semantics=sem, vmem_limit_bytes=VMEM_LIMIT)


def _bucket_of(dist):
    b = jnp.zeros(dist.shape, jnp.int32)
    for j in range(1, NUM_BUCKETS):
        b = b + (dist >= BUCKET_THR[j]).astype(jnp.int32)
    return b


def _sgu_kernel(x_ref, win_ref, bin_ref, gv_ref, bv_ref, wg_ref, bs_ref, wout_ref, bout_ref, lng_ref, lnb_ref,
                o_ref, *maybe_v_ref, d_sg):
    x = x_ref[...]
    xb = x.astype(MXU_DTYPE)
    v = _gelu(_mm(xb, win_ref[:, d_sg:]) + bin_ref[:, d_sg:])
    vn = _layer_norm(v, gv_ref[...], bv_ref[...])
    if maybe_v_ref:
        maybe_v_ref[0][...] = vn
    cg = d_sg // N_SG_GROUPS
    mix = wg_ref.shape[1]
    acc = jnp.zeros(o_ref.shape, F32)
    for g in range(N_SG_GROUPS):
        lo, hi = g * cg, (g + 1) * cg
        vg = vn[:, lo:hi].astype(MXU_DTYPE)
        gate = jnp.concatenate([_mm(wg_ref[g], vg[r:r + mix]) + bs_ref[g] for r in range(0, x.shape[0], mix)], axis=0)
        u = _gelu(_mm(xb, win_ref[:, lo:hi]) + bin_ref[:, lo:hi])
        acc = acc + _mm((u * gate).astype(MXU_DTYPE), wout_ref[lo:hi, :])
    y = acc + bout_ref[...]
    o_ref[...] = _layer_norm(ALPHA * x + y, lng_ref[...], lnb_ref[...])


def _sgu_layer(x, seq_len, w_in, b_in, g_v, b_v, w_s, b_s, w_out, b_out, ln_g, ln_b, emit_v):
    t, d = x.shape
    d_sg = w_out.shape[0]
    cg = d_sg // N_SG_GROUPS
    cl = min(CHUNK, seq_len)
    tm = min(SGU_ROW_TILE, t)
    mix = min(ROW_TILE, tm)
    assert seq_len % cl == 0 and mix % cl == 0 and tm % mix == 0 and t % tm == 0
    w_tri = jnp.tril(w_s[:, :cl, :cl])
    eye = jnp.eye(mix // cl, dtype=w_s.dtype)
    wg = jnp.einsum('ab,gts->gatbs', eye, w_tri).reshape(N_SG_GROUPS, mix, mix).astype(MXU_DTYPE)
    bs = jnp.broadcast_to(jnp.tile(b_s[:, :cl], (1, mix // cl))[:, :, None], (N_SG_GROUPS, mix, cg))
    row = lambda a: a.reshape(1, -1)
    out_specs = [pl.BlockSpec((tm, d), lambda i: (i, 0))]
    out_shape = [jax.ShapeDtypeStruct((t, d), F32)]
    if emit_v:
        out_specs.append(pl.BlockSpec((tm, d_sg), lambda i: (i, 0)))
        out_shape.append(jax.ShapeDtypeStruct((t, d_sg), F32))
    outs = pl.pallas_call(
        functools.partial(_sgu_kernel, d_sg=d_sg),
        grid=(t // tm,),
        in_specs=[pl.BlockSpec((tm, d), lambda i: (i, 0)),
                  _const_spec((d, 2 * d_sg)), _const_spec((1, 2 * d_sg)), _const_spec((1, d_sg)), _const_spec((1, d_sg)),
                  _const_spec((N_SG_GROUPS, mix, mix)), _const_spec((N_SG_GROUPS, mix, cg)),
                  _const_spec((d_sg, d)), _const_spec((1, d)), _const_spec((1, d)), _const_spec((1, d))],
        out_specs=out_specs,
        out_shape=out_shape,
        compiler_params=_params(("parallel",)),
        name="sgu_layer",
    )(x, w_in.astype(MXU_DTYPE), row(b_in), row(g_v), row(b_v), wg, bs, w_out.astype(MXU_DTYPE), row(b_out),
      row(ln_g), row(ln_b))
    return outs[0], (outs[1] if emit_v else None)


def _ffn_tail(x, a, gt, s1, s2, wdw_ref, bdw_ref, wdown_ref, bdown_ref, lng_ref, lnb_ref, o_ref):
    c = bdw_ref[...] + s2 * wdw_ref[0:1, :] + s1 * wdw_ref[1:2, :] + a * wdw_ref[2:3, :]
    h = (_gelu(c) * gt).astype(MXU_DTYPE)
    y = _mm(h, wdown_ref[...]) + bdown_ref[...]
    o_ref[...] = _layer_norm(ALPHA * x + y, lng_ref[...], lnb_ref[...])


def _ffn_long_kernel(x_ref, xh_ref, st_ref, wup_ref, bup_ref, wdw_ref, bdw_ref, wdown_ref, bdown_ref, lng_ref, lnb_ref,
                     o_ref, conv_ref, *, seq_len, d_ff):
    tm = x_ref.shape[0]
    x = x_ref[...]
    xb = x.astype(MXU_DTYPE)
    a = _mm(xb, wup_ref[:, :d_ff]) + bup_ref[:, :d_ff]
    gt = _mm(xb, wup_ref[:, d_ff:]) + bup_ref[:, d_ff:]
    ah = _mm(xh_ref[...].astype(MXU_DTYPE), wup_ref[:, :d_ff]) + bup_ref[:, :d_ff]
    at_start = (pl.program_id(0) * tm) % seq_len == 0
    p1 = jnp.where(at_start, st_ref[0, 1:2, :], ah[7:8, :])
    p0 = jnp.where(at_start, st_ref[0, 0:1, :], ah[6:7, :])
    r = lax.broadcasted_iota(jnp.int32, a.shape, 0)
    s1 = jnp.where(r == 0, p1, pltpu.roll(a, 1, 0))
    s2 = jnp.where(r == 0, p0, jnp.where(r == 1, p1, pltpu.roll(a, 2, 0)))
    conv_ref[0] = a[tm - (CONV_W - 1):, :]
    _ffn_tail(x, a, gt, s1, s2, wdw_ref, bdw_ref, wdown_ref, bdown_ref, lng_ref, lnb_ref, o_ref)


def _ffn_short_kernel(x_ref, p0_ref, p1_ref, wup_ref, bup_ref, wdw_ref, bdw_ref, wdown_ref, bdown_ref, lng_ref, lnb_ref,
                      o_ref, a_ref, *, seq_len, d_ff):
    x = x_ref[...]
    xb = x.astype(MXU_DTYPE)
    a = _mm(xb, wup_ref[:, :d_ff]) + bup_ref[:, :d_ff]
    gt = _mm(xb, wup_ref[:, d_ff:]) + bup_ref[:, d_ff:]
    pos = lax.broadcasted_iota(jnp.int32, a.shape, 0) % seq_len
    s1 = jnp.where(pos == 0, p1_ref[...], pltpu.roll(a, 1, 0))
    s2 = jnp.where(pos == 0, p0_ref[...], jnp.where(pos == 1, p1_ref[...], pltpu.roll(a, 2, 0)))
    a_ref[...] = a
    _ffn_tail(x, a, gt, s1, s2, wdw_ref, bdw_ref, wdown_ref, bdown_ref, lng_ref, lnb_ref, o_ref)


def _ffn_layer(x, seq_len, conv_state, w_up, b_up, w_dw, b_dw, w_down, b_down, ln_g, ln_b):
    t, d = x.shape
    d_ff = w_down.shape[0]
    n_seq = t // seq_len
    tm = min(FFN_ROW_TILE, t)
    assert seq_len >= CONV_W - 1 and t % tm == 0
    row = lambda a: a.reshape(1, -1)
    weights = (w_up.astype(MXU_DTYPE), row(b_up), w_dw, row(b_dw), w_down.astype(MXU_DTYPE), row(b_down), row(ln_g), row(ln_b))
    w_specs = [_const_spec((d, 2 * d_ff)), _const_spec((1, 2 * d_ff)), _const_spec((CONV_W, d_ff)), _const_spec((1, d_ff)),
               _const_spec((d_ff, d)), _const_spec((1, d)), _const_spec((1, d)), _const_spec((1, d))]
    if seq_len % tm == 0:
        tiles_per_seq = seq_len // tm
        out, conv = pl.pallas_call(
            functools.partial(_ffn_long_kernel, seq_len=seq_len, d_ff=d_ff),
            grid=(t // tm,),
            in_specs=[pl.BlockSpec((tm, d), lambda i: (i, 0)),
                      pl.BlockSpec((8, d), lambda i: (jnp.maximum(i * (tm // 8) - 1, 0), 0)),
                      pl.BlockSpec((1, CONV_W - 1, d_ff), lambda i: (i // tiles_per_seq, 0, 0))] + w_specs,
            out_specs=[pl.BlockSpec((tm, d), lambda i: (i, 0)),
                       pl.BlockSpec((1, CONV_W - 1, d_ff), lambda i: (i // tiles_per_seq, 0, 0))],
            out_shape=[jax.ShapeDtypeStruct((t, d), F32), jax.ShapeDtypeStruct((n_seq, CONV_W - 1, d_ff), F32)],
            compiler_params=_params(("arbitrary",)),
            name="ffn_long",
        )(x, x, conv_state, *weights)
        return out, conv
    assert tm % seq_len == 0
    p0 = jnp.repeat(conv_state[:, 0, :], seq_len, axis=0)
    p1 = jnp.repeat(conv_state[:, 1, :], seq_len, axis=0)
    out, a = pl.pallas_call(
        functools.partial(_ffn_short_kernel, seq_len=seq_len, d_ff=d_ff),
        grid=(t // tm,),
        in_specs=[pl.BlockSpec((tm, d), lambda i: (i, 0)), pl.BlockSpec((tm, d_ff), lambda i: (i, 0)),
                  pl.BlockSpec((tm, d_ff), lambda i: (i, 0))] + w_specs,
        out_specs=[pl.BlockSpec((tm, d), lambda i: (i, 0)), pl.BlockSpec((tm, d_ff), lambda i: (i, 0))],
        out_shape=[jax.ShapeDtypeStruct((t, d), F32), jax.ShapeDtypeStruct((t, d_ff), F32)],
        compiler_params=_params(("parallel",)),
        name="ffn_short",
    )(x, p0, p1, *weights)
    return out, a.reshape(n_seq, seq_len, d_ff)[:, seq_len - (CONV_W - 1):, :]


STEP_SUBS = 4
STEP_BLKS = STEP_SUBS * LANES // BLK


def _kv_kernel(x_ref, wt_ref, wk_ref, kvc_ref, kvs_ref, kvw_ref, vt_ref, k_ref, *, tiles):
    tm = x_ref.shape[0]
    xb = x_ref[...].astype(MXU_DTYPE)
    r = _mm_nt(wt_ref[...], xb)
    gd = N_KV * HEAD_DIM
    for pair, branch_ref in enumerate((kvc_ref, kvs_ref, kvw_ref)):
        branch_ref[0] = r[pair * 2 * gd:(pair + 1) * 2 * gd, :]
    pos = (pl.program_id(0) % tiles) * tm + lax.broadcasted_iota(jnp.int32, (tm, LANES), 0)
    lane = lax.broadcasted_iota(jnp.int32, (tm, LANES), 1)
    onehot = jnp.where((pos // BLK) % STEP_BLKS + HEAD_DIM == lane, 1.0, 0.0)
    for s in range(2):
        vt_ref[0, s] = r[(2 * s + 3) * gd:(2 * s + 4) * gd, :].astype(MXU_DTYPE)
        for g in range(N_KV):
            k_ref[0, s, g] = (_mm(xb, wk_ref[s, g]) + onehot).astype(MXU_DTYPE)


def _kv_proj(x, n_seq, kv_w):
    t, d = x.shape
    seq_len = t // n_seq
    n_out = kv_w.shape[1]
    gd = N_KV * HEAD_DIM
    tm = ROW_TILE
    assert seq_len % tm == 0 and n_out == 6 * gd and HEAD_DIM + STEP_BLKS <= LANES
    tiles = seq_len // tm
    wk = kv_w.reshape(d, 3, 2, N_KV, HEAD_DIM)[:, 1:, 0].transpose(1, 2, 0, 3)
    wk = jnp.pad(wk, ((0, 0), (0, 0), (0, 0), (0, LANES - HEAD_DIM))).astype(MXU_DTYPE)
    return pl.pallas_call(
        functools.partial(_kv_kernel, tiles=tiles),
        grid=(t // tm,),
        in_specs=[pl.BlockSpec((tm, d), lambda i: (i, 0)), _const_spec((n_out, d)), _const_spec(wk.shape)],
        out_specs=[pl.BlockSpec((1, 2 * gd, tm), lambda i: (i // tiles, 0, i % tiles))] * 3
        + [pl.BlockSpec((1, 2, gd, tm), lambda i: (i // tiles, 0, 0, i % tiles)),
           pl.BlockSpec((1, 2, N_KV, tm, LANES), lambda i: (i // tiles, 0, 0, i % tiles, 0))],
        out_shape=[jax.ShapeDtypeStruct((n_seq, 2 * gd, seq_len), F32)] * 3
        + [jax.ShapeDtypeStruct((n_seq, 2, gd, seq_len), MXU_DTYPE),
                   jax.ShapeDtypeStruct((n_seq, 2, N_KV, seq_len, LANES), MXU_DTYPE)],
        compiler_params=_params(("parallel",)),
        name="kv_proj",
    )(x, kv_w.T.astype(MXU_DTYPE), wk)


def _sigmoid(x):
    return 1.0 / (1.0 + jnp.exp(-x))


GATE_ROWS = 16


def _qg_t_kernel(x_ref, wq_ref, bq_ref, wg_ref, bg_ref, q_ref, gate_ref):
    xb = x_ref[...].astype(MXU_DTYPE)
    q_ref[0] = ((_mm_nt(wq_ref[...], xb) + bq_ref[...]) * (HEAD_DIM ** -0.5 * LOG2E)).astype(MXU_DTYPE)
    gate_ref[0] = _sigmoid(_mm_nt(wg_ref[...], xb) + bg_ref[...])


def _qg_proj_t(x, n_seq, w_qg, b_qg):
    t, d = x.shape
    seq_len = t // n_seq
    n_heads = (w_qg.shape[1] // (HEAD_DIM + 3))
    group = n_heads // N_KV
    nq = n_heads * HEAD_DIM
    tm = ROW_TILE
    assert seq_len % tm == 0 and 3 * group <= GATE_ROWS
    tiles = seq_len // tm
    pad = GATE_ROWS - 3 * group
    wg = jnp.pad(w_qg[:, nq:].T.reshape(N_KV, 3 * group, d), ((0, 0), (0, pad), (0, 0))).reshape(N_KV * GATE_ROWS, d)
    bg = jnp.pad(b_qg[nq:].reshape(N_KV, 3 * group), ((0, 0), (0, pad))).reshape(N_KV * GATE_ROWS, 1)
    bg = jnp.broadcast_to(bg, (N_KV * GATE_ROWS, tm))
    bq = jnp.broadcast_to(b_qg[:nq].reshape(nq, 1), (nq, tm))
    return pl.pallas_call(
        _qg_t_kernel,
        grid=(t // tm,),
        in_specs=[pl.BlockSpec((tm, d), lambda i: (i, 0)), _const_spec((nq, d)), _const_spec((nq, tm)),
                  _const_spec(wg.shape), _const_spec(bg.shape)],
        out_specs=[pl.BlockSpec((1, nq, tm), lambda i: (i // tiles, 0, i % tiles)),
                   pl.BlockSpec((1, N_KV * GATE_ROWS, tm), lambda i: (i // tiles, 0, i % tiles))],
        out_shape=[jax.ShapeDtypeStruct((n_seq, nq, seq_len), MXU_DTYPE),
                   jax.ShapeDtypeStruct((n_seq, N_KV * GATE_ROWS, seq_len), F32)],
        compiler_params=_params(("parallel",)),
        name="qg_proj_t",
    )(x, w_qg[:, :nq].T.astype(MXU_DTYPE), bq, wg.astype(MXU_DTYPE), bg)


def _qg_flat_kernel(x_ref, wq_ref, bq_ref, wg_ref, bg_ref, q_ref, gate_ref):
    xb = x_ref[...].astype(MXU_DTYPE)
    q_ref[...] = (_mm(xb, wq_ref[...]) + bq_ref[...]) * (HEAD_DIM ** -0.5)
    gate_ref[...] = _sigmoid(_mm(xb, wg_ref[...]) + bg_ref[...])


def _qg_proj_flat(x, w_qg, b_qg):
    t, d = x.shape
    n_gate = w_qg.shape[1] // (HEAD_DIM + 3) * 3
    nq = w_qg.shape[1] - n_gate
    tm = ROW_TILE
    assert t % tm == 0 and n_gate <= LANES
    wg = jnp.pad(w_qg[:, nq:], ((0, 0), (0, LANES - n_gate))).astype(MXU_DTYPE)
    bg = jnp.pad(b_qg[nq:], (0, LANES - n_gate)).reshape(1, LANES)
    return pl.pallas_call(
        _qg_flat_kernel,
        grid=(t // tm,),
        in_specs=[pl.BlockSpec((tm, d), lambda i: (i, 0)), _const_spec((d, nq)), _const_spec((1, nq)),
                  _const_spec((d, LANES)), _const_spec((1, LANES))],
        out_specs=[pl.BlockSpec((tm, nq), lambda i: (i, 0)), pl.BlockSpec((tm, LANES), lambda i: (i, 0))],
        out_shape=[jax.ShapeDtypeStruct((t, nq), F32), jax.ShapeDtypeStruct((t, LANES), F32)],
        compiler_params=_params(("parallel",)),
        name="qg_proj_flat",
    )(x, w_qg[:, :nq].astype(MXU_DTYPE), b_qg[:nq].reshape(1, nq), wg, bg)


def _oproj_t_kernel(o_ref, x_ref, wo_ref, bo_ref, lng_ref, lnb_ref, y_ref):
    y = _tn(o_ref[0], wo_ref[...]) + bo_ref[...]
    y_ref[...] = _layer_norm(ALPHA * x_ref[...] + y, lng_ref[...], lnb_ref[...])


def _oproj_t(o_t, x, w_o, b_o, ln_g, ln_b):
    n_seq, nq, seq_len = o_t.shape
    t, d = x.shape
    tm = ROW_TILE
    tiles = seq_len // tm
    row = lambda a: a.reshape(1, -1)
    return pl.pallas_call(
        _oproj_t_kernel,
        grid=(t // tm,),
        in_specs=[pl.BlockSpec((1, nq, tm), lambda i: (i // tiles, 0, i % tiles)),
                  pl.BlockSpec((tm, d), lambda i: (i, 0)), _const_spec((nq, d)),
                  _const_spec((1, d)), _const_spec((1, d)), _const_spec((1, d))],
        out_specs=pl.BlockSpec((tm, d), lambda i: (i, 0)),
        out_shape=jax.ShapeDtypeStruct((t, d), F32),
        compiler_params=_params(("parallel",)),
        name="oproj_t",
    )(o_t, x, w_o.astype(MXU_DTYPE), row(b_o), row(ln_g), row(ln_b))


def _oproj_flat_kernel(o_ref, x_ref, wo_ref, bo_ref, lng_ref, lnb_ref, y_ref):
    y = _mm(o_ref[...].astype(MXU_DTYPE), wo_ref[...]) + bo_ref[...]
    y_ref[...] = _layer_norm(ALPHA * x_ref[...] + y, lng_ref[...], lnb_ref[...])


def _oproj_flat(o, x, w_o, b_o, ln_g, ln_b):
    t, d = x.shape
    tm = ROW_TILE
    row = lambda a: a.reshape(1, -1)
    return pl.pallas_call(
        _oproj_flat_kernel,
        grid=(t // tm,),
        in_specs=[pl.BlockSpec((tm, o.shape[1]), lambda i: (i, 0)), pl.BlockSpec((tm, d), lambda i: (i, 0)),
                  _const_spec(w_o.shape), _const_spec((1, d)), _const_spec((1, d)), _const_spec((1, d))],
        out_specs=pl.BlockSpec((tm, d), lambda i: (i, 0)),
        out_shape=jax.ShapeDtypeStruct((t, d), F32),
        compiler_params=_params(("parallel",)),
        name="oproj_flat",
    )(o, x, w_o.astype(MXU_DTYPE), row(b_o), row(ln_g), row(ln_b))


def _bias_tile_kernel(rel_ref, o_ref):
    h = pl.program_id(0)
    far = rel_ref[NUM_BUCKETS - 1, h]
    key = lax.broadcasted_iota(jnp.int32, (LANES, Q_BLOCK), 0)
    qry = lax.broadcasted_iota(jnp.int32, (LANES, Q_BLOCK), 1)
    for dl in range(o_ref.shape[1]):
        dist = dl * LANES + qry - key
        val = jnp.full((LANES, Q_BLOCK), (rel_ref[0, h] - far) * LOG2E, F32)
        for j in range(1, NUM_BUCKETS):
            val = jnp.where(dist >= BUCKET_THR[j], (rel_ref[j, h] - far) * LOG2E, val)
        o_ref[0, dl] = val


def _bias_tiles(rel_bias):
    n_heads = rel_bias.shape[1]
    return pl.pallas_call(
        _bias_tile_kernel,
        grid=(n_heads,),
        in_specs=[pl.BlockSpec(memory_space=pltpu.SMEM)],
        out_specs=pl.BlockSpec((1, N_BIAS_TILES + 1, LANES, Q_BLOCK), lambda h: (h, 0, 0, 0)),
        out_shape=jax.ShapeDtypeStruct((n_heads, N_BIAS_TILES + 1, LANES, Q_BLOCK), F32),
        compiler_params=_params(("parallel",)),
        name="bias_tiles",
    )(rel_bias)


CMP_PAGES = 32
PE_ROWS = 8


def _compress_kernel(pt_ref, pages_ref, pe_ref, w1_ref, b1_ref, w2_ref, b2_ref, o_ref, buf_ref, sem_ref,
                     *, n_pages, paged):
    b, c = pl.program_id(0), pl.program_id(1)
    n_chunks = pl.num_programs(1)
    step = b * n_chunks + c
    slot = step % 2
    page_rows = buf_ref.shape[1]

    def page_copy(sb, sc, j, sl):
        page = pt_ref[sb, sc * n_pages + j]
        if paged:
            src = pages_ref.at[page]
        else:
            src = pages_ref.at[sb, :, pl.ds(pl.multiple_of(page * PAGE_SIZE, PAGE_SIZE), PAGE_SIZE)]
        return pltpu.make_async_copy(src, buf_ref.at[sl, :, j, :], sem_ref.at[sl])

    @pl.when(step == 0)
    def _():
        for j in range(n_pages):
            page_copy(b, c, j, slot).start()

    @pl.when(step + 1 < pl.num_programs(0) * n_chunks)
    def _():
        nxt_c = jnp.where(c + 1 == n_chunks, 0, c + 1)
        nxt_b = jnp.where(c + 1 == n_chunks, b + 1, b)
        for j in range(n_pages):
            page_copy(nxt_b, nxt_c, j, 1 - slot).start()

    for j in range(n_pages):
        page_copy(b, c, j, slot).wait()
    rows = N_KV * n_pages

    def head_dim_row(k, g, d):
        return buf_ref[slot, (k * N_KV + g) * HEAD_DIM + d]

    for k in range(2):
        acc = jnp.zeros((rows + PE_ROWS, 2 * LANES), F32)
        for dd in range(HEAD_DIM // 2):
            parts = [jnp.concatenate([head_dim_row(k, g, 2 * dd), head_dim_row(k, g, 2 * dd + 1)], axis=1)
                     for g in range(N_KV)]
            parts.append(pe_ref[k, dd])
            acc = acc + _mm(jnp.concatenate(parts, axis=0).astype(MXU_DTYPE), w1_ref[k, dd])
        pre = acc[:rows] + acc[rows:rows + 1] + b1_ref[k]
        o_ref[0, 0, k] = _mm(_gelu(pre).astype(MXU_DTYPE), w2_ref[k]) + b2_ref[k]


def _compress_pages(pages, page_table, cmp_pe, cmp_w1, cmp_b1, cmp_w2, cmp_b2, raw_layout=False):
    paged = page_table is not None
    if paged:
        pages = pages.reshape(pages.shape[0], 2 * N_KV * HEAD_DIM, PAGE_SIZE)
    else:
        assert pages.shape[1] == 2 * N_KV * HEAD_DIM
        page_table = jnp.broadcast_to(jnp.arange(pages.shape[2] // PAGE_SIZE, dtype=jnp.int32)[None],
                                      (pages.shape[0], pages.shape[2] // PAGE_SIZE))
    n_seq, pps = page_table.shape
    bpp = PAGE_SIZE // BLK
    hid = cmp_w1.shape[-1]
    n_pages = min(CMP_PAGES, pps)
    page_rows = 2 * N_KV * HEAD_DIM
    assert pps % n_pages == 0 and n_pages % 8 == 0 and bpp * hid == 2 * LANES and bpp * HEAD_DIM == LANES
    n_chunks = pps // n_pages
    w1r = cmp_w1.reshape(2, BLK, HEAD_DIM // 2, 2, hid)
    eye = jnp.eye(bpp, dtype=cmp_w1.dtype)
    w1bd = jnp.einsum('kldeh,nm->kdenlmh', w1r, eye).reshape(2, HEAD_DIM // 2, 2 * PAGE_SIZE, bpp * hid).astype(MXU_DTYPE)
    pe_t = jnp.tile(jnp.swapaxes(cmp_pe, 1, 2), (1, 1, bpp))
    pe_rows = jnp.pad(pe_t.reshape(2, HEAD_DIM // 2, 1, 2 * PAGE_SIZE), ((0, 0), (0, 0), (0, PE_ROWS - 1), (0, 0)))
    b1 = jnp.tile(cmp_b1, (1, bpp)).reshape(2, 1, bpp * hid)
    w2bd = jnp.einsum('khd,nm->knhmd', cmp_w2, eye).reshape(2, bpp * hid, bpp * HEAD_DIM).astype(MXU_DTYPE)
    b2 = jnp.tile(cmp_b2, (1, bpp)).reshape(2, 1, bpp * HEAD_DIM)
    rows = N_KV * n_pages
    const = lambda a: pl.BlockSpec(a.shape, lambda b, c, pt: (0,) * a.ndim, pipeline_mode=pl.Buffered(1))
    out = pl.pallas_call(
        functools.partial(_compress_kernel, n_pages=n_pages, paged=paged),
        grid_spec=pltpu.PrefetchScalarGridSpec(
            num_scalar_prefetch=1,
            grid=(n_seq, n_chunks),
            in_specs=[pl.BlockSpec(memory_space=pl.ANY)] + [const(a) for a in (pe_rows, w1bd, b1, w2bd, b2)],
            out_specs=pl.BlockSpec((1, 1, 2, rows, bpp * HEAD_DIM), lambda b, c, pt: (b, c, 0, 0, 0)),
            scratch_shapes=[pltpu.VMEM((2, page_rows, n_pages, PAGE_SIZE), F32), pltpu.SemaphoreType.DMA((2,))]),
        out_shape=jax.ShapeDtypeStruct((n_seq, n_chunks, 2, rows, bpp * HEAD_DIM), F32),
        compiler_params=_params(("arbitrary", "arbitrary")),
        name="compress_pages",
    )(page_table, pages, pe_rows, w1bd, b1, w2bd, b2)
    if raw_layout:
        return out
    out = out.reshape(n_seq, n_chunks, 2, N_KV, n_pages, bpp, HEAD_DIM)
    return out.transpose(0, 2, 3, 1, 4, 5, 6).reshape(n_seq, 2, N_KV, pps * bpp, HEAD_DIM)


WIN_SUBS = WINDOW // LANES + 1
CMP_LOOKBACK = (FAR_DIST + BLK - 1) // BLK + 1
CMP_BAND = -(-(CMP_LOOKBACK + Q_BLOCK // BLK + 6) // 8) * 8


def _attn_prompt_kernel(rel_ref, qt_ref, gate_ref, kc_ref, vct_ref, sk_ref, svt_ref, wk_ref, wvt_ref, tb_ref, o_ref,
                        sel_ref, lc_ref, s_ref, p_ref, *, group, n_cmp, n_sub):
    g = pl.program_id(1)
    i = pl.program_id(2)
    qt = jnp.concatenate([qt_ref[0, r] for r in range(group)], axis=1)
    lanes = lambda x, r: x[:, r * Q_BLOCK:(r + 1) * Q_BLOCK]

    nl = group * Q_BLOCK
    key = lax.broadcasted_iota(jnp.int32, (LANES, Q_BLOCK), 0)
    qpos = i * Q_BLOCK + lax.broadcasted_iota(jnp.int32, (LANES, Q_BLOCK), 1)
    zero_rows = jnp.zeros((LANES - HEAD_DIM, nl), MXU_DTYPE)
    qt_plain = jnp.concatenate([qt, zero_rows], axis=0)

    def sub_keys(k_ref, t):
        off = pl.multiple_of(jnp.clip(t, 0, n_sub - 1) * LANES, LANES)
        return k_ref[0, 0, 0, pl.ds(off, LANES), :]

    def with_bias(s, dl, madd=None):
        cols = [lanes(s, r) + tb_ref[r, dl] for r in range(group)]
        if madd is not None:
            cols = [c + madd for c in cols]
        return jnp.concatenate(cols, axis=1)

    lc_ref[...] = _mm(kc_ref[0, 0], qt)
    c0 = pl.multiple_of(jnp.clip((i * (Q_BLOCK // BLK) - CMP_LOOKBACK) // 8 * 8, 0, n_cmp - CMP_BAND), 8)
    row_b = c0 + lax.broadcasted_iota(jnp.int32, (CMP_BAND, Q_BLOCK), 0)
    dist_b = i * Q_BLOCK + lax.broadcasted_iota(jnp.int32, (CMP_BAND, Q_BLOCK), 1) - (row_b * BLK + BLK - 1)
    bkt = _bucket_of(dist_b)
    band = []
    for r in range(group):
        h = g * group + r
        far = rel_ref[NUM_BUCKETS - 1, h]
        bias = jnp.full((CMP_BAND, Q_BLOCK), (rel_ref[0, h] - far) * LOG2E, F32)
        for j in range(1, NUM_BUCKETS - 1):
            bias = jnp.where(bkt == j, (rel_ref[j, h] - far) * LOG2E, bias)
        band.append(jnp.where(bkt == NUM_BUCKETS - 1, 0.0, bias))
    lc_ref[pl.ds(c0, CMP_BAND), :] = lc_ref[pl.ds(c0, CMP_BAND), :] + jnp.concatenate(band, axis=1)
    blk = lax.broadcasted_iota(jnp.int32, (n_cmp, Q_BLOCK), 0)
    qpos_c = i * Q_BLOCK + lax.broadcasted_iota(jnp.int32, (n_cmp, Q_BLOCK), 1)
    ok_c = qpos_c >= blk * BLK + BLK - 1
    ok_f = ok_c.astype(F32)
    imp = jnp.zeros((n_cmp, Q_BLOCK), F32)
    pcs = []
    for r in range(group):
        lg = jnp.where(ok_c, lc_ref[:, r * Q_BLOCK:(r + 1) * Q_BLOCK], NEG_INF)
        p = jnp.exp2(lg - jnp.max(lg, axis=0, keepdims=True)) * ok_f
        pc = p * (1.0 / jnp.maximum(jnp.sum(p, axis=0, keepdims=True), 1e-30))
        imp = imp + pc
        pcs.append(pc.astype(MXU_DTYPE))
    o_c = _mm(vct_ref[0, 0], jnp.concatenate(pcs, axis=1))

    parts = []
    for u in range(WIN_SUBS):
        t = i - (WIN_SUBS - 1) + u
        dw = qpos - (t * LANES + key)
        madd = jnp.where((dw >= 0) & (dw < WINDOW) & (t >= 0), 0.0, NEG_INF)
        parts.append(with_bias(_mm(sub_keys(wk_ref, t), qt_plain), WIN_SUBS - 1 - u, madd))
    sw = jnp.concatenate(parts, axis=0)
    pw = jnp.exp2(sw - jnp.max(sw, axis=0, keepdims=True))
    acc_w = jnp.zeros((HEAD_DIM, nl), F32)
    for u in range(WIN_SUBS):
        off = pl.multiple_of(jnp.maximum(i - (WIN_SUBS - 1) + u, 0) * LANES, LANES)
        acc_w = acc_w + _mm(wvt_ref[0, 0, 0, :, pl.ds(off, LANES)], pw[u * LANES:(u + 1) * LANES].astype(MXU_DTYPE))
    o_w = acc_w * (1.0 / jnp.sum(pw, axis=0, keepdims=True))

    cur = qpos_c // BLK
    forced = (blk == 0) | (blk == cur) | (blk == cur - 1)
    score = jnp.where(forced, FORCED_SCORE, jnp.where(blk <= cur, imp, -1.0))
    sel = jnp.zeros((n_cmp, Q_BLOCK), F32)
    for _ in range(min(N_SEL, n_cmp)):
        mx = jnp.max(score, axis=0, keepdims=True)
        idx = jnp.min(jnp.where(score == mx, blk, n_cmp), axis=0, keepdims=True)
        hit = blk == idx
        sel = jnp.where(hit, 1.0, sel)
        score = jnp.where(hit, -2.0, score)
    sel_ref[...] = (sel - 1.0) * (-NEG_INF)

    def step_logits(step):
        mrows = sel_ref[pl.ds(pl.multiple_of(step * STEP_BLKS, STEP_BLKS), STEP_BLKS), :]
        mrows = jnp.concatenate([mrows] * group, axis=1)
        pad = jnp.zeros((LANES - HEAD_DIM - STEP_BLKS, nl), F32)
        q_aug = jnp.concatenate([qt, jnp.concatenate([mrows, pad], axis=0).astype(MXU_DTYPE)], axis=0)
        return [_mm(sub_keys(sk_ref, step * STEP_SUBS + u), q_aug) for u in range(STEP_SUBS)]

    step_keys = STEP_SUBS * LANES

    def qk_store(step, kind):
        cm = jnp.full((1, nl), NEG_INF, F32)
        for u, s in enumerate(step_logits(step)):
            t = step * STEP_SUBS + u
            if kind == 'near':
                s = with_bias(s, jnp.clip(i - t, 0, N_BIAS_TILES))
            elif kind == 'diag':
                s = with_bias(s, jnp.clip(i - t, 0, N_BIAS_TILES), jnp.where(t * LANES + key <= qpos, 0.0, NEG_INF))
            s_ref[pl.ds(pl.multiple_of(t * LANES, LANES), LANES), :] = s
            cm = jnp.maximum(cm, jnp.max(s, axis=0, keepdims=True))
        return cm

    def values(step):
        off = pl.multiple_of(jnp.maximum(step, 0) * step_keys, step_keys)
        return svt_ref[0, 0, 0, :, pl.ds(off, step_keys)]

    def softmax_step(step, m, l, cm):
        m_new = jnp.maximum(m, cm)
        a = jnp.exp2(m - m_new)
        p = jnp.exp2(s_ref[pl.ds(pl.multiple_of(step * step_keys, step_keys), step_keys), :] - m_new)
        return m_new, a * l + jnp.sum(p, axis=0, keepdims=True), a, p.astype(MXU_DTYPE)

    def pipe_step(j, carry, kind):
        m, l, acc, cm, a_prev = carry
        acc = a_prev * acc + _mm(values(j - 1), p_ref[...])
        m, l, a, p = softmax_step(j, m, l, cm)
        p_ref[...] = p
        return m, l, acc, qk_store(j + 1, kind), a

    last = i // STEP_SUBS
    n_far = jnp.maximum((i - (N_BIAS_TILES + STEP_SUBS - 1)) // STEP_SUBS + 1, 0)
    p_ref[...] = jnp.zeros(p_ref.shape, p_ref.dtype)
    carry = (jnp.full((1, nl), NEG_INF, F32), jnp.zeros((1, nl), F32), jnp.zeros((HEAD_DIM, nl), F32),
             qk_store(0, 'diag'), jnp.ones((1, nl), F32))
    n_a = jnp.maximum(n_far - 1, 0)
    carry = lax.fori_loop(0, n_a, functools.partial(pipe_step, kind='far'), carry)
    carry = lax.fori_loop(n_a, last - 1, functools.partial(pipe_step, kind='near'), carry)
    m, l_s, acc_s, cm, a_prev = lax.fori_loop(jnp.maximum(last - 1, 0), last,
                                              functools.partial(pipe_step, kind='diag'), carry)
    acc_s = a_prev * acc_s + _mm(values(last - 1), p_ref[...])
    _, l_s, a, p = softmax_step(last, m, l_s, cm)
    acc_s = a * acc_s + _mm(values(last), p)
    o_s = acc_s * (1.0 / l_s)

    gt = gate_ref[0]
    for r in range(group):
        o = (gt[3 * r:3 * r + 1, :] * lanes(o_c, r) + gt[3 * r + 1:3 * r + 2, :] * lanes(o_s, r)
             + gt[3 * r + 2:3 * r + 3, :] * lanes(o_w, r))
        o_ref[0, r] = o.astype(o_ref.dtype)


def _attn_prompt(q_t, gates_t, kvc, k_rows, v_t, bias_tiles, rel_bias):
    bsz, nq, seq = q_t.shape
    n_heads = nq // HEAD_DIM
    group = n_heads // N_KV
    n_cmp = kvc.shape[3]
    n_sub = seq // LANES
    assert seq % Q_BLOCK == 0 and Q_BLOCK == LANES and n_cmp * BLK == seq and n_sub % STEP_SUBS == 0
    assert WIN_SUBS - 1 <= N_BIAS_TILES and n_cmp >= CMP_BAND and n_cmp % 8 == 0
    kc = kvc[:, 0].astype(MXU_DTYPE)
    vct = jnp.swapaxes(kvc[:, 1], -1, -2).astype(MXU_DTYPE)
    q4 = q_t.reshape(bsz, n_heads, HEAD_DIM, seq)
    v4 = v_t.reshape(bsz, 2, N_KV, HEAD_DIM, seq)
    k_spec = lambda s: pl.BlockSpec((1, 1, 1, seq, LANES), lambda b, g, i: (b, s, g, 0, 0))
    v_spec = lambda s: pl.BlockSpec((1, 1, 1, HEAD_DIM, seq), lambda b, g, i: (b, s, g, 0, 0))
    o = pl.pallas_call(
        functools.partial(_attn_prompt_kernel, group=group, n_cmp=n_cmp, n_sub=n_sub),
        grid=(bsz, N_KV, seq // Q_BLOCK),
        in_specs=[pl.BlockSpec(memory_space=pltpu.SMEM),
                  pl.BlockSpec((1, group, HEAD_DIM, Q_BLOCK), lambda b, g, i: (b, g, 0, i)),
                  pl.BlockSpec((1, GATE_ROWS, Q_BLOCK), lambda b, g, i: (b, g, i)),
                  pl.BlockSpec((1, 1, n_cmp, HEAD_DIM), lambda b, g, i: (b, g, 0, 0)),
                  pl.BlockSpec((1, 1, HEAD_DIM, n_cmp), lambda b, g, i: (b, g, 0, 0)),
                  k_spec(0), v_spec(0), k_spec(1), v_spec(1),
                  pl.BlockSpec((group, N_BIAS_TILES + 1, LANES, Q_BLOCK), lambda b, g, i: (g, 0, 0, 0))],
        out_specs=pl.BlockSpec((1, group, HEAD_DIM, Q_BLOCK), lambda b, g, i: (b, g, 0, i)),
        out_shape=jax.ShapeDtypeStruct((bsz, n_heads, HEAD_DIM, seq), MXU_DTYPE),
        scratch_shapes=[pltpu.VMEM((n_cmp, Q_BLOCK), F32), pltpu.VMEM((n_cmp, group * Q_BLOCK), F32),
                        pltpu.VMEM((seq, group * Q_BLOCK), F32),
                        pltpu.VMEM((STEP_SUBS * LANES, group * Q_BLOCK), MXU_DTYPE)],
        compiler_params=_params(("parallel", "parallel", "arbitrary")),
        name="attn_prompt",
    )(rel_bias, q4, gates_t, kc, vct, k_rows, v4, k_rows, v4, bias_tiles)
    return o.reshape(bsz, nq, seq)


SLC_PAGES = 32
NEW_PAD = 16


def _bias_rows(trow_ref, dist):
    bkt = _bucket_of(dist)
    bias = jnp.broadcast_to(trow_ref[0:1, :], dist.shape)
    for j in range(1, NUM_BUCKETS):
        bias = jnp.where(bkt == j, trow_ref[j:j + 1, :], bias)
    return bias


def _attn_sample_kernel(pt_ref, *refs, n_pages, past_len, n_new, n_cmp, n_blk):
    del pt_ref
    page_refs = refs[:n_pages]
    (qbd_ref, gate_ref, trow_ref, cmp_ref, knew_ref, vnewt_ref, wkt_ref, wvt_ref, wknew_ref, wvnewt_ref,
     o_ref, sel_ref, oc_ref, m_ref, l_ref, acc_ref, s_ref) = refs[n_pages:]
    c = pl.program_id(1)
    qbd = qbd_ref[0]
    gd = qbd.shape[0]
    bpp = PAGE_SIZE // BLK
    npb = past_len // BLK
    pps = past_len // PAGE_SIZE

    def qpos_like(shape):
        return past_len + lax.broadcasted_iota(jnp.int32, shape, 1) % n_new

    def block_of_row(row):
        return jnp.where(row < n_cmp, (row % pps) * bpp + row // pps, row)

    @pl.when(c == 0)
    def _():
        ppc = cmp_ref.shape[3] // N_KV
        lane_g = (lax.broadcasted_iota(jnp.int32, (pps, LANES), 1) // n_new) % N_KV

        def slab(kv, g):
            return jnp.concatenate([cmp_ref[0, ch, kv, g * ppc:(g + 1) * ppc, :] for ch in range(cmp_ref.shape[1])],
                                   axis=0).astype(MXU_DTYPE)

        zero_q = jnp.zeros((HEAD_DIM, LANES), MXU_DTYPE)
        parts = []
        for n in range(bpp):
            s_n = jnp.zeros((pps, LANES), F32)
            for g in range(N_KV):
                q_n = jnp.concatenate([qbd[g * HEAD_DIM:(g + 1) * HEAD_DIM] if t == n else zero_q for t in range(bpp)],
                                      axis=0)
                s_n = jnp.where(lane_g == g, _mm(slab(0, g), q_n), s_n)
            parts.append(s_n)
        blk = block_of_row(lax.broadcasted_iota(jnp.int32, (n_cmp, LANES), 0))
        dist = qpos_like((n_cmp, LANES)) - (blk * BLK + BLK - 1)
        ok = dist >= 0
        lg = jnp.where(ok, jnp.concatenate(parts, axis=0) + _bias_rows(trow_ref, dist), NEG_INF)
        p = jnp.exp(lg - jnp.max(lg, axis=0, keepdims=True)) * ok.astype(F32)
        pc = p / jnp.maximum(jnp.sum(p, axis=0, keepdims=True), 1e-30)
        pc_b = pc.astype(MXU_DTYPE)
        o_groups = []
        for g in range(N_KV):
            v_g = slab(1, g)
            o_g = jnp.zeros((HEAD_DIM, LANES), F32)
            for n in range(bpp):
                o_g = o_g + _tn(v_g, pc_b[n * pps:(n + 1) * pps])[n * HEAD_DIM:(n + 1) * HEAD_DIM]
            o_groups.append(o_g)
        oc_ref[...] = jnp.concatenate(o_groups, axis=0)
        quarter = LANES // 4
        imp = pc + pltpu.roll(pc, quarter, 1) + pltpu.roll(pc, 2 * quarter, 1) + pltpu.roll(pc, 3 * quarter, 1)
        nbp = sel_ref.shape[0]
        imp = jnp.concatenate([imp, jnp.zeros((nbp - n_cmp, LANES), F32)], axis=0)
        blk = block_of_row(lax.broadcasted_iota(jnp.int32, (nbp, LANES), 0))
        cur = qpos_like((nbp, LANES)) // BLK
        forced = (blk == 0) | (blk == cur) | (blk == cur - 1)
        score = jnp.where(forced, FORCED_SCORE, jnp.where(blk <= cur, imp, -1.0))
        score = jnp.where(blk >= n_blk, -2.0, score)
        sel = jnp.zeros((nbp, LANES), F32)
        for _ in range(min(N_SEL, n_blk)):
            mx = jnp.max(score, axis=0, keepdims=True)
            idx = jnp.min(jnp.where(score == mx, blk, nbp), axis=0, keepdims=True)
            hit = blk == idx
            sel = jnp.where(hit, 1.0, sel)
            score = jnp.where(hit, -3.0, score)
        sel_ref[...] = sel
        m_ref[...] = jnp.full(m_ref.shape, NEG_INF, F32)
        l_ref[...] = jnp.zeros(l_ref.shape, F32)
        acc_ref[...] = jnp.zeros(acc_ref.shape, F32)

    tok = lax.broadcasted_iota(jnp.int32, (PAGE_SIZE, LANES), 0)
    qpos_p = qpos_like((PAGE_SIZE, LANES))

    far_pages = min(max((past_len - FAR_DIST - (PAGE_SIZE - 1)) // PAGE_SIZE + 1, 0), past_len // PAGE_SIZE)
    assert past_len // PAGE_SIZE - far_pages <= n_pages
    first_near = far_pages - (past_len // PAGE_SIZE - n_pages)

    def page_logits(near):
        for j in range(n_pages):
            pg = c * n_pages + j
            st = _tn(page_refs[j][0, 0].reshape(gd, PAGE_SIZE).astype(MXU_DTYPE), qbd)
            if near and j >= first_near:
                st = st + _bias_rows(trow_ref, qpos_p - (pg * PAGE_SIZE + tok))
            else:
                st = st + trow_ref[NUM_BUCKETS - 1:NUM_BUCKETS, :]
            chosen = jnp.broadcast_to(sel_ref[pl.ds((bpp - 1) * pps + pg, 1), :], (PAGE_SIZE, LANES))
            for t in reversed(range(bpp - 1)):
                chosen = jnp.where(tok < (t + 1) * BLK, sel_ref[pl.ds(t * pps + pg, 1), :], chosen)
            s_ref[j * PAGE_SIZE:(j + 1) * PAGE_SIZE, :] = jnp.where(chosen > 0.5, st, NEG_INF)

    is_last = c == pl.num_programs(1) - 1
    pl.when(jnp.logical_not(is_last))(functools.partial(page_logits, False))
    pl.when(is_last)(functools.partial(page_logits, True))
    s_all = s_ref[...]
    m_old = m_ref[...]
    m_new = jnp.maximum(m_old, jnp.max(s_all, axis=0, keepdims=True))
    a = jnp.exp(m_old - m_new)
    p = jnp.exp(s_all - m_new)
    acc = acc_ref[...] * a
    for j in range(n_pages):
        vt = page_refs[j][0, 1].reshape(gd, PAGE_SIZE).astype(MXU_DTYPE)
        acc = acc + _mm(vt, p[j * PAGE_SIZE:(j + 1) * PAGE_SIZE].astype(MXU_DTYPE))
    acc_ref[...] = acc
    l_ref[...] = a * l_ref[...] + jnp.sum(p, axis=0, keepdims=True)
    m_ref[...] = m_new

    @pl.when(c == pl.num_programs(1) - 1)
    def _():
        tok_n = lax.broadcasted_iota(jnp.int32, (NEW_PAD, LANES), 0)
        dist_n = qpos_like((NEW_PAD, LANES)) - (past_len + tok_n)
        live_n = (tok_n < n_new) & (dist_n >= 0)
        bias_n = _bias_rows(trow_ref, dist_n)
        chosen_n = sel_ref[pl.ds(npb, 1), :] > 0.5
        s_n = jnp.where(live_n & chosen_n, _mm(knew_ref[0], qbd) + bias_n, NEG_INF)
        m_old = m_ref[...]
        m_fin = jnp.maximum(m_old, jnp.max(s_n, axis=0, keepdims=True))
        a = jnp.exp(m_old - m_fin)
        p_n = jnp.exp(s_n - m_fin)
        acc_s = acc_ref[...] * a + _mm(vnewt_ref[0], p_n.astype(MXU_DTYPE))
        o_s = acc_s / (a * l_ref[...] + jnp.sum(p_n, axis=0, keepdims=True))
        wb = wkt_ref.shape[-1]
        tok_w = lax.broadcasted_iota(jnp.int32, (wb, LANES), 0)
        kwpos = past_len - wb + tok_w
        dw = qpos_like((wb, LANES)) - kwpos
        ok_w = (dw >= 0) & (dw < WINDOW) & (kwpos >= 0)
        s_w = _tn(wkt_ref[0, 0].reshape(gd, wb).astype(MXU_DTYPE), qbd) + _bias_rows(trow_ref, dw)
        s_w = jnp.where(ok_w, s_w, NEG_INF)
        s_wn = jnp.where(live_n & (dist_n < WINDOW), _mm(wknew_ref[0], qbd) + bias_n, NEG_INF)
        m_w = jnp.maximum(jnp.max(s_w, axis=0, keepdims=True), jnp.max(s_wn, axis=0, keepdims=True))
        p_w = jnp.exp(s_w - m_w)
        p_wn = jnp.exp(s_wn - m_w)
        o_w = (_mm(wvt_ref[0, 0].reshape(gd, wb).astype(MXU_DTYPE), p_w.astype(MXU_DTYPE))
               + _mm(wvnewt_ref[0], p_wn.astype(MXU_DTYPE)))
        o_w = o_w / (jnp.sum(p_w, axis=0, keepdims=True) + jnp.sum(p_wn, axis=0, keepdims=True))
        gt = gate_ref[0]
        tot = gt[0:1, :] * oc_ref[...] + gt[1:2, :] * o_s + gt[2:3, :] * o_w
        lane_g = (lax.broadcasted_iota(jnp.int32, (HEAD_DIM, LANES), 1) // n_new) % N_KV
        out = jnp.zeros((HEAD_DIM, LANES), F32)
        for g in range(N_KV):
            out = out + jnp.where(lane_g == g, tot[g * HEAD_DIM:(g + 1) * HEAD_DIM, :], 0.0)
        o_ref[0] = out


def _attn_sample(q, gates, cmp_raw, kv_new_t, slc_pages, page_table, win_t, rel_bias):
    bsz, n_new, nq = q.shape
    n_heads = nq // HEAD_DIM
    group = n_heads // N_KV
    pps = page_table.shape[1]
    past_len = pps * PAGE_SIZE
    n_cmp = cmp_raw.shape[1] * (cmp_raw.shape[3] // N_KV) * (PAGE_SIZE // BLK)
    n_blk = past_len // BLK + -(-n_new // BLK)
    nbp = -(-n_blk // 8) * 8
    gd = N_KV * HEAD_DIM
    wb = win_t.shape[-1]
    n_pages = min(SLC_PAGES, pps)
    assert n_heads * n_new == LANES and group == 4 and n_new <= min(BLK, NEW_PAD) and pps % n_pages == 0
    assert n_cmp == past_len // BLK and n_cmp % 8 == 0
    q5 = q.reshape(bsz, n_new, N_KV, group, HEAD_DIM)
    qbd = jnp.einsum('hg,bqgrd->bhdrgq', jnp.eye(N_KV, dtype=q.dtype), q5).reshape(bsz, gd, LANES).astype(MXU_DTYPE)
    g5 = gates[:, :, :3 * n_heads].reshape(bsz, n_new, N_KV, group, 3)
    gate_t = jnp.pad(g5.transpose(0, 4, 3, 2, 1).reshape(bsz, 3, LANES), ((0, 0), (0, 5), (0, 0)))
    trow = jnp.broadcast_to(rel_bias.reshape(NUM_BUCKETS, N_KV, group).transpose(0, 2, 1)[..., None],
                            (NUM_BUCKETS, group, N_KV, n_new)).reshape(NUM_BUCKETS, LANES)
    new = kv_new_t.reshape(6, gd, bsz, n_new).transpose(0, 2, 1, 3)
    new = jnp.pad(new, ((0, 0), (0, 0), (0, 0), (0, NEW_PAD - n_new))).astype(MXU_DTYPE)
    k_rows = lambda s: jnp.swapaxes(new[s], 1, 2)
    page_spec = lambda j: pl.BlockSpec((1, 2, N_KV, HEAD_DIM, PAGE_SIZE),
                                       lambda b, c, pt: (pt[b, c * n_pages + j], 0, 0, 0, 0))
    per_b = lambda a: pl.BlockSpec((1,) + a.shape[1:], lambda b, c, pt: (b,) + (0,) * (a.ndim - 1))
    win_spec = lambda s: pl.BlockSpec((1, 1, N_KV, HEAD_DIM, wb), lambda b, c, pt: (b, s, 0, 0, 0))
    smalls = (qbd, gate_t)
    o_t = pl.pallas_call(
        functools.partial(_attn_sample_kernel, n_pages=n_pages, past_len=past_len, n_new=n_new, n_cmp=n_cmp,
                          n_blk=n_blk),
        grid_spec=pltpu.PrefetchScalarGridSpec(
            num_scalar_prefetch=1,
            grid=(bsz, pps // n_pages),
            in_specs=[page_spec(j) for j in range(n_pages)]
            + [per_b(qbd), per_b(gate_t), pl.BlockSpec(trow.shape, lambda b, c, pt: (0, 0)),
               per_b(cmp_raw), per_b(k_rows(2)), per_b(new[3]), win_spec(0), win_spec(1),
               per_b(k_rows(4)), per_b(new[5])],
            out_specs=pl.BlockSpec((1, HEAD_DIM, LANES), lambda b, c, pt: (b, 0, 0)),
            scratch_shapes=[pltpu.VMEM((nbp, LANES), F32), pltpu.VMEM((gd, LANES), F32), pltpu.VMEM((1, LANES), F32),
                            pltpu.VMEM((1, LANES), F32), pltpu.VMEM((gd, LANES), F32),
                            pltpu.VMEM((n_pages * PAGE_SIZE, LANES), F32)]),
        out_shape=jax.ShapeDtypeStruct((bsz, HEAD_DIM, LANES), F32),
        compiler_params=_params(("parallel", "arbitrary")),
        name="attn_sample",
    )(page_table, *([slc_pages] * n_pages), qbd, gate_t, trow, cmp_raw, k_rows(2), new[3], win_t, win_t,
      k_rows(4), new[5])
    del smalls
    return o_t.reshape(bsz, HEAD_DIM, group, N_KV, n_new).transpose(0, 4, 3, 2, 1).reshape(bsz, n_new, nq)


def _tokens_last(x):
    nd = x.ndim
    return jnp.transpose(x, tuple(range(nd - 4)) + (nd - 3, nd - 2, nd - 1, nd - 4))


def _tokens_first(x):
    nd = x.ndim
    return jnp.transpose(x, tuple(range(nd - 4)) + (nd - 1, nd - 4, nd - 3, nd - 2))


def kernel(x_prompt, x_sample, cache_kv_cmp, cache_kv_slc, cache_win, state_conv, page_table, ln_g, ln_b, sg_w_in, sg_b_in, sg_ln_g, sg_ln_b, sg_w_s, sg_b_s, sg_w_out, sg_b_out, ffn_w_up, ffn_b_up, ffn_w_dw, ffn_b_dw, ffn_w_down, ffn_b_down, kv_w, cmp_pe, cmp_w1, cmp_b1, cmp_w2, cmp_b2, nsa_w_qg, nsa_b_qg, nsa_w_o, nsa_b_o, rel_bias):
    depth = ln_g.shape[0]
    n_a = sg_w_in.shape[0]
    assert depth == DEPTH
    bsz, seq, d_model = x_prompt.shape
    dbsz, n_new, _ = x_sample.shape
    d_ff = ffn_w_down.shape[1]
    kvd = (2, N_KV, HEAD_DIM)
    gd = N_KV * HEAD_DIM
    cmp_w = (cmp_pe, cmp_w1, cmp_b1, cmp_w2, cmp_b2)

    def trunk(x, seq_len, conv_state, attend_layer, emit_v):
        n_seq = x.shape[0] // seq_len
        new_conv, sg_v, kv_t, ctx = [], [], None, None
        for l in range(depth):
            if l < n_a:
                x, v = _sgu_layer(x, seq_len, sg_w_in[l], sg_b_in[l], sg_ln_g[l], sg_ln_b[l], sg_w_s[l], sg_b_s[l],
                                  sg_w_out[l], sg_b_out[l], ln_g[l, 0], ln_b[l, 0], emit_v)
                if emit_v:
                    sg_v.append(v.reshape(n_seq, seq_len, -1))
            else:
                i = l - n_a
                x = attend_layer(x, ctx, nsa_w_qg[i], nsa_b_qg[i], nsa_w_o[i], nsa_b_o[i], ln_g[l, 0], ln_b[l, 0])
            x, cs = _ffn_layer(x, seq_len, conv_state[l], ffn_w_up[l], ffn_b_up[l], ffn_w_dw[l], ffn_b_dw[l],
                               ffn_w_down[l], ffn_b_down[l], ln_g[l, 1], ln_b[l, 1])
            new_conv.append(cs)
            if l == n_a - 1:
                kv_t, ctx = attend_layer.prepare(x)
        return x, jnp.stack(new_conv), (jnp.stack(sg_v) if sg_v else None), kv_t

    bias_tiles = _bias_tiles(rel_bias)

    def prompt_prepare(x):
        *kv_t, v_t, k_rows = _kv_proj(x, bsz, kv_w)
        kvc = _compress_pages(kv_t[0], None, *cmp_w)
        return kv_t, (kvc, k_rows, v_t)

    def prompt_attend(x, ctx, w_qg, b_qg, w_o, b_o, g, b):
        kvc, k_rows, v_t = ctx
        q_t, gates_t = _qg_proj_t(x, bsz, w_qg, b_qg)
        o_t = _attn_prompt(q_t, gates_t, kvc, k_rows, v_t, bias_tiles, rel_bias)
        return _oproj_t(o_t, x, w_o, b_o, g, b)

    prompt_attend.prepare = prompt_prepare
    zeros_state = jnp.zeros((depth, bsz, CONV_W - 1, d_ff), x_prompt.dtype)
    y_p, conv_p, _, kv_p = trunk(x_prompt.reshape(bsz * seq, d_model), seq, zeros_state, prompt_attend, False)
    kv_p = [a.reshape(bsz, 2, N_KV, HEAD_DIM, seq) for a in kv_p]

    cmp_pages = _tokens_last(cache_kv_cmp)
    slc_pages = _tokens_last(cache_kv_slc)
    win_t = _tokens_last(cache_win)

    def sample_prepare(x):
        *kv_t, _, _ = _kv_proj(x, 1, kv_w)
        kv_t = jnp.concatenate(kv_t, axis=1).reshape(6 * gd, dbsz * n_new)
        kvc = _compress_pages(cmp_pages, page_table, *cmp_w, raw_layout=True)
        return kv_t, (kvc, kv_t)

    def sample_attend(x, ctx, w_qg, b_qg, w_o, b_o, g, b):
        kvc, kv_new_t = ctx
        q, gates = _qg_proj_flat(x, w_qg, b_qg)
        o = _attn_sample(q.reshape(dbsz, n_new, -1), gates.reshape(dbsz, n_new, -1), kvc, kv_new_t, slc_pages,
                         page_table, win_t, rel_bias)
        return _oproj_flat(o.reshape(dbsz * n_new, -1), x, w_o, b_o, g, b)

    sample_attend.prepare = sample_prepare
    assert n_new < BLK
    y_s, conv_s, sgv_s, kv_s = trunk(x_sample.reshape(dbsz * n_new, d_model), n_new, state_conv, sample_attend, True)
    kv_s = kv_s.reshape(3, 2, N_KV, HEAD_DIM, dbsz, n_new)

    win_len = min(WINDOW, seq)
    out_p = [_tokens_first(kv_p[s]) for s in range(3)]
    out_s = [jnp.transpose(kv_s[s], (3, 4, 0, 1, 2)) for s in range(3)]
    return (y_p.reshape(bsz, seq, d_model), y_s.reshape(dbsz, n_new, d_model),
            out_p[0], out_p[1], out_p[2][:, seq - win_len:], conv_p,
            out_s[0], out_s[1], out_s[2], conv_s, sgv_s)
```

```python
import functools
import math

import numpy as np
import jax
import jax.numpy as jnp
from jax import lax
from jax.experimental import pallas as pl
from jax.experimental.pallas import tpu as pltpu

N_SG_GROUPS = 8
CHUNK = 128
CONV_W = 3
HEAD_DIM = 64
N_KV = 4
BLK = 64
N_SEL = 16
WINDOW = 512
NUM_BUCKETS = 32
REL_MAX_DIST = 1024
Q_BLOCK = 128
PAGE_SIZE = 128
DEPTH = 2
ALPHA = (2 * DEPTH) ** 0.25
LN_EPS = 1e-5
NEG_INF = -1e30
FORCED_SCORE = 1e9
LOG2E = math.log2(math.e)

MXU_DTYPE = jnp.bfloat16
F32 = jnp.float32
LANES = 128
ROW_TILE = 256
SGU_ROW_TILE = 512
FFN_ROW_TILE = 512
VMEM_LIMIT = 56 * 1024 * 1024


def _bucket_thresholds():
    d = np.arange(0, 4 * REL_MAX_DIST)
    max_exact = NUM_BUCKETS // 2
    nf = np.maximum(d, 1).astype(np.float32)
    large = max_exact + (np.log(nf / np.float32(max_exact)) / np.float32(math.log(REL_MAX_DIST / max_exact))
                         * np.float32(NUM_BUCKETS - max_exact)).astype(np.int32)
    b = np.where(d < max_exact, d, np.minimum(large, NUM_BUCKETS - 1))
    assert np.all(np.diff(b) >= 0) and b[-1] == NUM_BUCKETS - 1
    return [int(np.nonzero(b >= j)[0][0]) for j in range(NUM_BUCKETS)]


BUCKET_THR = _bucket_thresholds()
FAR_DIST = BUCKET_THR[-1]
N_BIAS_TILES = -(-(FAR_DIST + LANES) // LANES)


def _gelu(x):
    return 0.5 * x * (1.0 + lax.erf(x * (1.0 / math.sqrt(2.0))))


def _layer_norm(x, g, b):
    mu = jnp.mean(x, axis=-1, keepdims=True)
    xc = x - mu
    var = jnp.mean(xc * xc, axis=-1, keepdims=True)
    return xc * lax.rsqrt(var + LN_EPS) * g + b


def _mm(a, b):
    return jnp.dot(a, b, preferred_element_type=F32)


def _mm_nt(a, b):
    return lax.dot_general(a, b, (((1,), (1,)), ((), ())), preferred_element_type=F32)


def _tn(a, b):
    return lax.dot_general(a, b, (((0,), (0,)), ((), ())), preferred_element_type=F32)


def _const_spec(shape):
    nd = len(shape)
    return pl.BlockSpec(shape, lambda *_: (0,) * nd, pipeline_mode=pl.Buffered(1))


def _params(sem):
    return pltpu.CompilerParams(dimension_semantics=sem, vmem_limit_bytes=VMEM_LIMIT)


def _bucket_of(dist):
    b = jnp.zeros(dist.shape, jnp.int32)
    for j in range(1, NUM_BUCKETS):
        b = b + (dist >= BUCKET_THR[j]).astype(jnp.int32)
    return b


def _sgu_kernel(x_ref, win_ref, bin_ref, gv_ref, bv_ref, wg_ref, bs_ref, wout_ref, bout_ref, lng_ref, lnb_ref,
                o_ref, *maybe_v_ref, d_sg):
    x = x_ref[...]
    xb = x.astype(MXU_DTYPE)
    v = _gelu(_mm(xb, win_ref[:, d_sg:]) + bin_ref[:, d_sg:])
    vn = _layer_norm(v, gv_ref[...], bv_ref[...])
    if maybe_v_ref:
        maybe_v_ref[0][...] = vn
    cg = d_sg // N_SG_GROUPS
    mix = wg_ref.shape[1]
    acc = jnp.zeros(o_ref.shape, F32)
    for g in range(N_SG_GROUPS):
        lo, hi = g * cg, (g + 1) * cg
        vg = vn[:, lo:hi].astype(MXU_DTYPE)
        gate = jnp.concatenate([_mm(wg_ref[g], vg[r:r + mix]) + bs_ref[g] for r in range(0, x.shape[0], mix)], axis=0)
        u = _gelu(_mm(xb, win_ref[:, lo:hi]) + bin_ref[:, lo:hi])
        acc = acc + _mm((u * gate).astype(MXU_DTYPE), wout_ref[lo:hi, :])
    y = acc + bout_ref[...]
    o_ref[...] = _layer_norm(ALPHA * x + y, lng_ref[...], lnb_ref[...])


def _sgu_layer(x, seq_len, w_in, b_in, g_v, b_v, w_s, b_s, w_out, b_out, ln_g, ln_b, emit_v):
    t, d = x.shape
    d_sg = w_out.shape[0]
    cg = d_sg // N_SG_GROUPS
    cl = min(CHUNK, seq_len)
    tm = min(SGU_ROW_TILE, t)
    mix = min(ROW_TILE, tm)
    assert seq_len % cl == 0 and mix % cl == 0 and tm % mix == 0 and t % tm == 0
    w_tri = jnp.tril(w_s[:, :cl, :cl])
    eye = jnp.eye(mix // cl, dtype=w_s.dtype)
    wg = jnp.einsum('ab,gts->gatbs', eye, w_tri).reshape(N_SG_GROUPS, mix, mix).astype(MXU_DTYPE)
    bs = jnp.broadcast_to(jnp.tile(b_s[:, :cl], (1, mix // cl))[:, :, None], (N_SG_GROUPS, mix, cg))
    row = lambda a: a.reshape(1, -1)
    out_specs = [pl.BlockSpec((tm, d), lambda i: (i, 0))]
    out_shape = [jax.ShapeDtypeStruct((t, d), F32)]
    if emit_v:
        out_specs.append(pl.BlockSpec((tm, d_sg), lambda i: (i, 0)))
        out_shape.append(jax.ShapeDtypeStruct((t, d_sg), F32))
    outs = pl.pallas_call(
        functools.partial(_sgu_kernel, d_sg=d_sg),
        grid=(t // tm,),
        in_specs=[pl.BlockSpec((tm, d), lambda i: (i, 0)),
                  _const_spec((d, 2 * d_sg)), _const_spec((1, 2 * d_sg)), _const_spec((1, d_sg)), _const_spec((1, d_sg)),
                  _const_spec((N_SG_GROUPS, mix, mix)), _const_spec((N_SG_GROUPS, mix, cg)),
                  _const_spec((d_sg, d)), _const_spec((1, d)), _const_spec((1, d)), _const_spec((1, d))],
        out_specs=out_specs,
        out_shape=out_shape,
        compiler_params=_params(("parallel",)),
        name="sgu_layer",
    )(x, w_in.astype(MXU_DTYPE), row(b_in), row(g_v), row(b_v), wg, bs, w_out.astype(MXU_DTYPE), row(b_out),
      row(ln_g), row(ln_b))
    return outs[0], (outs[1] if emit_v else None)


def _ffn_tail(x, a, gt, s1, s2, wdw_ref, bdw_ref, wdown_ref, bdown_ref, lng_ref, lnb_ref, o_ref):
    c = bdw_ref[...] + s2 * wdw_ref[0:1, :] + s1 * wdw_ref[1:2, :] + a * wdw_ref[2:3, :]
    h = (_gelu(c) * gt).astype(MXU_DTYPE)
    y = _mm(h, wdown_ref[...]) + bdown_ref[...]
    o_ref[...] = _layer_norm(ALPHA * x + y, lng_ref[...], lnb_ref[...])


def _ffn_long_kernel(x_ref, xh_ref, st_ref, wup_ref, bup_ref, wdw_ref, bdw_ref, wdown_ref, bdown_ref, lng_ref, lnb_ref,
                     o_ref, conv_ref, *, seq_len, d_ff):
    tm = x_ref.shape[0]
    x = x_ref[...]
    xb = x.astype(MXU_DTYPE)
    a = _mm(xb, wup_ref[:, :d_ff]) + bup_ref[:, :d_ff]
    gt = _mm(xb, wup_ref[:, d_ff:]) + bup_ref[:, d_ff:]
    ah = _mm(xh_ref[...].astype(MXU_DTYPE), wup_ref[:, :d_ff]) + bup_ref[:, :d_ff]
    at_start = (pl.program_id(0) * tm) % seq_len == 0
    p1 = jnp.where(at_start, st_ref[0, 1:2, :], ah[7:8, :])
    p0 = jnp.where(at_start, st_ref[0, 0:1, :], ah[6:7, :])
    r = lax.broadcasted_iota(jnp.int32, a.shape, 0)
    s1 = jnp.where(r == 0, p1, pltpu.roll(a, 1, 0))
    s2 = jnp.where(r == 0, p0, jnp.where(r == 1, p1, pltpu.roll(a, 2, 0)))
    conv_ref[0] = a[tm - (CONV_W - 1):, :]
    _ffn_tail(x, a, gt, s1, s2, wdw_ref, bdw_ref, wdown_ref, bdown_ref, lng_ref, lnb_ref, o_ref)


def _ffn_short_kernel(x_ref, p0_ref, p1_ref, wup_ref, bup_ref, wdw_ref, bdw_ref, wdown_ref, bdown_ref, lng_ref, lnb_ref,
                      o_ref, a_ref, *, seq_len, d_ff):
    x = x_ref[...]
    xb = x.astype(MXU_DTYPE)
    a = _mm(xb, wup_ref[:, :d_ff]) + bup_ref[:, :d_ff]
    gt = _mm(xb, wup_ref[:, d_ff:]) + bup_ref[:, d_ff:]
    pos = lax.broadcasted_iota(jnp.int32, a.shape, 0) % seq_len
    s1 = jnp.where(pos == 0, p1_ref[...], pltpu.roll(a, 1, 0))
    s2 = jnp.where(pos == 0, p0_ref[...], jnp.where(pos == 1, p1_ref[...], pltpu.roll(a, 2, 0)))
    a_ref[...] = a
    _ffn_tail(x, a, gt, s1, s2, wdw_ref, bdw_ref, wdown_ref, bdown_ref, lng_ref, lnb_ref, o_ref)


def _ffn_layer(x, seq_len, conv_state, w_up, b_up, w_dw, b_dw, w_down, b_down, ln_g, ln_b):
    t, d = x.shape
    d_ff = w_down.shape[0]
    n_seq = t // seq_len
    tm = min(FFN_ROW_TILE, t)
    assert seq_len >= CONV_W - 1 and t % tm == 0
    row = lambda a: a.reshape(1, -1)
    weights = (w_up.astype(MXU_DTYPE), row(b_up), w_dw, row(b_dw), w_down.astype(MXU_DTYPE), row(b_down), row(ln_g), row(ln_b))
    w_specs = [_const_spec((d, 2 * d_ff)), _const_spec((1, 2 * d_ff)), _const_spec((CONV_W, d_ff)), _const_spec((1, d_ff)),
               _const_spec((d_ff, d)), _const_spec((1, d)), _const_spec((1, d)), _const_spec((1, d))]
    if seq_len % tm == 0:
        tiles_per_seq = seq_len // tm
        out, conv = pl.pallas_call(
            functools.partial(_ffn_long_kernel, seq_len=seq_len, d_ff=d_ff),
            grid=(t // tm,),
            in_specs=[pl.BlockSpec((tm, d), lambda i: (i, 0)),
                      pl.BlockSpec((8, d), lambda i: (jnp.maximum(i * (tm // 8) - 1, 0), 0)),
                      pl.BlockSpec((1, CONV_W - 1, d_ff), lambda i: (i // tiles_per_seq, 0, 0))] + w_specs,
            out_specs=[pl.BlockSpec((tm, d), lambda i: (i, 0)),
                       pl.BlockSpec((1, CONV_W - 1, d_ff), lambda i: (i // tiles_per_seq, 0, 0))],
            out_shape=[jax.ShapeDtypeStruct((t, d), F32), jax.ShapeDtypeStruct((n_seq, CONV_W - 1, d_ff), F32)],
            compiler_params=_params(("arbitrary",)),
            name="ffn_long",
        )(x, x, conv_state, *weights)
        return out, conv
    assert tm % seq_len == 0
    p0 = jnp.repeat(conv_state[:, 0, :], seq_len, axis=0)
    p1 = jnp.repeat(conv_state[:, 1, :], seq_len, axis=0)
    out, a = pl.pallas_call(
        functools.partial(_ffn_short_kernel, seq_len=seq_len, d_ff=d_ff),
        grid=(t // tm,),
        in_specs=[pl.BlockSpec((tm, d), lambda i: (i, 0)), pl.BlockSpec((tm, d_ff), lambda i: (i, 0)),
                  pl.BlockSpec((tm, d_ff), lambda i: (i, 0))] + w_specs,
        out_specs=[pl.BlockSpec((tm, d), lambda i: (i, 0)), pl.BlockSpec((tm, d_ff), lambda i: (i, 0))],
        out_shape=[jax.ShapeDtypeStruct((t, d), F32), jax.ShapeDtypeStruct((t, d_ff), F32)],
        compiler_params=_params(("parallel",)),
        name="ffn_short",
    )(x, p0, p1, *weights)
    return out, a.reshape(n_seq, seq_len, d_ff)[:, seq_len - (CONV_W - 1):, :]


STEP_SUBS = 4
STEP_BLKS = STEP_SUBS * LANES // BLK


def _kv_kernel(x_ref, wt_ref, wk_ref, kvc_ref, kvs_ref, kvw_ref, vt_ref, k_ref, *, tiles):
    tm = x_ref.shape[0]
    xb = x_ref[...].astype(MXU_DTYPE)
    r = _mm_nt(wt_ref[...], xb)
    gd = N_KV * HEAD_DIM
    for pair, branch_ref in enumerate((kvc_ref, kvs_ref, kvw_ref)):
        branch_ref[0] = r[pair * 2 * gd:(pair + 1) * 2 * gd, :]
    pos = (pl.program_id(0) % tiles) * tm + lax.broadcasted_iota(jnp.int32, (tm, LANES), 0)
    lane = lax.broadcasted_iota(jnp.int32, (tm, LANES), 1)
    onehot = jnp.where((pos // BLK) % STEP_BLKS + HEAD_DIM == lane, 1.0, 0.0)
    for s in range(2):
        vt_ref[0, s] = r[(2 * s + 3) * gd:(2 * s + 4) * gd, :].astype(MXU_DTYPE)
        for g in range(N_KV):
            k_ref[0, s, g] = (_mm(xb, wk_ref[s, g]) + onehot).astype(MXU_DTYPE)


def _kv_proj(x, n_seq, kv_w):
    t, d = x.shape
    seq_len = t // n_seq
    n_out = kv_w.shape[1]
    gd = N_KV * HEAD_DIM
    tm = ROW_TILE
    assert seq_len % tm == 0 and n_out == 6 * gd and HEAD_DIM + STEP_BLKS <= LANES
    tiles = seq_len // tm
    wk = kv_w.reshape(d, 3, 2, N_KV, HEAD_DIM)[:, 1:, 0].transpose(1, 2, 0, 3)
    wk = jnp.pad(wk, ((0, 0), (0, 0), (0, 0), (0, LANES - HEAD_DIM))).astype(MXU_DTYPE)
    return pl.pallas_call(
        functools.partial(_kv_kernel, tiles=tiles),
        grid=(t // tm,),
        in_specs=[pl.BlockSpec((tm, d), lambda i: (i, 0)), _const_spec((n_out, d)), _const_spec(wk.shape)],
        out_specs=[pl.BlockSpec((1, 2 * gd, tm), lambda i: (i // tiles, 0, i % tiles))] * 3
        + [pl.BlockSpec((1, 2, gd, tm), lambda i: (i // tiles, 0, 0, i % tiles)),
           pl.BlockSpec((1, 2, N_KV, tm, LANES), lambda i: (i // tiles, 0, 0, i % tiles, 0))],
        out_shape=[jax.ShapeDtypeStruct((n_seq, 2 * gd, seq_len), F32)] * 3
        + [jax.ShapeDtypeStruct((n_seq, 2, gd, seq_len), MXU_DTYPE),
                   jax.ShapeDtypeStruct((n_seq, 2, N_KV, seq_len, LANES), MXU_DTYPE)],
        compiler_params=_params(("parallel",)),
        name="kv_proj",
    )(x, kv_w.T.astype(MXU_DTYPE), wk)


def _sigmoid(x):
    return 1.0 / (1.0 + jnp.exp(-x))


GATE_ROWS = 16


def _qg_t_kernel(x_ref, wq_ref, bq_ref, wg_ref, bg_ref, q_ref, gate_ref):
    xb = x_ref[...].astype(MXU_DTYPE)
    q_ref[0] = ((_mm_nt(wq_ref[...], xb) + bq_ref[...]) * (HEAD_DIM ** -0.5 * LOG2E)).astype(MXU_DTYPE)
    gate_ref[0] = _sigmoid(_mm_nt(wg_ref[...], xb) + bg_ref[...])


def _qg_proj_t(x, n_seq, w_qg, b_qg):
    t, d = x.shape
    seq_len = t // n_seq
    n_heads = (w_qg.shape[1] // (HEAD_DIM + 3))
    group = n_heads // N_KV
    nq = n_heads * HEAD_DIM
    tm = ROW_TILE
    assert seq_len % tm == 0 and 3 * group <= GATE_ROWS
    tiles = seq_len // tm
    pad = GATE_ROWS - 3 * group
    wg = jnp.pad(w_qg[:, nq:].T.reshape(N_KV, 3 * group, d), ((0, 0), (0, pad), (0, 0))).reshape(N_KV * GATE_ROWS, d)
    bg = jnp.pad(b_qg[nq:].reshape(N_KV, 3 * group), ((0, 0), (0, pad))).reshape(N_KV * GATE_ROWS, 1)
    bg = jnp.broadcast_to(bg, (N_KV * GATE_ROWS, tm))
    bq = jnp.broadcast_to(b_qg[:nq].reshape(nq, 1), (nq, tm))
    return pl.pallas_call(
        _qg_t_kernel,
        grid=(t // tm,),
        in_specs=[pl.BlockSpec((tm, d), lambda i: (i, 0)), _const_spec((nq, d)), _const_spec((nq, tm)),
                  _const_spec(wg.shape), _const_spec(bg.shape)],
        out_specs=[pl.BlockSpec((1, nq, tm), lambda i: (i // tiles, 0, i % tiles)),
                   pl.BlockSpec((1, N_KV * GATE_ROWS, tm), lambda i: (i // tiles, 0, i % tiles))],
        out_shape=[jax.ShapeDtypeStruct((n_seq, nq, seq_len), MXU_DTYPE),
                   jax.ShapeDtypeStruct((n_seq, N_KV * GATE_ROWS, seq_len), F32)],
        compiler_params=_params(("parallel",)),
        name="qg_proj_t",
    )(x, w_qg[:, :nq].T.astype(MXU_DTYPE), bq, wg.astype(MXU_DTYPE), bg)


def _qg_flat_kernel(x_ref, wq_ref, bq_ref, wg_ref, bg_ref, q_ref, gate_ref):
    xb = x_ref[...].astype(MXU_DTYPE)
    q_ref[...] = (_mm(xb, wq_ref[...]) + bq_ref[...]) * (HEAD_DIM ** -0.5)
    gate_ref[...] = _sigmoid(_mm(xb, wg_ref[...]) + bg_ref[...])


def _qg_proj_flat(x, w_qg, b_qg):
    t, d = x.shape
    n_gate = w_qg.shape[1] // (HEAD_DIM + 3) * 3
    nq = w_qg.shape[1] - n_gate
    tm = ROW_TILE
    assert t % tm == 0 and n_gate <= LANES
    wg = jnp.pad(w_qg[:, nq:], ((0, 0), (0, LANES - n_gate))).astype(MXU_DTYPE)
    bg = jnp.pad(b_qg[nq:], (0, LANES - n_gate)).reshape(1, LANES)
    return pl.pallas_call(
        _qg_flat_kernel,
        grid=(t // tm,),
        in_specs=[pl.BlockSpec((tm, d), lambda i: (i, 0)), _const_spec((d, nq)), _const_spec((1, nq)),
                  _const_spec((d, LANES)), _const_spec((1, LANES))],
        out_specs=[pl.BlockSpec((tm, nq), lambda i: (i, 0)), pl.BlockSpec((tm, LANES), lambda i: (i, 0))],
        out_shape=[jax.ShapeDtypeStruct((t, nq), F32), jax.ShapeDtypeStruct((t, LANES), F32)],
        compiler_params=_params(("parallel",)),
        name="qg_proj_flat",
    )(x, w_qg[:, :nq].astype(MXU_DTYPE), b_qg[:nq].reshape(1, nq), wg, bg)


def _oproj_t_kernel(o_ref, x_ref, wo_ref, bo_ref, lng_ref, lnb_ref, y_ref):
    y = _tn(o_ref[0], wo_ref[...]) + bo_ref[...]
    y_ref[...] = _layer_norm(ALPHA * x_ref[...] + y, lng_ref[...], lnb_ref[...])


def _oproj_t(o_t, x, w_o, b_o, ln_g, ln_b):
    n_seq, nq, seq_len = o_t.shape
    t, d = x.shape
    tm = ROW_TILE
    tiles = seq_len // tm
    row = lambda a: a.reshape(1, -1)
    return pl.pallas_call(
        _oproj_t_kernel,
        grid=(t // tm,),
        in_specs=[pl.BlockSpec((1, nq, tm), lambda i: (i // tiles, 0, i % tiles)),
                  pl.BlockSpec((tm, d), lambda i: (i, 0)), _const_spec((nq, d)),
                  _const_spec((1, d)), _const_spec((1, d)), _const_spec((1, d))],
        out_specs=pl.BlockSpec((tm, d), lambda i: (i, 0)),
        out_shape=jax.ShapeDtypeStruct((t, d), F32),
        compiler_params=_params(("parallel",)),
        name="oproj_t",
    )(o_t, x, w_o.astype(MXU_DTYPE), row(b_o), row(ln_g), row(ln_b))


def _oproj_flat_kernel(o_ref, x_ref, wo_ref, bo_ref, lng_ref, lnb_ref, y_ref):
    y = _mm(o_ref[...].astype(MXU_DTYPE), wo_ref[...]) + bo_ref[...]
    y_ref[...] = _layer_norm(ALPHA * x_ref[...] + y, lng_ref[...], lnb_ref[...])


def _oproj_flat(o, x, w_o, b_o, ln_g, ln_b):
    t, d = x.shape
    tm = ROW_TILE
    row = lambda a: a.reshape(1, -1)
    return pl.pallas_call(
        _oproj_flat_kernel,
        grid=(t // tm,),
        in_specs=[pl.BlockSpec((tm, o.shape[1]), lambda i: (i, 0)), pl.BlockSpec((tm, d), lambda i: (i, 0)),
                  _const_spec(w_o.shape), _const_spec((1, d)), _const_spec((1, d)), _const_spec((1, d))],
        out_specs=pl.BlockSpec((tm, d), lambda i: (i, 0)),
        out_shape=jax.ShapeDtypeStruct((t, d), F32),
        compiler_params=_params(("parallel",)),
        name="oproj_flat",
    )(o, x, w_o.astype(MXU_DTYPE), row(b_o), row(ln_g), row(ln_b))


def _bias_tile_kernel(rel_ref, o_ref):
    h = pl.program_id(0)
    far = rel_ref[NUM_BUCKETS - 1, h]
    key = lax.broadcasted_iota(jnp.int32, (LANES, Q_BLOCK), 0)
    qry = lax.broadcasted_iota(jnp.int32, (LANES, Q_BLOCK), 1)
    for dl in range(o_ref.shape[1]):
        dist = dl * LANES + qry - key
        val = jnp.full((LANES, Q_BLOCK), (rel_ref[0, h] - far) * LOG2E, F32)
        for j in range(1, NUM_BUCKETS):
            val = jnp.where(dist >= BUCKET_THR[j], (rel_ref[j, h] - far) * LOG2E, val)
        o_ref[0, dl] = val


def _bias_tiles(rel_bias):
    n_heads = rel_bias.shape[1]
    return pl.pallas_call(
        _bias_tile_kernel,
        grid=(n_heads,),
        in_specs=[pl.BlockSpec(memory_space=pltpu.SMEM)],
        out_specs=pl.BlockSpec((1, N_BIAS_TILES + 1, LANES, Q_BLOCK), lambda h: (h, 0, 0, 0)),
        out_shape=jax.ShapeDtypeStruct((n_heads, N_BIAS_TILES + 1, LANES, Q_BLOCK), F32),
        compiler_params=_params(("parallel",)),
        name="bias_tiles",
    )(rel_bias)


CMP_PAGES = 32
PE_ROWS = 8


def _compress_kernel(pt_ref, pages_ref, pe_ref, w1_ref, b1_ref, w2_ref, b2_ref, o_ref, buf_ref, sem_ref,
                     *, n_pages, paged):
    b, c = pl.program_id(0), pl.program_id(1)
    n_chunks = pl.num_programs(1)
    step = b * n_chunks + c
    slot = step % 2
    page_rows = buf_ref.shape[1]

    def page_copy(sb, sc, j, sl):
        page = pt_ref[sb, sc * n_pages + j]
        if paged:
            src = pages_ref.at[page]
        else:
            src = pages_ref.at[sb, :, pl.ds(pl.multiple_of(page * PAGE_SIZE, PAGE_SIZE), PAGE_SIZE)]
        return pltpu.make_async_copy(src, buf_ref.at[sl, :, j, :], sem_ref.at[sl])

    @pl.when(step == 0)
    def _():
        for j in range(n_pages):
            page_copy(b, c, j, slot).start()

    @pl.when(step + 1 < pl.num_programs(0) * n_chunks)
    def _():
        nxt_c = jnp.where(c + 1 == n_chunks, 0, c + 1)
        nxt_b = jnp.where(c + 1 == n_chunks, b + 1, b)
        for j in range(n_pages):
            page_copy(nxt_b, nxt_c, j, 1 - slot).start(priority=j % 2)

    for j in range(n_pages):
        page_copy(b, c, j, slot).wait()
    rows = N_KV * n_pages

    def head_dim_row(k, g, d):
        return buf_ref[slot, (k * N_KV + g) * HEAD_DIM + d]

    for k in range(2):
        acc = jnp.zeros((rows + PE_ROWS, 2 * LANES), F32)
        for dd in range(HEAD_DIM // 2):
            parts = [jnp.concatenate([head_dim_row(k, g, 2 * dd), head_dim_row(k, g, 2 * dd + 1)], axis=1)
                     for g in range(N_KV)]
            parts.append(pe_ref[k, dd])
            acc = acc + _mm(jnp.concatenate(parts, axis=0).astype(MXU_DTYPE), w1_ref[k, dd])
        pre = acc[:rows] + acc[rows:rows + 1] + b1_ref[k]
        o_ref[0, 0, k] = _mm(_gelu(pre).astype(MXU_DTYPE), w2_ref[k]) + b2_ref[k]


def _compress_pages(pages, page_table, cmp_pe, cmp_w1, cmp_b1, cmp_w2, cmp_b2, raw_layout=False):
    paged = page_table is not None
    if paged:
        pages = pages.reshape(pages.shape[0], 2 * N_KV * HEAD_DIM, PAGE_SIZE)
    else:
        assert pages.shape[1] == 2 * N_KV * HEAD_DIM
        page_table = jnp.broadcast_to(jnp.arange(pages.shape[2] // PAGE_SIZE, dtype=jnp.int32)[None],
                                      (pages.shape[0], pages.shape[2] // PAGE_SIZE))
    n_seq, pps = page_table.shape
    bpp = PAGE_SIZE // BLK
    hid = cmp_w1.shape[-1]
    n_pages = min(CMP_PAGES, pps)
    page_rows = 2 * N_KV * HEAD_DIM
    assert pps % n_pages == 0 and n_pages % 8 == 0 and bpp * hid == 2 * LANES and bpp * HEAD_DIM == LANES
    n_chunks = pps // n_pages
    w1r = cmp_w1.reshape(2, BLK, HEAD_DIM // 2, 2, hid)
    eye = jnp.eye(bpp, dtype=cmp_w1.dtype)
    w1bd = jnp.einsum('kldeh,nm->kdenlmh', w1r, eye).reshape(2, HEAD_DIM // 2, 2 * PAGE_SIZE, bpp * hid).astype(MXU_DTYPE)
    pe_t = jnp.tile(jnp.swapaxes(cmp_pe, 1, 2), (1, 1, bpp))
    pe_rows = jnp.pad(pe_t.reshape(2, HEAD_DIM // 2, 1, 2 * PAGE_SIZE), ((0, 0), (0, 0), (0, PE_ROWS - 1), (0, 0)))
    b1 = jnp.tile(cmp_b1, (1, bpp)).reshape(2, 1, bpp * hid)
    w2bd = jnp.einsum('khd,nm->knhmd', cmp_w2, eye).reshape(2, bpp * hid, bpp * HEAD_DIM).astype(MXU_DTYPE)
    b2 = jnp.tile(cmp_b2, (1, bpp)).reshape(2, 1, bpp * HEAD_DIM)
    rows = N_KV * n_pages
    const = lambda a: pl.BlockSpec(a.shape, lambda b, c, pt: (0,) * a.ndim, pipeline_mode=pl.Buffered(1))
    out = pl.pallas_call(
        functools.partial(_compress_kernel, n_pages=n_pages, paged=paged),
        grid_spec=pltpu.PrefetchScalarGridSpec(
            num_scalar_prefetch=1,
            grid=(n_seq, n_chunks),
            in_specs=[pl.BlockSpec(memory_space=pl.ANY)] + [const(a) for a in (pe_rows, w1bd, b1, w2bd, b2)],
            out_specs=pl.BlockSpec((1, 1, 2, rows, bpp * HEAD_DIM), lambda b, c, pt: (b, c, 0, 0, 0)),
            scratch_shapes=[pltpu.VMEM((2, page_rows, n_pages, PAGE_SIZE), F32), pltpu.SemaphoreType.DMA((2,))]),
        out_shape=jax.ShapeDtypeStruct((n_seq, n_chunks, 2, rows, bpp * HEAD_DIM), F32),
        compiler_params=_params(("arbitrary", "arbitrary")),
        name="compress_pages",
    )(page_table, pages, pe_rows, w1bd, b1, w2bd, b2)
    if raw_layout:
        return out
    out = out.reshape(n_seq, n_chunks, 2, N_KV, n_pages, bpp, HEAD_DIM)
    return out.transpose(0, 2, 3, 1, 4, 5, 6).reshape(n_seq, 2, N_KV, pps * bpp, HEAD_DIM)


WIN_SUBS = WINDOW // LANES + 1
CMP_LOOKBACK = (FAR_DIST + BLK - 1) // BLK + 1
CMP_BAND = -(-(CMP_LOOKBACK + Q_BLOCK // BLK + 6) // 8) * 8


def _attn_prompt_kernel(rel_ref, qt_ref, gate_ref, kc_ref, vct_ref, sk_ref, svt_ref, wk_ref, wvt_ref, tb_ref, o_ref,
                        sel_ref, lc_ref, s_ref, p_ref, *, group, n_cmp, n_sub):
    g = pl.program_id(1)
    i = pl.program_id(2)
    qt = jnp.concatenate([qt_ref[0, r] for r in range(group)], axis=1)
    lanes = lambda x, r: x[:, r * Q_BLOCK:(r + 1) * Q_BLOCK]

    nl = group * Q_BLOCK
    key = lax.broadcasted_iota(jnp.int32, (LANES, Q_BLOCK), 0)
    qpos = i * Q_BLOCK + lax.broadcasted_iota(jnp.int32, (LANES, Q_BLOCK), 1)
    zero_rows = jnp.zeros((LANES - HEAD_DIM, nl), MXU_DTYPE)
    qt_plain = jnp.concatenate([qt, zero_rows], axis=0)

    def sub_keys(k_ref, t):
        off = pl.multiple_of(jnp.clip(t, 0, n_sub - 1) * LANES, LANES)
        return k_ref[0, 0, 0, pl.ds(off, LANES), :]

    def with_bias(s, dl, madd=None):
        cols = [lanes(s, r) + tb_ref[r, dl] for r in range(group)]
        if madd is not None:
            cols = [c + madd for c in cols]
        return jnp.concatenate(cols, axis=1)

    lc_ref[...] = _mm(kc_ref[0, 0], qt)
    c0 = pl.multiple_of(jnp.clip((i * (Q_BLOCK // BLK) - CMP_LOOKBACK) // 8 * 8, 0, n_cmp - CMP_BAND), 8)
    row_b = c0 + lax.broadcasted_iota(jnp.int32, (CMP_BAND, Q_BLOCK), 0)
    dist_b = i * Q_BLOCK + lax.broadcasted_iota(jnp.int32, (CMP_BAND, Q_BLOCK), 1) - (row_b * BLK + BLK - 1)
    bkt = _bucket_of(dist_b)
    band = []
    for r in range(group):
        h = g * group + r
        far = rel_ref[NUM_BUCKETS - 1, h]
        bias = jnp.full((CMP_BAND, Q_BLOCK), (rel_ref[0, h] - far) * LOG2E, F32)
        for j in range(1, NUM_BUCKETS - 1):
            bias = jnp.where(bkt == j, (rel_ref[j, h] - far) * LOG2E, bias)
        band.append(jnp.where(bkt == NUM_BUCKETS - 1, 0.0, bias))
    lc_ref[pl.ds(c0, CMP_BAND), :] = lc_ref[pl.ds(c0, CMP_BAND), :] + jnp.concatenate(band, axis=1)
    blk = lax.broadcasted_iota(jnp.int32, (n_cmp, Q_BLOCK), 0)
    qpos_c = i * Q_BLOCK + lax.broadcasted_iota(jnp.int32, (n_cmp, Q_BLOCK), 1)
    ok_c = qpos_c >= blk * BLK + BLK - 1
    ok_f = ok_c.astype(F32)
    imp = jnp.zeros((n_cmp, Q_BLOCK), F32)
    pcs = []
    for r in range(group):
        lg = jnp.where(ok_c, lc_ref[:, r * Q_BLOCK:(r + 1) * Q_BLOCK], NEG_INF)
        p = jnp.exp2(lg - jnp.max(lg, axis=0, keepdims=True)) * ok_f
        pc = p * (1.0 / jnp.maximum(jnp.sum(p, axis=0, keepdims=True), 1e-30))
        imp = imp + pc
        pcs.append(pc.astype(MXU_DTYPE))
    o_c = _mm(vct_ref[0, 0], jnp.concatenate(pcs, axis=1))

    parts = []
    for u in range(WIN_SUBS):
        t = i - (WIN_SUBS - 1) + u
        dw = qpos - (t * LANES + key)
        madd = jnp.where((dw >= 0) & (dw < WINDOW) & (t >= 0), 0.0, NEG_INF)
        parts.append(with_bias(_mm(sub_keys(wk_ref, t), qt_plain), WIN_SUBS - 1 - u, madd))
    sw = jnp.concatenate(parts, axis=0)
    pw = jnp.exp2(sw - jnp.max(sw, axis=0, keepdims=True))
    acc_w = jnp.zeros((HEAD_DIM, nl), F32)
    for u in range(WIN_SUBS):
        off = pl.multiple_of(jnp.maximum(i - (WIN_SUBS - 1) + u, 0) * LANES, LANES)
        acc_w = acc_w + _mm(wvt_ref[0, 0, 0, :, pl.ds(off, LANES)], pw[u * LANES:(u + 1) * LANES].astype(MXU_DTYPE))
    o_w = acc_w * (1.0 / jnp.sum(pw, axis=0, keepdims=True))

    cur = qpos_c // BLK
    forced = (blk == 0) | (blk == cur) | (blk == cur - 1)
    score = jnp.where(forced, FORCED_SCORE, jnp.where(blk <= cur, imp, -1.0))
    sel = jnp.zeros((n_cmp, Q_BLOCK), F32)
    for _ in range(min(N_SEL, n_cmp)):
        mx = jnp.max(score, axis=0, keepdims=True)
        idx = jnp.min(jnp.where(score == mx, blk, n_cmp), axis=0, keepdims=True)
        hit = blk == idx
        sel = jnp.where(hit, 1.0, sel)
        score = jnp.where(hit, -2.0, score)
    sel_ref[...] = (sel - 1.0) * (-NEG_INF)

    def step_logits(step):
        mrows = sel_ref[pl.ds(pl.multiple_of(step * STEP_BLKS, STEP_BLKS), STEP_BLKS), :]
        mrows = jnp.concatenate([mrows] * group, axis=1)
        pad = jnp.zeros((LANES - HEAD_DIM - STEP_BLKS, nl), F32)
        q_aug = jnp.concatenate([qt, jnp.concatenate([mrows, pad], axis=0).astype(MXU_DTYPE)], axis=0)
        return [_mm(sub_keys(sk_ref, step * STEP_SUBS + u), q_aug) for u in range(STEP_SUBS)]

    step_keys = STEP_SUBS * LANES

    def qk_store(step, kind):
        cm = jnp.full((1, nl), NEG_INF, F32)
        for u, s in enumerate(step_logits(step)):
            t = step * STEP_SUBS + u
            if kind == 'near':
                s = with_bias(s, jnp.clip(i - t, 0, N_BIAS_TILES))
            elif kind == 'diag':
                s = with_bias(s, jnp.clip(i - t, 0, N_BIAS_TILES), jnp.where(t * LANES + key <= qpos, 0.0, NEG_INF))
            s_ref[pl.ds(pl.multiple_of(t * LANES, LANES), LANES), :] = s
            cm = jnp.maximum(cm, jnp.max(s, axis=0, keepdims=True))
        return cm

    def values(step):
        off = pl.multiple_of(jnp.maximum(step, 0) * step_keys, step_keys)
        return svt_ref[0, 0, 0, :, pl.ds(off, step_keys)]

    def softmax_step(step, m, l, cm):
        m_new = jnp.maximum(m, cm)
        a = jnp.exp2(m - m_new)
        p = jnp.exp2(s_ref[pl.ds(pl.multiple_of(step * step_keys, step_keys), step_keys), :] - m_new)
        return m_new, a * l + jnp.sum(p, axis=0, keepdims=True), a, p.astype(MXU_DTYPE)

    def pipe_step(j, carry, kind):
        m, l, acc, cm, a_prev = carry
        acc = a_prev * acc + _mm(values(j - 1), p_ref[...])
        m, l, a, p = softmax_step(j, m, l, cm)
        p_ref[...] = p
        return m, l, acc, qk_store(j + 1, kind), a

    last = i // STEP_SUBS
    n_far = jnp.maximum((i - (N_BIAS_TILES + STEP_SUBS - 1)) // STEP_SUBS + 1, 0)
    p_ref[...] = jnp.zeros(p_ref.shape, p_ref.dtype)
    carry = (jnp.full((1, nl), NEG_INF, F32), jnp.zeros((1, nl), F32), jnp.zeros((HEAD_DIM, nl), F32),
             qk_store(0, 'diag'), jnp.ones((1, nl), F32))
    n_a = jnp.maximum(n_far - 1, 0)
    carry = lax.fori_loop(0, n_a, functools.partial(pipe_step, kind='far'), carry)
    carry = lax.fori_loop(n_a, last - 1, functools.partial(pipe_step, kind='near'), carry)
    m, l_s, acc_s, cm, a_prev = lax.fori_loop(jnp.maximum(last - 1, 0), last,
                                              functools.partial(pipe_step, kind='diag'), carry)
    acc_s = a_prev * acc_s + _mm(values(last - 1), p_ref[...])
    _, l_s, a, p = softmax_step(last, m, l_s, cm)
    acc_s = a * acc_s + _mm(values(last), p)
    o_s = acc_s * (1.0 / l_s)

    gt = gate_ref[0]
    for r in range(group):
        o = (gt[3 * r:3 * r + 1, :] * lanes(o_c, r) + gt[3 * r + 1:3 * r + 2, :] * lanes(o_s, r)
             + gt[3 * r + 2:3 * r + 3, :] * lanes(o_w, r))
        o_ref[0, r] = o.astype(o_ref.dtype)


def _attn_prompt(q_t, gates_t, kvc, k_rows, v_t, bias_tiles, rel_bias):
    bsz, nq, seq = q_t.shape
    n_heads = nq // HEAD_DIM
    group = n_heads // N_KV
    n_cmp = kvc.shape[3]
    n_sub = seq // LANES
    assert seq % Q_BLOCK == 0 and Q_BLOCK == LANES and n_cmp * BLK == seq and n_sub % STEP_SUBS == 0
    assert WIN_SUBS - 1 <= N_BIAS_TILES and n_cmp >= CMP_BAND and n_cmp % 8 == 0
    kc = kvc[:, 0].astype(MXU_DTYPE)
    vct = jnp.swapaxes(kvc[:, 1], -1, -2).astype(MXU_DTYPE)
    q4 = q_t.reshape(bsz, n_heads, HEAD_DIM, seq)
    v4 = v_t.reshape(bsz, 2, N_KV, HEAD_DIM, seq)
    k_spec = lambda s: pl.BlockSpec((1, 1, 1, seq, LANES), lambda b, g, i: (b, s, g, 0, 0))
    v_spec = lambda s: pl.BlockSpec((1, 1, 1, HEAD_DIM, seq), lambda b, g, i: (b, s, g, 0, 0))
    o = pl.pallas_call(
        functools.partial(_attn_prompt_kernel, group=group, n_cmp=n_cmp, n_sub=n_sub),
        grid=(bsz, N_KV, seq // Q_BLOCK),
        in_specs=[pl.BlockSpec(memory_space=pltpu.SMEM),
                  pl.BlockSpec((1, group, HEAD_DIM, Q_BLOCK), lambda b, g, i: (b, g, 0, i)),
                  pl.BlockSpec((1, GATE_ROWS, Q_BLOCK), lambda b, g, i: (b, g, i)),
                  pl.BlockSpec((1, 1, n_cmp, HEAD_DIM), lambda b, g, i: (b, g, 0, 0)),
                  pl.BlockSpec((1, 1, HEAD_DIM, n_cmp), lambda b, g, i: (b, g, 0, 0)),
                  k_spec(0), v_spec(0), k_spec(1), v_spec(1),
                  pl.BlockSpec((group, N_BIAS_TILES + 1, LANES, Q_BLOCK), lambda b, g, i: (g, 0, 0, 0))],
        out_specs=pl.BlockSpec((1, group, HEAD_DIM, Q_BLOCK), lambda b, g, i: (b, g, 0, i)),
        out_shape=jax.ShapeDtypeStruct((bsz, n_heads, HEAD_DIM, seq), MXU_DTYPE),
        scratch_shapes=[pltpu.VMEM((n_cmp, Q_BLOCK), F32), pltpu.VMEM((n_cmp, group * Q_BLOCK), F32),
                        pltpu.VMEM((seq, group * Q_BLOCK), F32),
                        pltpu.VMEM((STEP_SUBS * LANES, group * Q_BLOCK), MXU_DTYPE)],
        compiler_params=_params(("parallel", "parallel", "arbitrary")),
        name="attn_prompt",
    )(rel_bias, q4, gates_t, kc, vct, k_rows, v4, k_rows, v4, bias_tiles)
    return o.reshape(bsz, nq, seq)


SLC_PAGES = 32
NEW_PAD = 16


def _bias_rows(trow_ref, dist):
    bkt = _bucket_of(dist)
    bias = jnp.broadcast_to(trow_ref[0:1, :], dist.shape)
    for j in range(1, NUM_BUCKETS):
        bias = jnp.where(bkt == j, trow_ref[j:j + 1, :], bias)
    return bias


def _attn_sample_kernel(pt_ref, *refs, n_pages, past_len, n_new, n_cmp, n_blk):
    del pt_ref
    page_refs = refs[:n_pages]
    (qbd_ref, gate_ref, trow_ref, cmp_ref, knew_ref, vnewt_ref, wkt_ref, wvt_ref, wknew_ref, wvnewt_ref,
     o_ref, sel_ref, oc_ref, m_ref, l_ref, acc_ref, s_ref) = refs[n_pages:]
    c = pl.program_id(1)
    qbd = qbd_ref[0]
    gd = qbd.shape[0]
    bpp = PAGE_SIZE // BLK
    npb = past_len // BLK
    pps = past_len // PAGE_SIZE

    def qpos_like(shape):
        return past_len + lax.broadcasted_iota(jnp.int32, shape, 1) % n_new

    def block_of_row(row):
        return jnp.where(row < n_cmp, (row % pps) * bpp + row // pps, row)

    @pl.when(c == 0)
    def _():
        ppc = cmp_ref.shape[3] // N_KV
        lane_g = (lax.broadcasted_iota(jnp.int32, (pps, LANES), 1) // n_new) % N_KV

        def slab(kv, g):
            return jnp.concatenate([cmp_ref[0, ch, kv, g * ppc:(g + 1) * ppc, :] for ch in range(cmp_ref.shape[1])],
                                   axis=0).astype(MXU_DTYPE)

        zero_q = jnp.zeros((HEAD_DIM, LANES), MXU_DTYPE)
        parts = []
        for n in range(bpp):
            s_n = jnp.zeros((pps, LANES), F32)
            for g in range(N_KV):
                q_n = jnp.concatenate([qbd[g * HEAD_DIM:(g + 1) * HEAD_DIM] if t == n else zero_q for t in range(bpp)],
                                      axis=0)
                s_n = jnp.where(lane_g == g, _mm(slab(0, g), q_n), s_n)
            parts.append(s_n)
        blk = block_of_row(lax.broadcasted_iota(jnp.int32, (n_cmp, LANES), 0))
        dist = qpos_like((n_cmp, LANES)) - (blk * BLK + BLK - 1)
        ok = dist >= 0
        lg = jnp.where(ok, jnp.concatenate(parts, axis=0) + _bias_rows(trow_ref, dist), NEG_INF)
        p = jnp.exp(lg - jnp.max(lg, axis=0, keepdims=True)) * ok.astype(F32)
        pc = p / jnp.maximum(jnp.sum(p, axis=0, keepdims=True), 1e-30)
        pc_b = pc.astype(MXU_DTYPE)
        o_groups = []
        for g in range(N_KV):
            v_g = slab(1, g)
            o_g = jnp.zeros((HEAD_DIM, LANES), F32)
            for n in range(bpp):
                o_g = o_g + _tn(v_g, pc_b[n * pps:(n + 1) * pps])[n * HEAD_DIM:(n + 1) * HEAD_DIM]
            o_groups.append(o_g)
        oc_ref[...] = jnp.concatenate(o_groups, axis=0)
        quarter = LANES // 4
        imp = pc + pltpu.roll(pc, quarter, 1) + pltpu.roll(pc, 2 * quarter, 1) + pltpu.roll(pc, 3 * quarter, 1)
        nbp = sel_ref.shape[0]
        imp = jnp.concatenate([imp, jnp.zeros((nbp - n_cmp, LANES), F32)], axis=0)
        blk = block_of_row(lax.broadcasted_iota(jnp.int32, (nbp, LANES), 0))
        cur = qpos_like((nbp, LANES)) // BLK
        forced = (blk == 0) | (blk == cur) | (blk == cur - 1)
        score = jnp.where(forced, FORCED_SCORE, jnp.where(blk <= cur, imp, -1.0))
        score = jnp.where(blk >= n_blk, -2.0, score)
        sel = jnp.zeros((nbp, LANES), F32)
        for _ in range(min(N_SEL, n_blk)):
            mx = jnp.max(score, axis=0, keepdims=True)
            idx = jnp.min(jnp.where(score == mx, blk, nbp), axis=0, keepdims=True)
            hit = blk == idx
            sel = jnp.where(hit, 1.0, sel)
            score = jnp.where(hit, -3.0, score)
        sel_ref[...] = sel
        m_ref[...] = jnp.full(m_ref.shape, NEG_INF, F32)
        l_ref[...] = jnp.zeros(l_ref.shape, F32)
        acc_ref[...] = jnp.zeros(acc_ref.shape, F32)

    tok = lax.broadcasted_iota(jnp.int32, (PAGE_SIZE, LANES), 0)
    qpos_p = qpos_like((PAGE_SIZE, LANES))

    far_pages = min(max((past_len - FAR_DIST - (PAGE_SIZE - 1)) // PAGE_SIZE + 1, 0), past_len // PAGE_SIZE)
    assert past_len // PAGE_SIZE - far_pages <= n_pages
    first_near = far_pages - (past_len // PAGE_SIZE - n_pages)

    def page_logits(near):
        for j in range(n_pages):
            pg = c * n_pages + j
            st = _tn(page_refs[j][0, 0].reshape(gd, PAGE_SIZE).astype(MXU_DTYPE), qbd)
            if near and j >= first_near:
                st = st + _bias_rows(trow_ref, qpos_p - (pg * PAGE_SIZE + tok))
            else:
                st = st + trow_ref[NUM_BUCKETS - 1:NUM_BUCKETS, :]
            chosen = jnp.broadcast_to(sel_ref[pl.ds((bpp - 1) * pps + pg, 1), :], (PAGE_SIZE, LANES))
            for t in reversed(range(bpp - 1)):
                chosen = jnp.where(tok < (t + 1) * BLK, sel_ref[pl.ds(t * pps + pg, 1), :], chosen)
            s_ref[j * PAGE_SIZE:(j + 1) * PAGE_SIZE, :] = jnp.where(chosen > 0.5, st, NEG_INF)

    is_last = c == pl.num_programs(1) - 1
    pl.when(jnp.logical_not(is_last))(functools.partial(page_logits, False))
    pl.when(is_last)(functools.partial(page_logits, True))
    s_all = s_ref[...]
    m_old = m_ref[...]
    m_new = jnp.maximum(m_old, jnp.max(s_all, axis=0, keepdims=True))
    a = jnp.exp(m_old - m_new)
    p = jnp.exp(s_all - m_new)
    acc = acc_ref[...] * a
    for j in range(n_pages):
        vt = page_refs[j][0, 1].reshape(gd, PAGE_SIZE).astype(MXU_DTYPE)
        acc = acc + _mm(vt, p[j * PAGE_SIZE:(j + 1) * PAGE_SIZE].astype(MXU_DTYPE))
    acc_ref[...] = acc
    l_ref[...] = a * l_ref[...] + jnp.sum(p, axis=0, keepdims=True)
    m_ref[...] = m_new

    @pl.when(c == pl.num_programs(1) - 1)
    def _():
        tok_n = lax.broadcasted_iota(jnp.int32, (NEW_PAD, LANES), 0)
        dist_n = qpos_like((NEW_PAD, LANES)) - (past_len + tok_n)
        live_n = (tok_n < n_new) & (dist_n >= 0)
        bias_n = _bias_rows(trow_ref, dist_n)
        chosen_n = sel_ref[pl.ds(npb, 1), :] > 0.5
        s_n = jnp.where(live_n & chosen_n, _mm(knew_ref[0], qbd) + bias_n, NEG_INF)
        m_old = m_ref[...]
        m_fin = jnp.maximum(m_old, jnp.max(s_n, axis=0, keepdims=True))
        a = jnp.exp(m_old - m_fin)
        p_n = jnp.exp(s_n - m_fin)
        acc_s = acc_ref[...] * a + _mm(vnewt_ref[0], p_n.astype(MXU_DTYPE))
        o_s = acc_s / (a * l_ref[...] + jnp.sum(p_n, axis=0, keepdims=True))
        wb = wkt_ref.shape[-1]
        tok_w = lax.broadcasted_iota(jnp.int32, (wb, LANES), 0)
        kwpos = past_len - wb + tok_w
        dw = qpos_like((wb, LANES)) - kwpos
        ok_w = (dw >= 0) & (dw < WINDOW) & (kwpos >= 0)
        s_w = _tn(wkt_ref[0, 0].reshape(gd, wb).astype(MXU_DTYPE), qbd) + _bias_rows(trow_ref, dw)
        s_w = jnp.where(ok_w, s_w, NEG_INF)
        s_wn = jnp.where(live_n & (dist_n < WINDOW), _mm(wknew_ref[0], qbd) + bias_n, NEG_INF)
        m_w = jnp.maximum(jnp.max(s_w, axis=0, keepdims=True), jnp.max(s_wn, axis=0, keepdims=True))
        p_w = jnp.exp(s_w - m_w)
        p_wn = jnp.exp(s_wn - m_w)
        o_w = (_mm(wvt_ref[0, 0].reshape(gd, wb).astype(MXU_DTYPE), p_w.astype(MXU_DTYPE))
               + _mm(wvnewt_ref[0], p_wn.astype(MXU_DTYPE)))
        o_w = o_w / (jnp.sum(p_w, axis=0, keepdims=True) + jnp.sum(p_wn, axis=0, keepdims=True))
        gt = gate_ref[0]
        tot = gt[0:1, :] * oc_ref[...] + gt[1:2, :] * o_s + gt[2:3, :] * o_w
        lane_g = (lax.broadcasted_iota(jnp.int32, (HEAD_DIM, LANES), 1) // n_new) % N_KV
        out = jnp.zeros((HEAD_DIM, LANES), F32)
        for g in range(N_KV):
            out = out + jnp.where(lane_g == g, tot[g * HEAD_DIM:(g + 1) * HEAD_DIM, :], 0.0)
        o_ref[0] = out


def _attn_sample(q, gates, cmp_raw, kv_new_t, slc_pages, page_table, win_t, rel_bias):
    bsz, n_new, nq = q.shape
    n_heads = nq // HEAD_DIM
    group = n_heads // N_KV
    pps = page_table.shape[1]
    past_len = pps * PAGE_SIZE
    n_cmp = cmp_raw.shape[1] * (cmp_raw.shape[3] // N_KV) * (PAGE_SIZE // BLK)
    n_blk = past_len // BLK + -(-n_new // BLK)
    nbp = -(-n_blk // 8) * 8
    gd = N_KV * HEAD_DIM
    wb = win_t.shape[-1]
    n_pages = min(SLC_PAGES, pps)
    assert n_heads * n_new == LANES and group == 4 and n_new <= min(BLK, NEW_PAD) and pps % n_pages == 0
    assert n_cmp == past_len // BLK and n_cmp % 8 == 0
    q5 = q.reshape(bsz, n_new, N_KV, group, HEAD_DIM)
    qbd = jnp.einsum('hg,bqgrd->bhdrgq', jnp.eye(N_KV, dtype=q.dtype), q5).reshape(bsz, gd, LANES).astype(MXU_DTYPE)
    g5 = gates[:, :, :3 * n_heads].reshape(bsz, n_new, N_KV, group, 3)
    gate_t = jnp.pad(g5.transpose(0, 4, 3, 2, 1).reshape(bsz, 3, LANES), ((0, 0), (0, 5), (0, 0)))
    trow = jnp.broadcast_to(rel_bias.reshape(NUM_BUCKETS, N_KV, group).transpose(0, 2, 1)[..., None],
                            (NUM_BUCKETS, group, N_KV, n_new)).reshape(NUM_BUCKETS, LANES)
    new = kv_new_t.reshape(6, gd, bsz, n_new).transpose(0, 2, 1, 3)
    new = jnp.pad(new, ((0, 0), (0, 0), (0, 0), (0, NEW_PAD - n_new))).astype(MXU_DTYPE)
    k_rows = lambda s: jnp.swapaxes(new[s], 1, 2)
    page_spec = lambda j: pl.BlockSpec((1, 2, N_KV, HEAD_DIM, PAGE_SIZE),
                                       lambda b, c, pt: (pt[b, c * n_pages + j], 0, 0, 0, 0))
    per_b = lambda a: pl.BlockSpec((1,) + a.shape[1:], lambda b, c, pt: (b,) + (0,) * (a.ndim - 1))
    win_spec = lambda s: pl.BlockSpec((1, 1, N_KV, HEAD_DIM, wb), lambda b, c, pt: (b, s, 0, 0, 0))
    smalls = (qbd, gate_t)
    o_t = pl.pallas_call(
        functools.partial(_attn_sample_kernel, n_pages=n_pages, past_len=past_len, n_new=n_new, n_cmp=n_cmp,
                          n_blk=n_blk),
        grid_spec=pltpu.PrefetchScalarGridSpec(
            num_scalar_prefetch=1,
            grid=(bsz, pps // n_pages),
            in_specs=[page_spec(j) for j in range(n_pages)]
            + [per_b(qbd), per_b(gate_t), pl.BlockSpec(trow.shape, lambda b, c, pt: (0, 0)),
               per_b(cmp_raw), per_b(k_rows(2)), per_b(new[3]), win_spec(0), win_spec(1),
               per_b(k_rows(4)), per_b(new[5])],
            out_specs=pl.BlockSpec((1, HEAD_DIM, LANES), lambda b, c, pt: (b, 0, 0)),
            scratch_shapes=[pltpu.VMEM((nbp, LANES), F32), pltpu.VMEM((gd, LANES), F32), pltpu.VMEM((1, LANES), F32),
                            pltpu.VMEM((1, LANES), F32), pltpu.VMEM((gd, LANES), F32),
                            pltpu.VMEM((n_pages * PAGE_SIZE, LANES), F32)]),
        out_shape=jax.ShapeDtypeStruct((bsz, HEAD_DIM, LANES), F32),
        compiler_params=_params(("parallel", "arbitrary")),
        name="attn_sample",
    )(page_table, *([slc_pages] * n_pages), qbd, gate_t, trow, cmp_raw, k_rows(2), new[3], win_t, win_t,
      k_rows(4), new[5])
    del smalls
    return o_t.reshape(bsz, HEAD_DIM, group, N_KV, n_new).transpose(0, 4, 3, 2, 1).reshape(bsz, n_new, nq)


def _tokens_last(x):
    nd = x.ndim
    return jnp.transpose(x, tuple(range(nd - 4)) + (nd - 3, nd - 2, nd - 1, nd - 4))


def _tokens_first(x):
    nd = x.ndim
    return jnp.transpose(x, tuple(range(nd - 4)) + (nd - 1, nd - 4, nd - 3, nd - 2))


def kernel(x_prompt, x_sample, cache_kv_cmp, cache_kv_slc, cache_win, state_conv, page_table, ln_g, ln_b, sg_w_in, sg_b_in, sg_ln_g, sg_ln_b, sg_w_s, sg_b_s, sg_w_out, sg_b_out, ffn_w_up, ffn_b_up, ffn_w_dw, ffn_b_dw, ffn_w_down, ffn_b_down, kv_w, cmp_pe, cmp_w1, cmp_b1, cmp_w2, cmp_b2, nsa_w_qg, nsa_b_qg, nsa_w_o, nsa_b_o, rel_bias):
    depth = ln_g.shape[0]
    n_a = sg_w_in.shape[0]
    assert depth == DEPTH
    bsz, seq, d_model = x_prompt.shape
    dbsz, n_new, _ = x_sample.shape
    d_ff = ffn_w_down.shape[1]
    kvd = (2, N_KV, HEAD_DIM)
    gd = N_KV * HEAD_DIM
    cmp_w = (cmp_pe, cmp_w1, cmp_b1, cmp_w2, cmp_b2)

    def trunk(x, seq_len, conv_state, attend_layer, emit_v):
        n_seq = x.shape[0] // seq_len
        new_conv, sg_v, kv_t, ctx = [], [], None, None
        for l in range(depth):
            if l < n_a:
                x, v = _sgu_layer(x, seq_len, sg_w_in[l], sg_b_in[l], sg_ln_g[l], sg_ln_b[l], sg_w_s[l], sg_b_s[l],
                                  sg_w_out[l], sg_b_out[l], ln_g[l, 0], ln_b[l, 0], emit_v)
                if emit_v:
                    sg_v.append(v.reshape(n_seq, seq_len, -1))
            else:
                i = l - n_a
                x = attend_layer(x, ctx, nsa_w_qg[i], nsa_b_qg[i], nsa_w_o[i], nsa_b_o[i], ln_g[l, 0], ln_b[l, 0])
            x, cs = _ffn_layer(x, seq_len, conv_state[l], ffn_w_up[l], ffn_b_up[l], ffn_w_dw[l], ffn_b_dw[l],
                               ffn_w_down[l], ffn_b_down[l], ln_g[l, 1], ln_b[l, 1])
            new_conv.append(cs)
            if l == n_a - 1:
                kv_t, ctx = attend_layer.prepare(x)
        return x, jnp.stack(new_conv), (jnp.stack(sg_v) if sg_v else None), kv_t

    bias_tiles = _bias_tiles(rel_bias)

    def prompt_prepare(x):
        *kv_t, v_t, k_rows = _kv_proj(x, bsz, kv_w)
        kvc = _compress_pages(kv_t[0], None, *cmp_w)
        return kv_t, (kvc, k_rows, v_t)

    def prompt_attend(x, ctx, w_qg, b_qg, w_o, b_o, g, b):
        kvc, k_rows, v_t = ctx
        q_t, gates_t = _qg_proj_t(x, bsz, w_qg, b_qg)
        o_t = _attn_prompt(q_t, gates_t, kvc, k_rows, v_t, bias_tiles, rel_bias)
        return _oproj_t(o_t, x, w_o, b_o, g, b)

    prompt_attend.prepare = prompt_prepare
    zeros_state = jnp.zeros((depth, bsz, CONV_W - 1, d_ff), x_prompt.dtype)
    y_p, conv_p, _, kv_p = trunk(x_prompt.reshape(bsz * seq, d_model), seq, zeros_state, prompt_attend, False)
    kv_p = [a.reshape(bsz, 2, N_KV, HEAD_DIM, seq) for a in kv_p]

    cmp_pages = _tokens_last(cache_kv_cmp)
    slc_pages = _tokens_last(cache_kv_slc)
    win_t = _tokens_last(cache_win)

    def sample_prepare(x):
        *kv_t, _, _ = _kv_proj(x, 1, kv_w)
        kv_t = jnp.concatenate(kv_t, axis=1).reshape(6 * gd, dbsz * n_new)
        kvc = _compress_pages(cmp_pages, page_table, *cmp_w, raw_layout=True)
        return kv_t, (kvc, kv_t)

    def sample_attend(x, ctx, w_qg, b_qg, w_o, b_o, g, b):
        kvc, kv_new_t = ctx
        q, gates = _qg_proj_flat(x, w_qg, b_qg)
        o = _attn_sample(q.reshape(dbsz, n_new, -1), gates.reshape(dbsz, n_new, -1), kvc, kv_new_t, slc_pages,
                         page_table, win_t, rel_bias)
        return _oproj_flat(o.reshape(dbsz * n_new, -1), x, w_o, b_o, g, b)

    sample_attend.prepare = sample_prepare
    assert n_new < BLK
    y_s, conv_s, sgv_s, kv_s = trunk(x_sample.reshape(dbsz * n_new, d_model), n_new, state_conv, sample_attend, True)
    kv_s = kv_s.reshape(3, 2, N_KV, HEAD_DIM, dbsz, n_new)

    win_len = min(WINDOW, seq)
    out_p = [_tokens_first(kv_p[s]) for s in range(3)]
    out_s = [jnp.transpose(kv_s[s], (3, 4, 0, 1, 2)) for s in range(3)]
    return (y_p.reshape(bsz, seq, d_model), y_s.reshape(dbsz, n_new, d_model),
            out_p[0], out_p[1], out_p[2][:, seq - win_len:], conv_p,
            out_s[0], out_s[1], out_s[2], conv_s, sgv_s)
```
